```python
import math
import jax, jax.numpy as jnp
from jax import lax
import numpy as np

D_MODEL = 2048
BATCH = 2
SEQ = 4096
DEPTH = 4

A_HEAD_DIM = 128
A_WIDTH = D_MODEL // 2
A_HEADS = A_WIDTH // A_HEAD_DIM
DILATED_BRANCHES = ((128, 1), (512, 4), (2048, 16))
BAND_BLOCK = 128
B_CHANNELS = D_MODEL - A_WIDTH
CONV_WIDTH = 31
C_HEADS = 4
C_V_WIDTH = D_MODEL
C_V_DIM = C_V_WIDTH // C_HEADS
C_QK_DIM = C_V_DIM // 2
C_QK_WIDTH = C_HEADS * C_QK_DIM
C_CHUNK = 64
FFN_HIDDEN = 4 * D_MODEL
DEEPNORM_ALPHA = (2 * DEPTH) ** 0.25
DEEPNORM_BETA = (8 * DEPTH) ** -0.25
LN_EPS = 1e-5
N_EVEN = (DEPTH + 1) // 2
N_ODD = DEPTH // 2
EVEN_IN_WIDTH = 3 * A_WIDTH + 2 * B_CHANNELS
ODD_IN_WIDTH = 2 * C_QK_WIDTH + 2 * C_V_WIDTH + 2 * C_HEADS

kernel_name = "dilated_conv_mlstm_deepnorm_hybrid"


def layer_norm(x, g, b):
    xf = x.astype(jnp.float32)
    mu = jnp.mean(xf, axis=-1, keepdims=True)
    var = jnp.mean(jnp.square(xf - mu), axis=-1, keepdims=True)
    return ((xf - mu) * lax.rsqrt(var + LN_EPS) * g + b).astype(x.dtype)


def alibi_slopes(n):
    return 2.0 ** (-8.0 * jnp.arange(1, n + 1, dtype=jnp.float32) / n)


def dilated_branch(q, k, v, slopes, window, dil):
    bsz, seq, nh, hd = q.shape
    span = window // dil
    P = BAND_BLOCK
    ls = seq // dil
    nb = -(-ls // P)
    lp = nb * P

    def to_sub(t):
        t = t.reshape(bsz, ls, dil, nh, hd).transpose(0, 2, 1, 3, 4)
        t = jnp.pad(t, ((0, 0), (0, 0), (0, lp - ls), (0, 0), (0, 0)))
        return t.reshape(bsz, dil, nb, P, nh, hd)

    def with_prev(t):
        prev = jnp.pad(t, ((0, 0), (0, 0), (1, 0), (0, 0), (0, 0), (0, 0)))[:, :, :-1]
        return jnp.concatenate([prev, t], axis=3)

    qb = to_sub(q)
    kk = with_prev(to_sub(k))
    vv = with_prev(to_sub(v))
    s = jnp.einsum('brnqhe,brnkhe->brnhqk', qb, kk, preferred_element_type=jnp.float32)
    qi = jnp.arange(P)[:, None]
    ki = jnp.arange(2 * P)[None, :]
    dist = P + qi - ki
    blk = jnp.arange(nb)[:, None, None]
    valid = (dist >= 0) & (dist <= span) & (blk * P - P + ki >= 0)
    s = s - slopes[:, None, None] * (dist * dil).astype(jnp.float32)
    s = jnp.where(valid[:, None], s, -jnp.inf)
    lse = jax.nn.logsumexp(s, axis=-1)
    p = jnp.exp(s - lse[..., None])
    o = jnp.einsum('brnhqk,brnkhe->brnqhe', p, vv.astype(jnp.float32))

    def from_sub(t):
        t = t.reshape((bsz, dil, lp) + t.shape[4:])[:, :, :ls]
        t = jnp.moveaxis(t, 1, 2)
        return t.reshape((bsz, seq) + t.shape[3:])

    return from_sub(o), from_sub(lse.transpose(0, 1, 2, 4, 3))


def dilated_mixture_attention(q, k, v):
    slopes = alibi_slopes(A_HEADS)
    outs, lses = [], []
    for window, dil in DILATED_BRANCHES:
        o, l = dilated_branch(q, k, v, slopes, window, dil)
        outs.append(o)
        lses.append(l)
    wts = jax.nn.softmax(jnp.stack(lses), axis=0)
    return jnp.sum(wts[..., None] * jnp.stack(outs), axis=0)


def conformer_conv(u, conv_w, conv_b, ln_g, ln_b):
    a, g = jnp.split(u, 2, axis=-1)
    h = a * jax.nn.sigmoid(g)
    h = lax.conv_general_dilated(h, conv_w[:, None, :], window_strides=(1,),
                                 padding=((CONV_WIDTH - 1, 0),),
                                 dimension_numbers=('NWC', 'WIO', 'NWC'),
                                 feature_group_count=B_CHANNELS) + conv_b
    return jax.nn.silu(layer_norm(h, ln_g, ln_b))


def mlstm_chunkwise(q, k, v, ig, lf):
    bsz, nh, seq, dk = q.shape
    dv = v.shape[-1]
    nc = seq // C_CHUNK

    def chunks(t):
        return jnp.moveaxis(t.reshape((bsz, nh, nc, C_CHUNK) + t.shape[3:]), 2, 0)

    causal = jnp.tril(jnp.ones((C_CHUNK, C_CHUNK), dtype=bool))

    def step(carry, inp):
        c_st, n_st, m_st = carry
        qc, kc, vc, ic, fc = inp
        b = jnp.cumsum(fc, axis=-1)
        dmat = jnp.where(causal, b[..., :, None] - b[..., None, :] + ic[..., None, :], -jnp.inf)
        inter = b + m_st[..., None]
        m_t = jnp.maximum(inter, jnp.max(dmat, axis=-1))
        w = jnp.exp(dmat - m_t[..., None])
        a = jnp.exp(inter - m_t)
        sqk = jnp.einsum('bhtd,bhsd->bhts', qc, kc) * w
        num = a[..., None] * jnp.einsum('bhvd,bhtd->bhtv', c_st, qc) + jnp.einsum('bhts,bhsv->bhtv', sqk, vc)
        den = a * jnp.einsum('bhd,bhtd->bht', n_st, qc) + jnp.sum(sqk, axis=-1)
        h = num / jnp.maximum(jnp.abs(den), jnp.exp(-m_t))[..., None]
        b_last = b[..., -1]
        g = b_last[..., None] - b + ic
        m_new = jnp.maximum(b_last + m_st, jnp.max(g, axis=-1))
        decay = jnp.exp(b_last + m_st - m_new)
        ws = jnp.exp(g - m_new[..., None])
        c_new = decay[..., None, None] * c_st + jnp.einsum('bhs,bhsv,bhsd->bhvd', ws, vc, kc)
        n_new = decay[..., None] * n_st + jnp.einsum('bhs,bhsd->bhd', ws, kc)
        return (c_new, n_new, m_new), h

    init = (jnp.zeros((bsz, nh, dv, dk), jnp.float32),
            jnp.zeros((bsz, nh, dk), jnp.float32),
            jnp.zeros((bsz, nh), jnp.float32))
    _, hs = lax.scan(step, init, (chunks(q), chunks(k), chunks(v), chunks(ig), chunks(lf)))
    return jnp.moveaxis(hs, 0, 2).reshape(bsz, nh, seq, dv).transpose(0, 2, 1, 3)


def even_mixer(x, w_in, conv_w, conv_b, conv_ln_g, conv_ln_b, w_out):
    bsz, seq, _ = x.shape
    proj = x @ w_in
    q, k, v, u = jnp.split(proj, [A_WIDTH, 2 * A_WIDTH, 3 * A_WIDTH], axis=-1)
    q = q.reshape(bsz, seq, A_HEADS, A_HEAD_DIM) * (A_HEAD_DIM ** -0.5)
    k = k.reshape(bsz, seq, A_HEADS, A_HEAD_DIM)
    v = v.reshape(bsz, seq, A_HEADS, A_HEAD_DIM)
    att = dilated_mixture_attention(q, k, v).reshape(bsz, seq, A_WIDTH).astype(x.dtype)
    conv = conformer_conv(u, conv_w, conv_b, conv_ln_g, conv_ln_b)
    return jnp.concatenate([att, conv], axis=-1) @ w_out


def odd_mixer(x, w_in, igate_b, fgate_b, norm_g, w_out):
    bsz, seq, _ = x.shape
    proj = x @ w_in
    q, k, v, o, ig, fg = jnp.split(proj, [C_QK_WIDTH, 2 * C_QK_WIDTH, 2 * C_QK_WIDTH + C_V_WIDTH,
                                          2 * C_QK_WIDTH + 2 * C_V_WIDTH,
                                          2 * C_QK_WIDTH + 2 * C_V_WIDTH + C_HEADS], axis=-1)

    def to_heads(t, d):
        return t.reshape(bsz, seq, C_HEADS, d).transpose(0, 2, 1, 3).astype(jnp.float32)

    igf = (ig + igate_b).astype(jnp.float32).transpose(0, 2, 1)
    lf = jax.nn.log_sigmoid((fg + fgate_b).astype(jnp.float32)).transpose(0, 2, 1)
    h = mlstm_chunkwise(to_heads(q, C_QK_DIM) * (C_QK_DIM ** -0.5), to_heads(k, C_QK_DIM),
                        to_heads(v, C_V_DIM), igf, lf)
    mu = jnp.mean(h, axis=-1, keepdims=True)
    var = jnp.mean(jnp.square(h - mu), axis=-1, keepdims=True)
    hn = ((h - mu) * lax.rsqrt(var + LN_EPS)).reshape(bsz, seq, C_V_WIDTH) * norm_g
    y = (jax.nn.sigmoid(o.astype(jnp.float32)) * hn).astype(x.dtype)
    return y @ w_out


def setup_inputs(seed: int = 0) -> dict:
    key = jax.random.key(seed)
    ks = iter(jax.random.split(key, 32))

    def nrm(shape, scale):
        return scale * jax.random.normal(next(ks), shape, jnp.float32)

    D = D_MODEL
    x = nrm((BATCH, SEQ, D), 1.0)
    even_col = jnp.ones((EVEN_IN_WIDTH,), jnp.float32).at[2 * A_WIDTH:3 * A_WIDTH].set(DEEPNORM_BETA)
    even_w_in = nrm((N_EVEN, D, EVEN_IN_WIDTH), D ** -0.5) * even_col
    even_conv_w = nrm((N_EVEN, CONV_WIDTH, B_CHANNELS), CONV_WIDTH ** -0.5)
    even_conv_b = nrm((N_EVEN, B_CHANNELS), 0.02)
    even_conv_ln_g = 1.0 + nrm((N_EVEN, B_CHANNELS), 0.05)
    even_conv_ln_b = nrm((N_EVEN, B_CHANNELS), 0.02)
    even_w_out = nrm((N_EVEN, D, D), D ** -0.5 * DEEPNORM_BETA)
    odd_col = jnp.ones((ODD_IN_WIDTH,), jnp.float32).at[2 * C_QK_WIDTH:2 * C_QK_WIDTH + C_V_WIDTH].set(DEEPNORM_BETA)
    odd_w_in = nrm((N_ODD, D, ODD_IN_WIDTH), D ** -0.5) * odd_col
    odd_igate_b = nrm((N_ODD, C_HEADS), 0.1)
    odd_fgate_b = jnp.linspace(3.0, 6.0, C_HEADS, dtype=jnp.float32)[None, :] + nrm((N_ODD, C_HEADS), 0.1)
    odd_norm_g = 1.0 + nrm((N_ODD, C_V_WIDTH), 0.05)
    odd_w_out = nrm((N_ODD, C_V_WIDTH, D), C_V_WIDTH ** -0.5 * DEEPNORM_BETA)
    mix_ln_g = 1.0 + nrm((DEPTH, D), 0.05)
    mix_ln_b = nrm((DEPTH, D), 0.02)
    ffn_w1 = nrm((DEPTH, D, FFN_HIDDEN), D ** -0.5 * DEEPNORM_BETA)
    ffn_w2 = nrm((DEPTH, FFN_HIDDEN, D), FFN_HIDDEN ** -0.5 * DEEPNORM_BETA)
    ffn_ln_g = 1.0 + nrm((DEPTH, D), 0.05)
    ffn_ln_b = nrm((DEPTH, D), 0.02)
    return {"x": x, "even_w_in": even_w_in, "even_conv_w": even_conv_w, "even_conv_b": even_conv_b,
            "even_conv_ln_g": even_conv_ln_g, "even_conv_ln_b": even_conv_ln_b, "even_w_out": even_w_out,
            "odd_w_in": odd_w_in, "odd_igate_b": odd_igate_b, "odd_fgate_b": odd_fgate_b,
            "odd_norm_g": odd_norm_g, "odd_w_out": odd_w_out, "mix_ln_g": mix_ln_g, "mix_ln_b": mix_ln_b,
            "ffn_w1": ffn_w1, "ffn_w2": ffn_w2, "ffn_ln_g": ffn_ln_g, "ffn_ln_b": ffn_ln_b}


def reference(x, even_w_in, even_conv_w, even_conv_b, even_conv_ln_g, even_conv_ln_b, even_w_out,
              odd_w_in, odd_igate_b, odd_fgate_b, odd_norm_g, odd_w_out, mix_ln_g, mix_ln_b,
              ffn_w1, ffn_w2, ffn_ln_g, ffn_ln_b):
    for layer in range(DEPTH):
        j = layer // 2
        if layer % 2 == 0:
            mix = even_mixer(x, even_w_in[j], even_conv_w[j], even_conv_b[j],
                             even_conv_ln_g[j], even_conv_ln_b[j], even_w_out[j])
        else:
            mix = odd_mixer(x, odd_w_in[j], odd_igate_b[j], odd_fgate_b[j], odd_norm_g[j], odd_w_out[j])
        x = layer_norm(DEEPNORM_ALPHA * x + mix, mix_ln_g[layer], mix_ln_b[layer])
        hid = jnp.square(jax.nn.relu(x @ ffn_w1[layer]))
        x = layer_norm(DEEPNORM_ALPHA * x + hid @ ffn_w2[layer], ffn_ln_g[layer], ffn_ln_b[layer])
    return x
```

```python
import functools

import jax
import jax.numpy as jnp
from jax import lax
from jax.experimental import pallas as pl
from jax.experimental.pallas import tpu as pltpu

LN_EPS = 1e-5
DILATED_BRANCHES = ((128, 1), (512, 4), (2048, 16))
BAND_BLOCK = 128
A_HEAD_DIM = 128
CONV_WIDTH = 31
CONV_HALO = 32
C_HEADS = 4
MLSTM_CHUNK = 256
GATE_LANES = 128
V7X_VMEM_LIMIT = 56 * 1024 * 1024

BF16 = jnp.bfloat16
F32 = jnp.float32


def _params(semantics):
    return pltpu.CompilerParams(dimension_semantics=semantics, vmem_limit_bytes=V7X_VMEM_LIMIT)


def _layer_norm(z, g, b):
    mu = jnp.mean(z, axis=-1, keepdims=True)
    zc = z - mu
    var = jnp.mean(zc * zc, axis=-1, keepdims=True)
    return zc * lax.rsqrt(var + LN_EPS) * g + b


def _matmul_kernel(x_ref, w_ref, s_ref, o_ref):
    acc = jnp.dot(x_ref[...], w_ref[...], preferred_element_type=F32)
    o_ref[...] = (acc * s_ref[...]).astype(o_ref.dtype)


def _matmul(x, w, col_scale, out_dtype, tm=1024, tn=1024):
    m, k = x.shape
    n = w.shape[1]
    return pl.pallas_call(
        _matmul_kernel,
        grid=(m // tm, n // tn),
        in_specs=[pl.BlockSpec((tm, k), lambda i, j: (i, 0)),
                  pl.BlockSpec((k, tn), lambda i, j: (0, j)),
                  pl.BlockSpec((1, tn), lambda i, j: (0, j))],
        out_specs=pl.BlockSpec((tm, tn), lambda i, j: (i, j)),
        out_shape=jax.ShapeDtypeStruct((m, n), out_dtype),
        compiler_params=_params(("parallel", "arbitrary")),
        name="proj_matmul",
    )(x, w, col_scale)


def _out_proj_ln_kernel(alpha, n_in, *refs):
    ys = refs[:n_in]
    ws = refs[n_in:2 * n_in]
    x_ref, g_ref, b_ref, of_ref, ob_ref = refs[2 * n_in:]
    acc = alpha * x_ref[...]
    for y_ref, w_ref in zip(ys, ws):
        acc = acc + jnp.dot(y_ref[...], w_ref[...], preferred_element_type=F32)
    out = _layer_norm(acc, g_ref[...], b_ref[...])
    of_ref[...] = out
    ob_ref[...] = out.astype(BF16)


def _out_proj_ln(ys, ws, x, g, b, alpha, tm=512):
    m, d = x.shape
    n_in = len(ys)
    in_specs = ([pl.BlockSpec((tm, y.shape[1]), lambda i: (i, 0)) for y in ys]
                + [pl.BlockSpec(w.shape, lambda i: (0, 0)) for w in ws]
                + [pl.BlockSpec((tm, d), lambda i: (i, 0)),
                   pl.BlockSpec((1, d), lambda i: (0, 0)),
                   pl.BlockSpec((1, d), lambda i: (0, 0))])
    return pl.pallas_call(
        functools.partial(_out_proj_ln_kernel, alpha, n_in),
        grid=(m // tm,),
        in_specs=in_specs,
        out_specs=[pl.BlockSpec((tm, d), lambda i: (i, 0)),
                   pl.BlockSpec((tm, d), lambda i: (i, 0))],
        out_shape=[jax.ShapeDtypeStruct((m, d), F32), jax.ShapeDtypeStruct((m, d), BF16)],
        compiler_params=_params(("parallel",)),
        name="out_proj_ln",
    )(*ys, *ws, x, g, b)


def _ffn_kernel(alpha, xb_ref, w1_ref, w2_ref, x_ref, g_ref, b_ref, of_ref, ob_ref, acc_ref):
    j = pl.program_id(1)

    @pl.when(j == 0)
    def _():
        acc_ref[...] = alpha * x_ref[...]

    h = jnp.dot(xb_ref[...], w1_ref[...], preferred_element_type=F32)
    h = jnp.maximum(h, 0.0)
    h = (h * h).astype(BF16)
    acc_ref[...] += jnp.dot(h, w2_ref[...], preferred_element_type=F32)

    @pl.when(j == pl.num_programs(1) - 1)
    def _():
        out = _layer_norm(acc_ref[...], g_ref[...], b_ref[...])
        of_ref[...] = out
        ob_ref[...] = out.astype(BF16)


def _ffn_ln(xb, x, w1, w2, g, b, alpha, tm=512, th=512):
    m, d = x.shape
    f = w1.shape[1]
    return pl.pallas_call(
        functools.partial(_ffn_kernel, alpha),
        grid=(m // tm, f // th),
        in_specs=[pl.BlockSpec((tm, d), lambda i, j: (i, 0)),
                  pl.BlockSpec((d, th), lambda i, j: (0, j)),
                  pl.BlockSpec((th, d), lambda i, j: (j, 0)),
                  pl.BlockSpec((tm, d), lambda i, j: (i, 0)),
                  pl.BlockSpec((1, d), lambda i, j: (0, 0)),
                  pl.BlockSpec((1, d), lambda i, j: (0, 0))],
        out_specs=[pl.BlockSpec((tm, d), lambda i, j: (i, 0)),
                   pl.BlockSpec((tm, d), lambda i, j: (i, 0))],
        out_shape=[jax.ShapeDtypeStruct((m, d), F32), jax.ShapeDtypeStruct((m, d), BF16)],
        scratch_shapes=[pltpu.VMEM((tm, d), F32)],
        compiler_params=_params(("parallel", "arbitrary")),
        name="ffn_ln",
    )(xb, w1, w2, x, g, b)


def _dilated_branch_kernel(dil, n_heads, q_ref, kp_ref, kc_ref, vp_ref, vc_ref, o_ref, lse_ref):
    n = pl.program_id(2)
    p = BAND_BLOCK
    qi = lax.broadcasted_iota(jnp.int32, (p, p), 0)
    ki = lax.broadcasted_iota(jnp.int32, (p, p), 1)
    dist_cur = (qi - ki).astype(F32) * float(dil)
    dist_prev = (p + qi - ki).astype(F32) * float(dil)
    valid_cur = qi >= ki
    valid_prev = jnp.logical_and(ki >= qi, n > 0)
    lane = lax.broadcasted_iota(jnp.int32, (p, GATE_LANES), 1)
    lse_tile = jnp.zeros((p, GATE_LANES), F32)
    dn = (((1,), (1,)), ((), ()))
    for h in range(n_heads):
        slope = 2.0 ** (-8.0 * (h + 1) / n_heads)
        sl = slice(h * A_HEAD_DIM, (h + 1) * A_HEAD_DIM)
        q = q_ref[0, :, sl]
        s_cur = lax.dot_general(q, kc_ref[0, :, sl], dn, preferred_element_type=F32)
        s_prev = lax.dot_general(q, kp_ref[0, :, sl], dn, preferred_element_type=F32)
        s_cur = jnp.where(valid_cur, s_cur - slope * dist_cur, -jnp.inf)
        s_prev = jnp.where(valid_prev, s_prev - slope * dist_prev, -jnp.inf)
        m = jnp.maximum(jnp.max(s_cur, axis=-1, keepdims=True), jnp.max(s_prev, axis=-1, keepdims=True))
        p_cur = jnp.exp(s_cur - m)
        p_prev = jnp.exp(s_prev - m)
        l = jnp.sum(p_cur, axis=-1, keepdims=True) + jnp.sum(p_prev, axis=-1, keepdims=True)
        o = (jnp.dot(p_cur.astype(BF16), vc_ref[0, :, sl], preferred_element_type=F32)
             + jnp.dot(p_prev.astype(BF16), vp_ref[0, :, sl], preferred_element_type=F32))
        o_ref[0, :, sl] = o / l
        lse_tile = jnp.where(lane == h, m + jnp.log(l), lse_tile)
    lse_ref[0] = lse_tile


def _dilated_branch(qkv, bsz, seq, dil, n_heads):
    width = n_heads * A_HEAD_DIM
    ls = seq // dil
    nb = ls // BAND_BLOCK
    p = BAND_BLOCK
    qkv_v = qkv.reshape(bsz, ls, dil * 3 * width)

    def spec(col, prev):
        if prev:
            return pl.BlockSpec((1, p, width), lambda b, r, n: (b, jnp.maximum(n - 1, 0), 3 * r + col))
        return pl.BlockSpec((1, p, width), lambda b, r, n: (b, n, 3 * r + col))

    o, lse = pl.pallas_call(
        functools.partial(_dilated_branch_kernel, dil, n_heads),
        grid=(bsz, dil, nb),
        in_specs=[spec(0, False), spec(1, True), spec(1, False), spec(2, True), spec(2, False)],
        out_specs=[pl.BlockSpec((1, p, width), lambda b, r, n: (b, n, r)),
                   pl.BlockSpec((1, p, GATE_LANES), lambda b, r, n: (b, n, r))],
        out_shape=[jax.ShapeDtypeStruct((bsz, ls, dil * width), F32),
                   jax.ShapeDtypeStruct((bsz, ls, dil * GATE_LANES), F32)],
        compiler_params=_params(("parallel", "parallel", "arbitrary")),
        name=f"dilated_branch_d{dil}",
    )(qkv_v, qkv_v, qkv_v, qkv_v, qkv_v)
    return o.reshape(bsz * seq, width), lse.reshape(bsz * seq, GATE_LANES)


def _branch_mix_kernel(n_heads, o1_ref, o2_ref, o3_ref, l1_ref, l2_ref, l3_ref, out_ref):
    l1, l2, l3 = l1_ref[...], l2_ref[...], l3_ref[...]
    mx = jnp.maximum(jnp.maximum(l1, l2), l3)
    e1, e2, e3 = jnp.exp(l1 - mx), jnp.exp(l2 - mx), jnp.exp(l3 - mx)
    inv = 1.0 / (e1 + e2 + e3)
    w1, w2, w3 = e1 * inv, e2 * inv, e3 * inv
    for h in range(n_heads):
        sl = slice(h * A_HEAD_DIM, (h + 1) * A_HEAD_DIM)
        mix = (w1[:, h:h + 1] * o1_ref[:, sl] + w2[:, h:h + 1] * o2_ref[:, sl]
               + w3[:, h:h + 1] * o3_ref[:, sl])
        out_ref[:, sl] = mix.astype(out_ref.dtype)


def _branch_mix(outs, lses, n_heads, tm=512):
    m, width = outs[0].shape
    return pl.pallas_call(
        functools.partial(_branch_mix_kernel, n_heads),
        grid=(m // tm,),
        in_specs=[pl.BlockSpec((tm, width), lambda i: (i, 0))] * 3
        + [pl.BlockSpec((tm, GATE_LANES), lambda i: (i, 0))] * 3,
        out_specs=pl.BlockSpec((tm, width), lambda i: (i, 0)),
        out_shape=jax.ShapeDtypeStruct((m, width), BF16),
        compiler_params=_params(("parallel",)),
        name="branch_mix",
    )(*outs, *lses)


def _conv_kernel(ts, ch, u_ref, halo_ref, w_ref, cb_ref, g_ref, b_ref, o_ref, h_ref, c_ref):
    t = pl.program_id(1)
    halo = halo_ref[0]
    hist = halo[:, :ch] * jax.nn.sigmoid(halo[:, ch:])
    h_ref[0:CONV_HALO, :] = jnp.where(t > 0, hist, 0.0)
    u = u_ref[0]
    h_ref[CONV_HALO:, :] = u[:, :ch] * jax.nn.sigmoid(u[:, ch:])
    rows, lanes = 32, 512
    first = CONV_HALO - (CONV_WIDTH - 1)
    for r0 in range(0, ts, rows):
        for c0 in range(0, ch, lanes):
            acc = jnp.broadcast_to(cb_ref[:, c0:c0 + lanes], (rows, lanes))
            for j in range(CONV_WIDTH):
                acc = acc + w_ref[j:j + 1, c0:c0 + lanes] * h_ref[r0 + first + j:r0 + first + j + rows, c0:c0 + lanes]
            c_ref[r0:r0 + rows, c0:c0 + lanes] = acc
    y = _layer_norm(c_ref[...], g_ref[...], b_ref[...])
    o_ref[0] = (y * jax.nn.sigmoid(y)).astype(o_ref.dtype)


def _conformer_conv(u, bsz, seq, conv_w, conv_b, ln_g, ln_b, ts=256):
    ch = u.shape[1] // 2
    u3 = u.reshape(bsz, seq, 2 * ch)
    w_pad = jnp.zeros((CONV_HALO, ch), F32).at[:CONV_WIDTH].set(conv_w)
    per = ts // CONV_HALO
    out = pl.pallas_call(
        functools.partial(_conv_kernel, ts, ch),
        grid=(bsz, seq // ts),
        in_specs=[pl.BlockSpec((1, ts, 2 * ch), lambda b, t: (b, t, 0)),
                  pl.BlockSpec((1, CONV_HALO, 2 * ch), lambda b, t: (b, jnp.maximum(t * per - 1, 0), 0)),
                  pl.BlockSpec((CONV_HALO, ch), lambda b, t: (0, 0)),
                  pl.BlockSpec((1, ch), lambda b, t: (0, 0)),
                  pl.BlockSpec((1, ch), lambda b, t: (0, 0)),
                  pl.BlockSpec((1, ch), lambda b, t: (0, 0))],
        out_specs=pl.BlockSpec((1, ts, ch), lambda b, t: (b, t, 0)),
        out_shape=jax.ShapeDtypeStruct((bsz, seq, ch), BF16),
        scratch_shapes=[pltpu.VMEM((ts + CONV_HALO, ch), F32), pltpu.VMEM((ts, ch), F32)],
        compiler_params=_params(("parallel", "arbitrary")),
        name="conformer_conv",
    )(u3, u3, w_pad, conv_b.reshape(1, ch), ln_g.reshape(1, ch), ln_b.reshape(1, ch))
    return out.reshape(bsz * seq, ch)


def _gates_kernel(x_ref, w_ref, bias_ref, gc_ref, gr_ref):
    length = x_ref.shape[0]
    pre = jnp.dot(x_ref[...], w_ref[...], preferred_element_type=F32,
                  precision=lax.Precision.HIGHEST) + bias_ref[...]
    log_f = jnp.minimum(pre, 0.0) - jnp.log1p(jnp.exp(-jnp.abs(pre)))
    ti = lax.broadcasted_iota(jnp.int32, (length, length), 0)
    si = lax.broadcasted_iota(jnp.int32, (length, length), 1)
    tril = (ti >= si).astype(F32)
    cum_f = jnp.dot(tril, log_f, preferred_element_type=F32, precision=lax.Precision.HIGHEST)
    lane = lax.broadcasted_iota(jnp.int32, pre.shape, 1)
    gates = jnp.where(lane < C_HEADS, pre, cum_f)
    gc_ref[...] = gates
    gr_ref[...] = gates.T


def _mlstm_gates(x, w_gate, bias):
    m, d = x.shape
    length = MLSTM_CHUNK
    return pl.pallas_call(
        _gates_kernel,
        grid=(m // length,),
        in_specs=[pl.BlockSpec((length, d), lambda i: (i, 0)),
                  pl.BlockSpec((d, GATE_LANES), lambda i: (0, 0)),
                  pl.BlockSpec((1, GATE_LANES), lambda i: (0, 0))],
        out_specs=[pl.BlockSpec((length, GATE_LANES), lambda i: (i, 0)),
                   pl.BlockSpec((GATE_LANES, length), lambda i: (0, i))],
        out_shape=[jax.ShapeDtypeStruct((m, GATE_LANES), F32),
                   jax.ShapeDtypeStruct((GATE_LANES, m), F32)],
        compiler_params=_params(("parallel",)),
        name="mlstm_gates",
    )(x, w_gate, bias)


def _mlstm_kernel(dk, dv, q_ref, k_ref, v_ref, o_ref, gc_ref, gr_ref, ng_ref, y_ref, c_ref, n_ref, m_ref):
    length = q_ref.shape[0]

    @pl.when(pl.program_id(1) == 0)
    def _():
        c_ref[...] = jnp.zeros_like(c_ref)
        n_ref[...] = jnp.zeros_like(n_ref)
        m_ref[...] = jnp.zeros_like(m_ref)

    ti = lax.broadcasted_iota(jnp.int32, (length, length), 0)
    si = lax.broadcasted_iota(jnp.int32, (length, length), 1)
    causal = ti >= si
    for h in range(C_HEADS):
        q = q_ref[:, h * dk:(h + 1) * dk]
        k = k_ref[:, h * dk:(h + 1) * dk]
        v = v_ref[:, h * dv:(h + 1) * dv]
        i_row = gr_ref[h:h + 1, :]
        b_row = gr_ref[C_HEADS + h:C_HEADS + h + 1, :]
        i_col = gc_ref[:, h:h + 1]
        b_col = gc_ref[:, C_HEADS + h:C_HEADS + h + 1]
        m_prev = m_ref[h:h + 1, 0:1]
        c_prev = c_ref[h]
        n_prev = n_ref[h:h + 1, :]

        dmat = jnp.where(causal, b_col - b_row + i_row, -jnp.inf)
        inter = b_col + m_prev
        m_t = jnp.maximum(inter, jnp.max(dmat, axis=-1, keepdims=True))
        w = jnp.exp(dmat - m_t)
        a = jnp.exp(inter - m_t)
        qk = lax.dot_general(q, k, (((1,), (1,)), ((), ())), preferred_element_type=F32)
        sqk = qk * w
        num = (a * jnp.dot(q, c_prev.astype(BF16), preferred_element_type=F32)
               + jnp.dot(sqk.astype(BF16), v, preferred_element_type=F32))
        den = (a * jnp.sum(q.astype(F32) * n_prev, axis=-1, keepdims=True)
               + jnp.sum(sqk, axis=-1, keepdims=True))
        hh = num / jnp.maximum(jnp.abs(den), jnp.exp(-m_t))

        b_last = b_col[length - 1:length, :]
        g = b_last - b_col + i_col
        m_new = jnp.maximum(b_last + m_prev, jnp.max(g, axis=0, keepdims=True))
        decay = jnp.exp(b_last + m_prev - m_new)
        kw = k.astype(F32) * jnp.exp(g - m_new)
        c_ref[h] = decay * c_prev + lax.dot_general(kw.astype(BF16), v, (((0,), (0,)), ((), ())),
                                                    preferred_element_type=F32)
        n_ref[h:h + 1, :] = decay * n_prev + jnp.sum(kw, axis=0, keepdims=True)
        m_ref[h:h + 1, :] = jnp.broadcast_to(m_new, (1, m_ref.shape[1]))

        mu = jnp.mean(hh, axis=-1, keepdims=True)
        hc = hh - mu
        var = jnp.mean(hc * hc, axis=-1, keepdims=True)
        hn = hc * lax.rsqrt(var + LN_EPS) * ng_ref[:, h * dv:(h + 1) * dv]
        y_ref[:, h * dv:(h + 1) * dv] = (jax.nn.sigmoid(o_ref[:, h * dv:(h + 1) * dv]) * hn).astype(y_ref.dtype)


def _mlstm(qkv, o, gates_col, gates_row, norm_g, bsz, seq, dk, dv):
    length = MLSTM_CHUNK
    nc = seq // length
    qk_w = C_HEADS * dk
    v_w = C_HEADS * dv
    v_blk = (2 * qk_w) // v_w
    return pl.pallas_call(
        functools.partial(_mlstm_kernel, dk, dv),
        grid=(bsz, nc),
        in_specs=[pl.BlockSpec((length, qk_w), lambda b, c: (b * nc + c, 0)),
                  pl.BlockSpec((length, qk_w), lambda b, c: (b * nc + c, 1)),
                  pl.BlockSpec((length, v_w), lambda b, c: (b * nc + c, v_blk)),
                  pl.BlockSpec((length, v_w), lambda b, c: (b * nc + c, 0)),
                  pl.BlockSpec((length, GATE_LANES), lambda b, c: (b * nc + c, 0)),
                  pl.BlockSpec((GATE_LANES, length), lambda b, c: (0, b * nc + c)),
                  pl.BlockSpec((1, v_w), lambda b, c: (0, 0))],
        out_specs=pl.BlockSpec((length, v_w), lambda b, c: (b * nc + c, 0)),
        out_shape=jax.ShapeDtypeStruct((bsz * seq, v_w), BF16),
        scratch_shapes=[pltpu.VMEM((C_HEADS, dk, dv), F32),
                        pltpu.VMEM((8, dk), F32),
                        pltpu.VMEM((8, GATE_LANES), F32)],
        compiler_params=_params(("parallel", "arbitrary")),
        name="mlstm_chunk",
    )(qkv, qkv, qkv, o, gates_col, gates_row, norm_g)


def _even_mixer(xb, x, w_in, conv_w, conv_b, conv_ln_g, conv_ln_b, w_out, ln_g, ln_b, alpha, bsz, seq):
    d = x.shape[1]
    a_width = d // 2
    n_heads = a_width // A_HEAD_DIM
    w_in = w_in.astype(BF16)
    qkv_scale = jnp.ones((1, 3 * a_width), F32).at[:, :a_width].set(A_HEAD_DIM ** -0.5)
    qkv = _matmul(xb, w_in[:, :3 * a_width], qkv_scale, BF16)
    u = _matmul(xb, w_in[:, 3 * a_width:], jnp.ones((1, w_in.shape[1] - 3 * a_width), F32), F32)
    outs, lses = [], []
    for _, dil in DILATED_BRANCHES:
        o, lse = _dilated_branch(qkv, bsz, seq, dil, n_heads)
        outs.append(o)
        lses.append(lse)
    att = _branch_mix(outs, lses, n_heads)
    conv = _conformer_conv(u, bsz, seq, conv_w, conv_b, conv_ln_g, conv_ln_b)
    w_out = w_out.astype(BF16)
    return _out_proj_ln([att, conv], [w_out[:a_width], w_out[a_width:]], x,
                        ln_g.reshape(1, d), ln_b.reshape(1, d), alpha)


def _odd_mixer(xb, x, w_in, igate_b, fgate_b, norm_g, w_out, ln_g, ln_b, alpha, bsz, seq):
    d = x.shape[1]
    v_w = d
    dv = v_w // C_HEADS
    dk = dv // 2
    qk_w = C_HEADS * dk
    main = 2 * qk_w + 2 * v_w
    w_main = w_in[:, :main].astype(BF16)
    qkv_scale = jnp.ones((1, 2 * qk_w + v_w), F32).at[:, :qk_w].set(dk ** -0.5)
    qkv = _matmul(xb, w_main[:, :2 * qk_w + v_w], qkv_scale, BF16)
    o = _matmul(xb, w_main[:, 2 * qk_w + v_w:], jnp.ones((1, v_w), F32), F32)
    w_gate = jnp.zeros((d, GATE_LANES), F32).at[:, :2 * C_HEADS].set(w_in[:, main:])
    bias = jnp.zeros((1, GATE_LANES), F32).at[0, :C_HEADS].set(igate_b).at[0, C_HEADS:2 * C_HEADS].set(fgate_b)
    gates_col, gates_row = _mlstm_gates(x, w_gate, bias)
    y = _mlstm(qkv, o, gates_col, gates_row, norm_g.reshape(1, v_w), bsz, seq, dk, dv)
    return _out_proj_ln([y], [w_out.astype(BF16)], x, ln_g.reshape(1, d), ln_b.reshape(1, d), alpha)


def kernel(x, even_w_in, even_conv_w, even_conv_b, even_conv_ln_g, even_conv_ln_b, even_w_out,
           odd_w_in, odd_igate_b, odd_fgate_b, odd_norm_g, odd_w_out, mix_ln_g, mix_ln_b,
           ffn_w1, ffn_w2, ffn_ln_g, ffn_ln_b):
    bsz, seq, d = x.shape
    depth = mix_ln_g.shape[0]
    alpha = (2 * depth) ** 0.25
    xf = x.reshape(bsz * seq, d)
    xb = xf.astype(BF16)
    for layer in range(depth):
        j = layer // 2
        if layer % 2 == 0:
            xf, xb = _even_mixer(xb, xf, even_w_in[j], even_conv_w[j], even_conv_b[j], even_conv_ln_g[j],
                                 even_conv_ln_b[j], even_w_out[j], mix_ln_g[layer], mix_ln_b[layer],
                                 alpha, bsz, seq)
        else:
            xf, xb = _odd_mixer(xb, xf, odd_w_in[j], odd_igate_b[j], odd_fgate_b[j], odd_norm_g[j],
                                odd_w_out[j], mix_ln_g[layer], mix_ln_b[layer], alpha, bsz, seq)
        xf, xb = _ffn_ln(xb, xf, ffn_w1[layer].astype(BF16), ffn_w2[layer].astype(BF16),
                         ffn_ln_g[layer].reshape(1, d), ffn_ln_b[layer].reshape(1, d), alpha)
    return xf.reshape(bsz, seq, d)
```

```python
import functools

import jax
import jax.numpy as jnp
from jax import lax
from jax.experimental import pallas as pl
from jax.experimental.pallas import tpu as pltpu

LN_EPS = 1e-5
DILATED_BRANCHES = ((128, 1), (512, 4), (2048, 16))
BAND_BLOCK = 128
A_HEAD_DIM = 128
ATT_TILE = 2048
ATT_UNROLL = 8
CONV_WIDTH = 31
CONV_HALO = 32
C_HEADS = 4
MLSTM_CHUNK = 256
GATE_LANES = 128
V7X_VMEM_LIMIT = 56 * 1024 * 1024

BF16 = jnp.bfloat16
F32 = jnp.float32


def _params(semantics):
    return pltpu.CompilerParams(dimension_semantics=semantics, vmem_limit_bytes=V7X_VMEM_LIMIT)


def _layer_norm(z, g, b):
    mu = jnp.mean(z, axis=-1, keepdims=True)
    zc = z - mu
    var = jnp.mean(zc * zc, axis=-1, keepdims=True)
    return zc * lax.rsqrt(var + LN_EPS) * g + b


def _matmul_kernel(x_ref, w_ref, s_ref, o_ref):
    acc = jnp.dot(x_ref[...], w_ref[...], preferred_element_type=F32)
    o_ref[...] = (acc * s_ref[...]).astype(o_ref.dtype)


def _matmul(x, w, col_scale, out_dtype, tm=1024, tn=1024):
    m, k = x.shape
    n = w.shape[1]
    return pl.pallas_call(
        _matmul_kernel,
        grid=(m // tm, n // tn),
        in_specs=[pl.BlockSpec((tm, k), lambda i, j: (i, 0)),
                  pl.BlockSpec((k, tn), lambda i, j: (0, j)),
                  pl.BlockSpec((1, tn), lambda i, j: (0, j))],
        out_specs=pl.BlockSpec((tm, tn), lambda i, j: (i, j)),
        out_shape=jax.ShapeDtypeStruct((m, n), out_dtype),
        compiler_params=_params(("parallel", "arbitrary")),
        name="proj_matmul",
    )(x, w, col_scale)


def _out_proj_ln_kernel(alpha, n_in, *refs):
    ys = refs[:n_in]
    ws = refs[n_in:2 * n_in]
    x_ref, g_ref, b_ref, of_ref, ob_ref = refs[2 * n_in:]
    acc = alpha * x_ref[...]
    for y_ref, w_ref in zip(ys, ws):
        acc = acc + jnp.dot(y_ref[...], w_ref[...], preferred_element_type=F32)
    out = _layer_norm(acc, g_ref[...], b_ref[...])
    of_ref[...] = out
    ob_ref[...] = out.astype(BF16)


def _out_proj_ln(ys, ws, x, g, b, alpha, tm=512):
    m, d = x.shape
    n_in = len(ys)
    in_specs = ([pl.BlockSpec((tm, y.shape[1]), lambda i: (i, 0)) for y in ys]
                + [pl.BlockSpec(w.shape, lambda i: (0, 0)) for w in ws]
                + [pl.BlockSpec((tm, d), lambda i: (i, 0)),
                   pl.BlockSpec((1, d), lambda i: (0, 0)),
                   pl.BlockSpec((1, d), lambda i: (0, 0))])
    return pl.pallas_call(
        functools.partial(_out_proj_ln_kernel, alpha, n_in),
        grid=(m // tm,),
        in_specs=in_specs,
        out_specs=[pl.BlockSpec((tm, d), lambda i: (i, 0)),
                   pl.BlockSpec((tm, d), lambda i: (i, 0))],
        out_shape=[jax.ShapeDtypeStruct((m, d), F32), jax.ShapeDtypeStruct((m, d), BF16)],
        compiler_params=_params(("parallel",)),
        name="out_proj_ln",
    )(*ys, *ws, x, g, b)


def _ffn_kernel(alpha, xb_ref, w1_ref, w2_ref, x_ref, g_ref, b_ref, of_ref, ob_ref, acc_ref):
    j = pl.program_id(1)

    @pl.when(j == 0)
    def _():
        acc_ref[...] = alpha * x_ref[...]

    h = jnp.dot(xb_ref[...], w1_ref[...], preferred_element_type=F32)
    h = jnp.maximum(h, 0.0)
    h = (h * h).astype(BF16)
    acc_ref[...] += jnp.dot(h, w2_ref[...], preferred_element_type=F32)

    @pl.when(j == pl.num_programs(1) - 1)
    def _():
        out = _layer_norm(acc_ref[...], g_ref[...], b_ref[...])
        of_ref[...] = out
        ob_ref[...] = out.astype(BF16)


def _ffn_ln(xb, x, w1, w2, g, b, alpha, tm=512, th=512):
    m, d = x.shape
    f = w1.shape[1]
    return pl.pallas_call(
        functools.partial(_ffn_kernel, alpha),
        grid=(m // tm, f // th),
        in_specs=[pl.BlockSpec((tm, d), lambda i, j: (i, 0)),
                  pl.BlockSpec((d, th), lambda i, j: (0, j)),
                  pl.BlockSpec((th, d), lambda i, j: (j, 0)),
                  pl.BlockSpec((tm, d), lambda i, j: (i, 0)),
                  pl.BlockSpec((1, d), lambda i, j: (0, 0)),
                  pl.BlockSpec((1, d), lambda i, j: (0, 0))],
        out_specs=[pl.BlockSpec((tm, d), lambda i, j: (i, 0)),
                   pl.BlockSpec((tm, d), lambda i, j: (i, 0))],
        out_shape=[jax.ShapeDtypeStruct((m, d), F32), jax.ShapeDtypeStruct((m, d), BF16)],
        scratch_shapes=[pltpu.VMEM((tm, d), F32)],
        compiler_params=_params(("parallel", "arbitrary")),
        name="ffn_ln",
    )(xb, w1, w2, x, g, b)


def _kv_rows(dil):
    return dil * BAND_BLOCK + ATT_TILE


def _attn_kernel(q_ref, k_ref, v_ref, slope_ref, out_ref,
                 g4_ref, qs_ref, ks_ref, vs_ref, o_ref, l_ref, bias_ref, s_ref, e_ref):
    tile = pl.program_id(2)
    p = BAND_BLOCK
    t_len = ATT_TILE
    qi = lax.broadcasted_iota(jnp.int32, (p, 2 * p), 0)
    kj = lax.broadcasted_iota(jnp.int32, (p, 2 * p), 1)
    dist = p + qi - kj
    valid = jnp.logical_and(dist >= 0, dist <= p)
    in_prev_block = kj < p
    slope = slope_ref[0]
    for bi, (_, dil) in enumerate(DILATED_BRANCHES):
        bias_ref[bi] = jnp.where(valid, -(slope * (dist * dil).astype(F32)), -jnp.inf)

    def rows(start, size, dil):
        return pl.ds(start, size, stride=dil) if dil > 1 else pl.ds(start, size)

    kv_base = [sum(_kv_rows(d) for _, d in DILATED_BRANCHES[:bi]) for bi in range(len(DILATED_BRANCHES))]
    srcs = (q_ref, k_ref, v_ref)
    dil4 = DILATED_BRANCHES[1][1]
    w4 = t_len // dil4
    for x, src in enumerate(srcs):
        for r in range(dil4):
            g4_ref[x, r * w4:(r + 1) * w4, :] = src[rows(r, w4, dil4), :]

    def residue_rows(x, r, dil):
        if dil == 1:
            return srcs[x][...]
        if dil == dil4:
            return g4_ref[x, r * w4:(r + 1) * w4, :]
        return g4_ref[x, rows((r % dil4) * w4 + r // dil4, t_len // dil, dil // dil4), :]

    nt = (((1,), (1,)), ((), ()))
    for bi, (_, dil) in enumerate(DILATED_BRANCHES):
        wq = t_len // dil
        nq = wq // p
        wk = p + wq
        base = kv_base[bi]

        @pl.when(tile == 0)
        def _(dil=dil, wk=wk, base=base):
            for r in range(dil):
                for dst_ref in (ks_ref, vs_ref):
                    dst_ref[base + r * wk:base + r * wk + p, :] = jnp.zeros((p, A_HEAD_DIM), BF16)

        @pl.when(tile > 0)
        def _(dil=dil, wk=wk, wq=wq, base=base):
            for r in range(dil):
                for dst_ref in (ks_ref, vs_ref):
                    dst_ref[base + r * wk:base + r * wk + p, :] = dst_ref[base + r * wk + wq:base + (r + 1) * wk, :]

        for r in range(dil):
            qs_ref[r * wq:(r + 1) * wq, :] = residue_rows(0, r, dil).astype(BF16)
            ks_ref[base + r * wk + p:base + (r + 1) * wk, :] = residue_rows(1, r, dil).astype(BF16)
            vs_ref[base + r * wk + p:base + (r + 1) * wk, :] = residue_rows(2, r, dil).astype(BF16)

        def group(g, carry, dil=dil, nq=nq, wk=wk, bi=bi, base=base):
            starts = []
            for u in range(ATT_UNROLL):
                idx = g * ATT_UNROLL + u
                r = idx // nq
                n = idx - r * nq
                q0 = pl.multiple_of(idx * p, p)
                k0 = pl.multiple_of(base + r * wk + n * p, p)
                s = lax.dot_general(qs_ref[pl.ds(q0, p), :], ks_ref[pl.ds(k0, 2 * p), :], nt,
                                    preferred_element_type=F32) + bias_ref[bi]
                no_history = jnp.logical_and(tile == 0, n == 0)
                s_ref[u * p:(u + 1) * p, :] = jnp.where(jnp.logical_and(no_history, in_prev_block), -jnp.inf, s)
                starts.append((k0, r + dil * p * n))
            s = s_ref[...]
            m = jnp.max(s, axis=-1, keepdims=True)
            e = jnp.exp(s - m)
            l = jnp.sum(e, axis=-1, keepdims=True)
            e_ref[...] = e.astype(BF16)
            inv_l = 1.0 / l
            lse = m + jnp.log(l)
            for u, (k0, t0) in enumerate(starts):
                o = jnp.dot(e_ref[u * p:(u + 1) * p, :], vs_ref[pl.ds(k0, 2 * p), :], preferred_element_type=F32)
                o_ref[bi, rows(t0, p, dil), :] = o * inv_l[u * p:(u + 1) * p]
                l_ref[bi, rows(t0, p, dil), :] = jnp.broadcast_to(lse[u * p:(u + 1) * p], (p, A_HEAD_DIM))
            return carry

        lax.fori_loop(0, (dil * nq) // ATT_UNROLL, group, 0)

    chunk = 256
    for c0 in range(0, t_len, chunk):
        sl = slice(c0, c0 + chunk)
        l1, l2, l3 = l_ref[0, sl, :], l_ref[1, sl, :], l_ref[2, sl, :]
        mx = jnp.maximum(jnp.maximum(l1, l2), l3)
        e1, e2, e3 = jnp.exp(l1 - mx), jnp.exp(l2 - mx), jnp.exp(l3 - mx)
        mix = (e1 * o_ref[0, sl, :] + e2 * o_ref[1, sl, :] + e3 * o_ref[2, sl, :]) / (e1 + e2 + e3)
        out_ref[sl, :] = mix.astype(out_ref.dtype)


def _dilated_mixture_attention(proj, bsz, seq, n_heads):
    m = proj.shape[0]
    hd = A_HEAD_DIM
    t_len = ATT_TILE
    nt = seq // t_len
    slopes = 2.0 ** (-8.0 * jnp.arange(1, n_heads + 1, dtype=F32) / n_heads)
    slopes = jnp.broadcast_to(slopes[:, None, None], (n_heads, 1, 2 * BAND_BLOCK))

    kv_rows = sum(_kv_rows(dil) for _, dil in DILATED_BRANCHES)

    def head_block(col):
        return pl.BlockSpec((t_len, hd), lambda b, h, t: (b * nt + t, col * n_heads + h))

    return pl.pallas_call(
        _attn_kernel,
        grid=(bsz, n_heads, nt),
        in_specs=[head_block(0), head_block(1), head_block(2),
                  pl.BlockSpec((1, 1, 2 * BAND_BLOCK), lambda b, h, t: (h, 0, 0))],
        out_specs=pl.BlockSpec((t_len, hd), lambda b, h, t: (b * nt + t, h)),
        out_shape=jax.ShapeDtypeStruct((m, n_heads * hd), BF16),
        scratch_shapes=[pltpu.VMEM((3, t_len, hd), F32),
                        pltpu.VMEM((t_len, hd), BF16),
                        pltpu.VMEM((kv_rows, hd), BF16),
                        pltpu.VMEM((kv_rows, hd), BF16),
                        pltpu.VMEM((3, t_len, hd), F32),
                        pltpu.VMEM((3, t_len, hd), F32),
                        pltpu.VMEM((3, BAND_BLOCK, 2 * BAND_BLOCK), F32),
                        pltpu.VMEM((ATT_UNROLL * BAND_BLOCK, 2 * BAND_BLOCK), F32),
                        pltpu.VMEM((ATT_UNROLL * BAND_BLOCK, 2 * BAND_BLOCK), BF16)],
        compiler_params=_params(("parallel", "parallel", "arbitrary")),
        name="dilated_attention",
    )(proj, proj, proj, slopes)


def _conv_kernel(ts, ch, a_ref, g_ref, ha_ref, hg_ref, w_ref, cb_ref, lg_ref, lb_ref, o_ref, h_ref, c_ref):
    t = pl.program_id(1)
    hist = ha_ref[...] * jax.nn.sigmoid(hg_ref[...])
    h_ref[0:CONV_HALO, :] = jnp.where(t > 0, hist, 0.0)
    h_ref[CONV_HALO:, :] = a_ref[...] * jax.nn.sigmoid(g_ref[...])
    rows, lanes = 32, 512
    first = CONV_HALO - (CONV_WIDTH - 1)
    for r0 in range(0, ts, rows):
        for c0 in range(0, ch, lanes):
            acc = jnp.broadcast_to(cb_ref[:, c0:c0 + lanes], (rows, lanes))
            for j in range(CONV_WIDTH):
                acc = acc + w_ref[j:j + 1, c0:c0 + lanes] * h_ref[r0 + first + j:r0 + first + j + rows, c0:c0 + lanes]
            c_ref[r0:r0 + rows, c0:c0 + lanes] = acc
    y = _layer_norm(c_ref[...], lg_ref[...], lb_ref[...])
    o_ref[...] = (y * jax.nn.sigmoid(y)).astype(o_ref.dtype)


def _conformer_conv(proj, col0, bsz, seq, conv_w, conv_b, ln_g, ln_b, ts=256):
    m = proj.shape[0]
    ch = conv_w.shape[1]
    ca, cg = col0 // ch, col0 // ch + 1
    w_pad = jnp.zeros((CONV_HALO, ch), F32).at[:CONV_WIDTH].set(conv_w)
    nt = seq // ts
    per = ts // CONV_HALO

    def halo(col):
        return pl.BlockSpec((CONV_HALO, ch), lambda b, t: (jnp.maximum((b * nt + t) * per - 1, 0), col))

    return pl.pallas_call(
        functools.partial(_conv_kernel, ts, ch),
        grid=(bsz, nt),
        in_specs=[pl.BlockSpec((ts, ch), lambda b, t: (b * nt + t, ca)),
                  pl.BlockSpec((ts, ch), lambda b, t: (b * nt + t, cg)),
                  halo(ca), halo(cg),
                  pl.BlockSpec((CONV_HALO, ch), lambda b, t: (0, 0)),
                  pl.BlockSpec((1, ch), lambda b, t: (0, 0)),
                  pl.BlockSpec((1, ch), lambda b, t: (0, 0)),
                  pl.BlockSpec((1, ch), lambda b, t: (0, 0))],
        out_specs=pl.BlockSpec((ts, ch), lambda b, t: (b * nt + t, 0)),
        out_shape=jax.ShapeDtypeStruct((m, ch), BF16),
        scratch_shapes=[pltpu.VMEM((ts + CONV_HALO, ch), F32), pltpu.VMEM((ts, ch), F32)],
        compiler_params=_params(("parallel", "arbitrary")),
        name="conformer_conv",
    )(proj, proj, proj, proj, w_pad, conv_b.reshape(1, ch), ln_g.reshape(1, ch), ln_b.reshape(1, ch))


def _gates_kernel(x_ref, w_ref, bias_ref, gc_ref, gr_ref):
    length = x_ref.shape[0]
    pre = jnp.dot(x_ref[...], w_ref[...], preferred_element_type=F32,
                  precision=lax.Precision.HIGHEST) + bias_ref[...]
    log_f = jnp.minimum(pre, 0.0) - jnp.log1p(jnp.exp(-jnp.abs(pre)))
    ti = lax.broadcasted_iota(jnp.int32, (length, length), 0)
    si = lax.broadcasted_iota(jnp.int32, (length, length), 1)
    tril = (ti >= si).astype(F32)
    cum_f = jnp.dot(tril, log_f, preferred_element_type=F32, precision=lax.Precision.HIGHEST)
    lane = lax.broadcasted_iota(jnp.int32, pre.shape, 1)
    gates = jnp.where(lane < C_HEADS, pre, cum_f)
    gc_ref[...] = gates
    gr_ref[...] = gates.T


def _mlstm_gates(x, w_gate, bias):
    m, d = x.shape
    length = MLSTM_CHUNK
    return pl.pallas_call(
        _gates_kernel,
        grid=(m // length,),
        in_specs=[pl.BlockSpec((length, d), lambda i: (i, 0)),
                  pl.BlockSpec((d, GATE_LANES), lambda i: (0, 0)),
                  pl.BlockSpec((1, GATE_LANES), lambda i: (0, 0))],
        out_specs=[pl.BlockSpec((length, GATE_LANES), lambda i: (i, 0)),
                   pl.BlockSpec((GATE_LANES, length), lambda i: (0, i))],
        out_shape=[jax.ShapeDtypeStruct((m, GATE_LANES), F32),
                   jax.ShapeDtypeStruct((GATE_LANES, m), F32)],
        compiler_params=_params(("parallel",)),
        name="mlstm_gates",
    )(x, w_gate, bias)


def _mlstm_kernel(dk, dv, q_ref, k_ref, v_ref, o_ref, gc_ref, gr_ref, ng_ref, y_ref, c_ref, n_ref, m_ref):
    length = q_ref.shape[0]

    @pl.when(pl.program_id(1) == 0)
    def _():
        c_ref[...] = jnp.zeros_like(c_ref)
        n_ref[...] = jnp.zeros_like(n_ref)
        m_ref[...] = jnp.zeros_like(m_ref)

    ti = lax.broadcasted_iota(jnp.int32, (length, length), 0)
    si = lax.broadcasted_iota(jnp.int32, (length, length), 1)
    causal = ti >= si
    for h in range(C_HEADS):
        q = q_ref[:, h * dk:(h + 1) * dk]
        k = k_ref[:, h * dk:(h + 1) * dk]
        v = v_ref[:, h * dv:(h + 1) * dv]
        i_row = gr_ref[h:h + 1, :]
        b_row = gr_ref[C_HEADS + h:C_HEADS + h + 1, :]
        i_col = gc_ref[:, h:h + 1]
        b_col = gc_ref[:, C_HEADS + h:C_HEADS + h + 1]
        m_prev = m_ref[h:h + 1, 0:1]
        c_prev = c_ref[h]
        n_prev = n_ref[h:h + 1, :]

        dmat = jnp.where(causal, b_col - b_row + i_row, -jnp.inf)
        inter = b_col + m_prev
        m_t = jnp.maximum(inter, jnp.max(dmat, axis=-1, keepdims=True))
        w = jnp.exp(dmat - m_t)
        a = jnp.exp(inter - m_t)
        qk = lax.dot_general(q, k, (((1,), (1,)), ((), ())), preferred_element_type=F32)
        sqk = qk * w
        num = (a * jnp.dot(q, c_prev.astype(BF16), preferred_element_type=F32)
               + jnp.dot(sqk.astype(BF16), v, preferred_element_type=F32))
        den = (a * jnp.sum(q.astype(F32) * n_prev, axis=-1, keepdims=True)
               + jnp.sum(sqk, axis=-1, keepdims=True))
        hh = num / jnp.maximum(jnp.abs(den), jnp.exp(-m_t))

        b_last = b_col[length - 1:length, :]
        g = b_last - b_col + i_col
        m_new = jnp.maximum(b_last + m_prev, jnp.max(g, axis=0, keepdims=True))
        decay = jnp.exp(b_last + m_prev - m_new)
        kw = k.astype(F32) * jnp.exp(g - m_new)
        c_ref[h] = decay * c_prev + lax.dot_general(kw.astype(BF16), v, (((0,), (0,)), ((), ())),
                                                    preferred_element_type=F32)
        n_ref[h:h + 1, :] = decay * n_prev + jnp.sum(kw, axis=0, keepdims=True)
        m_ref[h:h + 1, :] = jnp.broadcast_to(m_new, (1, m_ref.shape[1]))

        mu = jnp.mean(hh, axis=-1, keepdims=True)
        hc = hh - mu
        var = jnp.mean(hc * hc, axis=-1, keepdims=True)
        hn = hc * lax.rsqrt(var + LN_EPS) * ng_ref[:, h * dv:(h + 1) * dv]
        y_ref[:, h * dv:(h + 1) * dv] = (jax.nn.sigmoid(o_ref[:, h * dv:(h + 1) * dv]) * hn).astype(y_ref.dtype)


def _mlstm(qkv, o, gates_col, gates_row, norm_g, bsz, seq, dk, dv):
    length = MLSTM_CHUNK
    nc = seq // length
    qk_w = C_HEADS * dk
    v_w = C_HEADS * dv
    v_blk = (2 * qk_w) // v_w
    return pl.pallas_call(
        functools.partial(_mlstm_kernel, dk, dv),
        grid=(bsz, nc),
        in_specs=[pl.BlockSpec((length, qk_w), lambda b, c: (b * nc + c, 0)),
                  pl.BlockSpec((length, qk_w), lambda b, c: (b * nc + c, 1)),
                  pl.BlockSpec((length, v_w), lambda b, c: (b * nc + c, v_blk)),
                  pl.BlockSpec((length, v_w), lambda b, c: (b * nc + c, 0)),
                  pl.BlockSpec((length, GATE_LANES), lambda b, c: (b * nc + c, 0)),
                  pl.BlockSpec((GATE_LANES, length), lambda b, c: (0, b * nc + c)),
                  pl.BlockSpec((1, v_w), lambda b, c: (0, 0))],
        out_specs=pl.BlockSpec((length, v_w), lambda b, c: (b * nc + c, 0)),
        out_shape=jax.ShapeDtypeStruct((bsz * seq, v_w), BF16),
        scratch_shapes=[pltpu.VMEM((C_HEADS, dk, dv), F32),
                        pltpu.VMEM((8, dk), F32),
                        pltpu.VMEM((8, GATE_LANES), F32)],
        compiler_params=_params(("parallel", "arbitrary")),
        name="mlstm_chunk",
    )(qkv, qkv, qkv, o, gates_col, gates_row, norm_g)


def _even_mixer(xb, x, w_in, conv_w, conv_b, conv_ln_g, conv_ln_b, w_out, ln_g, ln_b, alpha, bsz, seq):
    d = x.shape[1]
    a_width = d // 2
    n_heads = a_width // A_HEAD_DIM
    col_scale = jnp.ones((1, w_in.shape[1]), F32).at[:, :a_width].set(A_HEAD_DIM ** -0.5)
    proj = _matmul(xb, w_in.astype(BF16), col_scale, F32)
    att = _dilated_mixture_attention(proj, bsz, seq, n_heads)
    conv = _conformer_conv(proj, 3 * a_width, bsz, seq, conv_w, conv_b, conv_ln_g, conv_ln_b)
    w_out = w_out.astype(BF16)
    return _out_proj_ln([att, conv], [w_out[:a_width], w_out[a_width:]], x,
                        ln_g.reshape(1, d), ln_b.reshape(1, d), alpha)


def _odd_mixer(xb, x, w_in, igate_b, fgate_b, norm_g, w_out, ln_g, ln_b, alpha, bsz, seq):
    d = x.shape[1]
    v_w = d
    dv = v_w // C_HEADS
    dk = dv // 2
    qk_w = C_HEADS * dk
    main = 2 * qk_w + 2 * v_w
    w_main = w_in[:, :main].astype(BF16)
    qkv_scale = jnp.ones((1, 2 * qk_w + v_w), F32).at[:, :qk_w].set(dk ** -0.5)
    qkv = _matmul(xb, w_main[:, :2 * qk_w + v_w], qkv_scale, BF16)
    o = _matmul(xb, w_main[:, 2 * qk_w + v_w:], jnp.ones((1, v_w), F32), F32)
    w_gate = jnp.zeros((d, GATE_LANES), F32).at[:, :2 * C_HEADS].set(w_in[:, main:])
    bias = jnp.zeros((1, GATE_LANES), F32).at[0, :C_HEADS].set(igate_b).at[0, C_HEADS:2 * C_HEADS].set(fgate_b)
    gates_col, gates_row = _mlstm_gates(x, w_gate, bias)
    y = _mlstm(qkv, o, gates_col, gates_row, norm_g.reshape(1, v_w), bsz, seq, dk, dv)
    return _out_proj_ln([y], [w_out.astype(BF16)], x, ln_g.reshape(1, d), ln_b.reshape(1, d), alpha)


def kernel(x, even_w_in, even_conv_w, even_conv_b, even_conv_ln_g, even_conv_ln_b, even_w_out,
           odd_w_in, odd_igate_b, odd_fgate_b, odd_norm_g, odd_w_out, mix_ln_g, mix_ln_b,
           ffn_w1, ffn_w2, ffn_ln_g, ffn_ln_b):
    bsz, seq, d = x.shape
    depth = mix_ln_g.shape[0]
    alpha = (2 * depth) ** 0.25
    xf = x.reshape(bsz * seq, d)
    xb = xf.astype(BF16)
    for layer in range(depth):
        j = layer // 2
        if layer % 2 == 0:
            xf, xb = _even_mixer(xb, xf, even_w_in[j], even_conv_w[j], even_conv_b[j], even_conv_ln_g[j],
                                 even_conv_ln_b[j], even_w_out[j], mix_ln_g[layer], mix_ln_b[layer],
                                 alpha, bsz, seq)
        else:
            xf, xb = _odd_mixer(xb, xf, odd_w_in[j], odd_igate_b[j], odd_fgate_b[j], odd_norm_g[j],
                                odd_w_out[j], mix_ln_g[layer], mix_ln_b[layer], alpha, bsz, seq)
        xf, xb = _ffn_ln(xb, xf, ffn_w1[layer].astype(BF16), ffn_w2[layer].astype(BF16),
                         ffn_ln_g[layer].reshape(1, d), ffn_ln_b[layer].reshape(1, d), alpha)
    return xf.reshape(bsz, seq, d)
```

```python
import functools

import jax
import jax.numpy as jnp
from jax import lax
from jax.experimental import pallas as pl
from jax.experimental.pallas import tpu as pltpu

LN_EPS = 1e-5
DILATED_BRANCHES = ((128, 1), (512, 4), (2048, 16))
BAND_BLOCK = 128
A_HEAD_DIM = 128
ATT_TILE = 2048
ATT_UNROLL = 8
CONV_WIDTH = 31
CONV_HALO = 32
C_HEADS = 4
MLSTM_CHUNK = 256
GATE_LANES = 128
V7X_VMEM_LIMIT = 56 * 1024 * 1024

BF16 = jnp.bfloat16
F32 = jnp.float32


def _params(semantics):
    return pltpu.CompilerParams(dimension_semantics=semantics, vmem_limit_bytes=V7X_VMEM_LIMIT)


def _layer_norm(z, g, b):
    mu = jnp.mean(z, axis=-1, keepdims=True)
    zc = z - mu
    var = jnp.mean(zc * zc, axis=-1, keepdims=True)
    return zc * lax.rsqrt(var + LN_EPS) * g + b


def _matmul_kernel(x_ref, w_ref, s_ref, o_ref, wb_ref):
    @pl.when(pl.program_id(1) == 0)
    def _():
        wb_ref[...] = w_ref[...].astype(BF16)

    acc = jnp.dot(x_ref[...], wb_ref[...], preferred_element_type=F32)
    o_ref[...] = (acc * s_ref[...]).astype(o_ref.dtype)


def _matmul(x, w_stack, layer, col0, n_cols, col_scale, out_dtype, tm=1024, tn=1024):
    m, k = x.shape
    cb0 = col0 // tn
    return pl.pallas_call(
        _matmul_kernel,
        grid=(n_cols // tn, m // tm),
        in_specs=[pl.BlockSpec((tm, k), lambda j, i: (i, 0)),
                  pl.BlockSpec((None, k, tn), lambda j, i: (layer, 0, cb0 + j)),
                  pl.BlockSpec((1, tn), lambda j, i: (0, j))],
        out_specs=pl.BlockSpec((tm, tn), lambda j, i: (i, j)),
        out_shape=jax.ShapeDtypeStruct((m, n_cols), out_dtype),
        scratch_shapes=[pltpu.VMEM((k, tn), BF16)],
        compiler_params=_params(("parallel", "arbitrary")),
        name="proj_matmul",
    )(x, w_stack, col_scale)


def _out_proj_ln_kernel(alpha, n_in, *refs):
    ys = refs[:n_in]
    w_ref, x_ref, g_ref, b_ref, of_ref, ob_ref, wb_ref = refs[n_in:]

    @pl.when(pl.program_id(0) == 0)
    def _():
        wb_ref[...] = w_ref[...].astype(BF16)

    acc = alpha * x_ref[...]
    k0 = 0
    for y_ref in ys:
        kw = y_ref.shape[1]
        acc = acc + jnp.dot(y_ref[...], wb_ref[k0:k0 + kw, :], preferred_element_type=F32)
        k0 += kw
    out = _layer_norm(acc, g_ref[...], b_ref[...])
    of_ref[...] = out
    ob_ref[...] = out.astype(BF16)


def _out_proj_ln(ys, w_stack, layer, x, g, b, alpha, tm=512):
    m, d = x.shape
    n_in = len(ys)
    k = w_stack.shape[1]
    in_specs = ([pl.BlockSpec((tm, y.shape[1]), lambda i: (i, 0)) for y in ys]
                + [pl.BlockSpec((None, k, d), lambda i: (layer, 0, 0), pipeline_mode=pl.Buffered(1)),
                   pl.BlockSpec((tm, d), lambda i: (i, 0)),
                   pl.BlockSpec((1, d), lambda i: (0, 0)),
                   pl.BlockSpec((1, d), lambda i: (0, 0))])
    return pl.pallas_call(
        functools.partial(_out_proj_ln_kernel, alpha, n_in),
        grid=(m // tm,),
        in_specs=in_specs,
        out_specs=[pl.BlockSpec((tm, d), lambda i: (i, 0)),
                   pl.BlockSpec((tm, d), lambda i: (i, 0))],
        out_shape=[jax.ShapeDtypeStruct((m, d), F32), jax.ShapeDtypeStruct((m, d), BF16)],
        scratch_shapes=[pltpu.VMEM((k, d), BF16)],
        compiler_params=_params(("arbitrary",)),
        name="out_proj_ln",
    )(*ys, w_stack, x, g, b)


def _ffn_kernel(alpha, xb_ref, w1_ref, w2_ref, x_ref, g_ref, b_ref, of_ref, ob_ref, acc_ref):
    j = pl.program_id(1)

    @pl.when(j == 0)
    def _():
        acc_ref[...] = alpha * x_ref[...]

    h = jnp.dot(xb_ref[...], w1_ref[...], preferred_element_type=F32)
    h = jnp.maximum(h, 0.0)
    h = (h * h).astype(BF16)
    acc_ref[...] += jnp.dot(h, w2_ref[...], preferred_element_type=F32)

    @pl.when(j == pl.num_programs(1) - 1)
    def _():
        out = _layer_norm(acc_ref[...], g_ref[...], b_ref[...])
        of_ref[...] = out
        ob_ref[...] = out.astype(BF16)


def _ffn_ln(xb, x, w1, w2, layer, g, b, alpha, tm=512, th=1024):
    m, d = x.shape
    f = w1.shape[2]
    return pl.pallas_call(
        functools.partial(_ffn_kernel, alpha),
        grid=(m // tm, f // th),
        in_specs=[pl.BlockSpec((tm, d), lambda i, j: (i, 0)),
                  pl.BlockSpec((None, d, th), lambda i, j: (layer, 0, j)),
                  pl.BlockSpec((None, th, d), lambda i, j: (layer, j, 0)),
                  pl.BlockSpec((tm, d), lambda i, j: (i, 0)),
                  pl.BlockSpec((1, d), lambda i, j: (0, 0)),
                  pl.BlockSpec((1, d), lambda i, j: (0, 0))],
        out_specs=[pl.BlockSpec((tm, d), lambda i, j: (i, 0)),
                   pl.BlockSpec((tm, d), lambda i, j: (i, 0))],
        out_shape=[jax.ShapeDtypeStruct((m, d), F32), jax.ShapeDtypeStruct((m, d), BF16)],
        scratch_shapes=[pltpu.VMEM((tm, d), F32)],
        compiler_params=_params(("parallel", "arbitrary")),
        name="ffn_ln",
    )(xb, w1, w2, x, g, b)


def _kv_rows(dil):
    return dil * BAND_BLOCK + ATT_TILE


def _attn_kernel(q_ref, k_ref, v_ref, slope_ref, out_ref,
                 g4_ref, qs_ref, ks_ref, vs_ref, o_ref, l_ref, bias_ref, s_ref, e_ref):
    tile = pl.program_id(2)
    p = BAND_BLOCK
    t_len = ATT_TILE
    qi = lax.broadcasted_iota(jnp.int32, (p, 2 * p), 0)
    kj = lax.broadcasted_iota(jnp.int32, (p, 2 * p), 1)
    dist = p + qi - kj
    valid = jnp.logical_and(dist >= 0, dist <= p)
    in_prev_block = kj < p
    slope = slope_ref[0]
    for bi, (_, dil) in enumerate(DILATED_BRANCHES):
        bias_ref[bi] = jnp.where(valid, -(slope * (dist * dil).astype(F32)), -jnp.inf)

    def rows(start, size, dil):
        return pl.ds(start, size, stride=dil) if dil > 1 else pl.ds(start, size)

    kv_base = [sum(_kv_rows(d) for _, d in DILATED_BRANCHES[:bi]) for bi in range(len(DILATED_BRANCHES))]
    srcs = (q_ref, k_ref, v_ref)
    dil4 = DILATED_BRANCHES[1][1]
    w4 = t_len // dil4
    for x, src in enumerate(srcs):
        for r in range(dil4):
            g4_ref[x, r * w4:(r + 1) * w4, :] = src[rows(r, w4, dil4), :]

    def residue_rows(x, r, dil):
        if dil == 1:
            return srcs[x][...]
        if dil == dil4:
            return g4_ref[x, r * w4:(r + 1) * w4, :]
        return g4_ref[x, rows((r % dil4) * w4 + r // dil4, t_len // dil, dil // dil4), :]

    nt = (((1,), (1,)), ((), ()))
    for bi, (_, dil) in enumerate(DILATED_BRANCHES):
        wq = t_len // dil
        nq = wq // p
        wk = p + wq
        base = kv_base[bi]

        @pl.when(tile == 0)
        def _(dil=dil, wk=wk, base=base):
            for r in range(dil):
                for dst_ref in (ks_ref, vs_ref):
                    dst_ref[base + r * wk:base + r * wk + p, :] = jnp.zeros((p, A_HEAD_DIM), BF16)

        @pl.when(tile > 0)
        def _(dil=dil, wk=wk, wq=wq, base=base):
            for r in range(dil):
                for dst_ref in (ks_ref, vs_ref):
                    dst_ref[base + r * wk:base + r * wk + p, :] = dst_ref[base + r * wk + wq:base + (r + 1) * wk, :]

        for r in range(dil):
            qs_ref[r * wq:(r + 1) * wq, :] = residue_rows(0, r, dil).astype(BF16)
            ks_ref[base + r * wk + p:base + (r + 1) * wk, :] = residue_rows(1, r, dil).astype(BF16)
            vs_ref[base + r * wk + p:base + (r + 1) * wk, :] = residue_rows(2, r, dil).astype(BF16)

        def group(g, carry, dil=dil, nq=nq, wk=wk, bi=bi, base=base):
            starts = []
            for u in range(ATT_UNROLL):
                idx = g * ATT_UNROLL + u
                r = idx // nq
                n = idx - r * nq
                q0 = pl.multiple_of(idx * p, p)
                k0 = pl.multiple_of(base + r * wk + n * p, p)
                s = lax.dot_general(qs_ref[pl.ds(q0, p), :], ks_ref[pl.ds(k0, 2 * p), :], nt,
                                    preferred_element_type=F32) + bias_ref[bi]
                no_history = jnp.logical_and(tile == 0, n == 0)
                s_ref[u * p:(u + 1) * p, :] = jnp.where(jnp.logical_and(no_history, in_prev_block), -jnp.inf, s)
                starts.append((k0, r + dil * p * n))
            s = s_ref[...]
            m = jnp.max(s, axis=-1, keepdims=True)
            e = jnp.exp(s - m)
            l = jnp.sum(e, axis=-1, keepdims=True)
            e_ref[...] = e.astype(BF16)
            inv_l = 1.0 / l
            lse = m + jnp.log(l)
            for u, (k0, t0) in enumerate(starts):
                o = jnp.dot(e_ref[u * p:(u + 1) * p, :], vs_ref[pl.ds(k0, 2 * p), :], preferred_element_type=F32)
                o_ref[bi, rows(t0, p, dil), :] = o * inv_l[u * p:(u + 1) * p]
                l_ref[bi, rows(t0, p, dil), :] = jnp.broadcast_to(lse[u * p:(u + 1) * p], (p, A_HEAD_DIM))
            return carry

        lax.fori_loop(0, (dil * nq) // ATT_UNROLL, group, 0)

    chunk = 256
    for c0 in range(0, t_len, chunk):
        sl = slice(c0, c0 + chunk)
        l1, l2, l3 = l_ref[0, sl, :], l_ref[1, sl, :], l_ref[2, sl, :]
        mx = jnp.maximum(jnp.maximum(l1, l2), l3)
        e1, e2, e3 = jnp.exp(l1 - mx), jnp.exp(l2 - mx), jnp.exp(l3 - mx)
        mix = (e1 * o_ref[0, sl, :] + e2 * o_ref[1, sl, :] + e3 * o_ref[2, sl, :]) / (e1 + e2 + e3)
        out_ref[sl, :] = mix.astype(out_ref.dtype)


def _dilated_mixture_attention(proj, bsz, seq, n_heads):
    m = proj.shape[0]
    hd = A_HEAD_DIM
    t_len = ATT_TILE
    nt = seq // t_len
    slopes = 2.0 ** (-8.0 * jnp.arange(1, n_heads + 1, dtype=F32) / n_heads)
    slopes = jnp.broadcast_to(slopes[:, None, None], (n_heads, 1, 2 * BAND_BLOCK))

    kv_rows = sum(_kv_rows(dil) for _, dil in DILATED_BRANCHES)

    def head_block(col):
        return pl.BlockSpec((t_len, hd), lambda b, h, t: (b * nt + t, col * n_heads + h))

    return pl.pallas_call(
        _attn_kernel,
        grid=(bsz, n_heads, nt),
        in_specs=[head_block(0), head_block(1), head_block(2),
                  pl.BlockSpec((1, 1, 2 * BAND_BLOCK), lambda b, h, t: (h, 0, 0))],
        out_specs=pl.BlockSpec((t_len, hd), lambda b, h, t: (b * nt + t, h)),
        out_shape=jax.ShapeDtypeStruct((m, n_heads * hd), BF16),
        scratch_shapes=[pltpu.VMEM((3, t_len, hd), F32),
                        pltpu.VMEM((t_len, hd), BF16),
                        pltpu.VMEM((kv_rows, hd), BF16),
                        pltpu.VMEM((kv_rows, hd), BF16),
                        pltpu.VMEM((3, t_len, hd), F32),
                        pltpu.VMEM((3, t_len, hd), F32),
                        pltpu.VMEM((3, BAND_BLOCK, 2 * BAND_BLOCK), F32),
                        pltpu.VMEM((ATT_UNROLL * BAND_BLOCK, 2 * BAND_BLOCK), F32),
                        pltpu.VMEM((ATT_UNROLL * BAND_BLOCK, 2 * BAND_BLOCK), BF16)],
        compiler_params=_params(("parallel", "parallel", "arbitrary")),
        name="dilated_attention",
    )(proj, proj, proj, slopes)


def _conv_kernel(ts, ch, a_ref, g_ref, ha_ref, hg_ref, w_ref, cb_ref, lg_ref, lb_ref, o_ref, h_ref, c_ref):
    t = pl.program_id(1)
    hist = ha_ref[...] * jax.nn.sigmoid(hg_ref[...])
    h_ref[0:CONV_HALO, :] = jnp.where(t > 0, hist, 0.0)
    h_ref[CONV_HALO:, :] = a_ref[...] * jax.nn.sigmoid(g_ref[...])
    rows, lanes = 32, 512
    first = CONV_HALO - (CONV_WIDTH - 1)
    for r0 in range(0, ts, rows):
        for c0 in range(0, ch, lanes):
            acc = jnp.broadcast_to(cb_ref[:, c0:c0 + lanes], (rows, lanes))
            for j in range(CONV_WIDTH):
                acc = acc + w_ref[j:j + 1, c0:c0 + lanes] * h_ref[r0 + first + j:r0 + first + j + rows, c0:c0 + lanes]
            c_ref[r0:r0 + rows, c0:c0 + lanes] = acc
    y = _layer_norm(c_ref[...], lg_ref[...], lb_ref[...])
    o_ref[...] = (y * jax.nn.sigmoid(y)).astype(o_ref.dtype)


def _conformer_conv(proj, col0, bsz, seq, conv_w, conv_b, ln_g, ln_b, ts=256):
    m = proj.shape[0]
    ch = conv_w.shape[1]
    ca, cg = col0 // ch, col0 // ch + 1
    w_pad = jnp.zeros((CONV_HALO, ch), F32).at[:CONV_WIDTH].set(conv_w)
    nt = seq // ts
    per = ts // CONV_HALO

    def halo(col):
        return pl.BlockSpec((CONV_HALO, ch), lambda b, t: (jnp.maximum((b * nt + t) * per - 1, 0), col))

    return pl.pallas_call(
        functools.partial(_conv_kernel, ts, ch),
        grid=(bsz, nt),
        in_specs=[pl.BlockSpec((ts, ch), lambda b, t: (b * nt + t, ca)),
                  pl.BlockSpec((ts, ch), lambda b, t: (b * nt + t, cg)),
                  halo(ca), halo(cg),
                  pl.BlockSpec((CONV_HALO, ch), lambda b, t: (0, 0)),
                  pl.BlockSpec((1, ch), lambda b, t: (0, 0)),
                  pl.BlockSpec((1, ch), lambda b, t: (0, 0)),
                  pl.BlockSpec((1, ch), lambda b, t: (0, 0))],
        out_specs=pl.BlockSpec((ts, ch), lambda b, t: (b * nt + t, 0)),
        out_shape=jax.ShapeDtypeStruct((m, ch), BF16),
        scratch_shapes=[pltpu.VMEM((ts + CONV_HALO, ch), F32), pltpu.VMEM((ts, ch), F32)],
        compiler_params=_params(("parallel", "arbitrary")),
        name="conformer_conv",
    )(proj, proj, proj, proj, w_pad, conv_b.reshape(1, ch), ln_g.reshape(1, ch), ln_b.reshape(1, ch))


def _gates_kernel(x_ref, w_ref, bias_ref, gc_ref, gr_ref):
    length = x_ref.shape[0]
    pre = jnp.dot(x_ref[...], w_ref[...], preferred_element_type=F32,
                  precision=lax.Precision.HIGHEST) + bias_ref[...]
    log_f = jnp.minimum(pre, 0.0) - jnp.log1p(jnp.exp(-jnp.abs(pre)))
    ti = lax.broadcasted_iota(jnp.int32, (length, length), 0)
    si = lax.broadcasted_iota(jnp.int32, (length, length), 1)
    tril = (ti >= si).astype(F32)
    cum_f = jnp.dot(tril, log_f, preferred_element_type=F32, precision=lax.Precision.HIGHEST)
    lane = lax.broadcasted_iota(jnp.int32, pre.shape, 1)
    gates = jnp.where(lane < C_HEADS, pre, cum_f)
    gc_ref[...] = gates
    gr_ref[...] = gates.T


def _mlstm_gates(x, w_gate, bias):
    m, d = x.shape
    length = MLSTM_CHUNK
    return pl.pallas_call(
        _gates_kernel,
        grid=(m // length,),
        in_specs=[pl.BlockSpec((length, d), lambda i: (i, 0)),
                  pl.BlockSpec((d, GATE_LANES), lambda i: (0, 0)),
                  pl.BlockSpec((1, GATE_LANES), lambda i: (0, 0))],
        out_specs=[pl.BlockSpec((length, GATE_LANES), lambda i: (i, 0)),
                   pl.BlockSpec((GATE_LANES, length), lambda i: (0, i))],
        out_shape=[jax.ShapeDtypeStruct((m, GATE_LANES), F32),
                   jax.ShapeDtypeStruct((GATE_LANES, m), F32)],
        compiler_params=_params(("parallel",)),
        name="mlstm_gates",
    )(x, w_gate, bias)


def _mlstm_kernel(dk, dv, q_ref, k_ref, v_ref, o_ref, gc_ref, gr_ref, ng_ref, y_ref, c_ref, n_ref, m_ref):
    length = q_ref.shape[0]

    @pl.when(pl.program_id(1) == 0)
    def _():
        c_ref[...] = jnp.zeros_like(c_ref)
        n_ref[...] = jnp.zeros_like(n_ref)
        m_ref[...] = jnp.zeros_like(m_ref)

    ti = lax.broadcasted_iota(jnp.int32, (length, length), 0)
    si = lax.broadcasted_iota(jnp.int32, (length, length), 1)
    causal = ti >= si
    for h in range(C_HEADS):
        q = q_ref[:, h * dk:(h + 1) * dk]
        k = k_ref[:, h * dk:(h + 1) * dk]
        v = v_ref[:, h * dv:(h + 1) * dv]
        i_row = gr_ref[h:h + 1, :]
        b_row = gr_ref[C_HEADS + h:C_HEADS + h + 1, :]
        i_col = gc_ref[:, h:h + 1]
        b_col = gc_ref[:, C_HEADS + h:C_HEADS + h + 1]
        m_prev = m_ref[h:h + 1, 0:1]
        c_prev = c_ref[h]
        n_prev = n_ref[h:h + 1, :]

        dmat = jnp.where(causal, b_col - b_row + i_row, -jnp.inf)
        inter = b_col + m_prev
        m_t = jnp.maximum(inter, jnp.max(dmat, axis=-1, keepdims=True))
        w = jnp.exp(dmat - m_t)
        a = jnp.exp(inter - m_t)
        qk = lax.dot_general(q, k, (((1,), (1,)), ((), ())), preferred_element_type=F32)
        sqk = qk * w
        num = (a * jnp.dot(q, c_prev.astype(BF16), preferred_element_type=F32)
               + jnp.dot(sqk.astype(BF16), v, preferred_element_type=F32))
        den = (a * jnp.sum(q.astype(F32) * n_prev, axis=-1, keepdims=True)
               + jnp.sum(sqk, axis=-1, keepdims=True))
        hh = num / jnp.maximum(jnp.abs(den), jnp.exp(-m_t))

        b_last = b_col[length - 1:length, :]
        g = b_last - b_col + i_col
        m_new = jnp.maximum(b_last + m_prev, jnp.max(g, axis=0, keepdims=True))
        decay = jnp.exp(b_last + m_prev - m_new)
        kw = k.astype(F32) * jnp.exp(g - m_new)
        c_ref[h] = decay * c_prev + lax.dot_general(kw.astype(BF16), v, (((0,), (0,)), ((), ())),
                                                    preferred_element_type=F32)
        n_ref[h:h + 1, :] = decay * n_prev + jnp.sum(kw, axis=0, keepdims=True)
        m_ref[h:h + 1, :] = jnp.broadcast_to(m_new, (1, m_ref.shape[1]))

        mu = jnp.mean(hh, axis=-1, keepdims=True)
        hc = hh - mu
        var = jnp.mean(hc * hc, axis=-1, keepdims=True)
        hn = hc * lax.rsqrt(var + LN_EPS) * ng_ref[:, h * dv:(h + 1) * dv]
        y_ref[:, h * dv:(h + 1) * dv] = (jax.nn.sigmoid(o_ref[:, h * dv:(h + 1) * dv]) * hn).astype(y_ref.dtype)


def _mlstm(qkv, o, gates_col, gates_row, norm_g, bsz, seq, dk, dv):
    length = MLSTM_CHUNK
    nc = seq // length
    qk_w = C_HEADS * dk
    v_w = C_HEADS * dv
    v_blk = (2 * qk_w) // v_w
    return pl.pallas_call(
        functools.partial(_mlstm_kernel, dk, dv),
        grid=(bsz, nc),
        in_specs=[pl.BlockSpec((length, qk_w), lambda b, c: (b * nc + c, 0)),
                  pl.BlockSpec((length, qk_w), lambda b, c: (b * nc + c, 1)),
                  pl.BlockSpec((length, v_w), lambda b, c: (b * nc + c, v_blk)),
                  pl.BlockSpec((length, v_w), lambda b, c: (b * nc + c, 0)),
                  pl.BlockSpec((length, GATE_LANES), lambda b, c: (b * nc + c, 0)),
                  pl.BlockSpec((GATE_LANES, length), lambda b, c: (0, b * nc + c)),
                  pl.BlockSpec((1, v_w), lambda b, c: (0, 0))],
        out_specs=pl.BlockSpec((length, v_w), lambda b, c: (b * nc + c, 0)),
        out_shape=jax.ShapeDtypeStruct((bsz * seq, v_w), BF16),
        scratch_shapes=[pltpu.VMEM((C_HEADS, dk, dv), F32),
                        pltpu.VMEM((8, dk), F32),
                        pltpu.VMEM((8, GATE_LANES), F32)],
        compiler_params=_params(("parallel", "arbitrary")),
        name="mlstm_chunk",
    )(qkv, qkv, qkv, o, gates_col, gates_row, norm_g)


def _even_mixer(xb, x, w_in, j, conv_w, conv_b, conv_ln_g, conv_ln_b, w_out, ln_g, ln_b, alpha, bsz, seq):
    d = x.shape[1]
    a_width = d // 2
    n_heads = a_width // A_HEAD_DIM
    in_width = w_in.shape[2]
    col_scale = jnp.ones((1, in_width), F32).at[:, :a_width].set(A_HEAD_DIM ** -0.5)
    proj = _matmul(xb, w_in, j, 0, in_width, col_scale, F32)
    att = _dilated_mixture_attention(proj, bsz, seq, n_heads)
    conv = _conformer_conv(proj, 3 * a_width, bsz, seq, conv_w, conv_b, conv_ln_g, conv_ln_b)
    return _out_proj_ln([att, conv], w_out, j, x, ln_g.reshape(1, d), ln_b.reshape(1, d), alpha)


def _odd_mixer(xb, x, w_in, j, igate_b, fgate_b, norm_g, w_out, ln_g, ln_b, alpha, bsz, seq):
    d = x.shape[1]
    v_w = d
    dv = v_w // C_HEADS
    dk = dv // 2
    qk_w = C_HEADS * dk
    qkv_w = 2 * qk_w + v_w
    qkv_scale = jnp.ones((1, qkv_w), F32).at[:, :qk_w].set(dk ** -0.5)
    qkv = _matmul(xb, w_in, j, 0, qkv_w, qkv_scale, BF16)
    o = _matmul(xb, w_in, j, qkv_w, v_w, jnp.ones((1, v_w), F32), F32)
    w_gate = jnp.zeros((d, GATE_LANES), F32).at[:, :2 * C_HEADS].set(w_in[j, :, qkv_w + v_w:])
    bias = jnp.zeros((1, GATE_LANES), F32).at[0, :C_HEADS].set(igate_b).at[0, C_HEADS:2 * C_HEADS].set(fgate_b)
    gates_col, gates_row = _mlstm_gates(x, w_gate, bias)
    y = _mlstm(qkv, o, gates_col, gates_row, norm_g.reshape(1, v_w), bsz, seq, dk, dv)
    return _out_proj_ln([y], w_out, j, x, ln_g.reshape(1, d), ln_b.reshape(1, d), alpha)


def kernel(x, even_w_in, even_conv_w, even_conv_b, even_conv_ln_g, even_conv_ln_b, even_w_out,
           odd_w_in, odd_igate_b, odd_fgate_b, odd_norm_g, odd_w_out, mix_ln_g, mix_ln_b,
           ffn_w1, ffn_w2, ffn_ln_g, ffn_ln_b):
    bsz, seq, d = x.shape
    depth = mix_ln_g.shape[0]
    alpha = (2 * depth) ** 0.25
    xf = x.reshape(bsz * seq, d)
    xb = xf.astype(BF16)
    ffn_w1 = ffn_w1.astype(BF16)
    ffn_w2 = ffn_w2.astype(BF16)
    for layer in range(depth):
        j = layer // 2
        if layer % 2 == 0:
            xf, xb = _even_mixer(xb, xf, even_w_in, j, even_conv_w[j], even_conv_b[j], even_conv_ln_g[j],
                                 even_conv_ln_b[j], even_w_out, mix_ln_g[layer], mix_ln_b[layer],
                                 alpha, bsz, seq)
        else:
            xf, xb = _odd_mixer(xb, xf, odd_w_in, j, odd_igate_b[j], odd_fgate_b[j], odd_norm_g[j],
                                odd_w_out, mix_ln_g[layer], mix_ln_b[layer], alpha, bsz, seq)
        xf, xb = _ffn_ln(xb, xf, ffn_w1, ffn_w2, layer,
                         ffn_ln_g[layer].reshape(1, d), ffn_ln_b[layer].reshape(1, d), alpha)
    return xf.reshape(bsz, seq, d)
```

```python
import functools

import jax
import jax.numpy as jnp
from jax import lax
from jax.experimental import pallas as pl
from jax.experimental.pallas import tpu as pltpu

LN_EPS = 1e-5
DILATED_BRANCHES = ((128, 1), (512, 4), (2048, 16))
BAND_BLOCK = 128
A_HEAD_DIM = 128
ATT_TILE = 2048
ATT_UNROLL = 16
CONV_WIDTH = 31
CONV_HALO = 32
C_HEADS = 4
MLSTM_CHUNK = 256
GATE_LANES = 128
SUBLANES = 8
V7X_VMEM_LIMIT = 56 * 1024 * 1024

BF16 = jnp.bfloat16
F32 = jnp.float32


def _params(semantics):
    return pltpu.CompilerParams(dimension_semantics=semantics, vmem_limit_bytes=V7X_VMEM_LIMIT)


def _layer_norm(z, g, b):
    mu = jnp.mean(z, axis=-1, keepdims=True)
    zc = z - mu
    var = jnp.mean(zc * zc, axis=-1, keepdims=True)
    return zc * lax.rsqrt(var + LN_EPS) * g + b


def _matmul_kernel(x_ref, w_ref, s_ref, o_ref, wb_ref):
    @pl.when(pl.program_id(1) == 0)
    def _():
        wb_ref[...] = w_ref[...].astype(BF16)

    acc = jnp.dot(x_ref[...], wb_ref[...], preferred_element_type=F32)
    o_ref[...] = (acc * s_ref[...]).astype(o_ref.dtype)


def _matmul(x, w_stack, layer, col0, n_cols, col_scale, out_dtype, tm=1024, tn=1024):
    m, k = x.shape
    cb0 = col0 // tn
    return pl.pallas_call(
        _matmul_kernel,
        grid=(n_cols // tn, m // tm),
        in_specs=[pl.BlockSpec((tm, k), lambda j, i: (i, 0)),
                  pl.BlockSpec((None, k, tn), lambda j, i: (layer, 0, cb0 + j)),
                  pl.BlockSpec((1, tn), lambda j, i: (0, j))],
        out_specs=pl.BlockSpec((tm, tn), lambda j, i: (i, j)),
        out_shape=jax.ShapeDtypeStruct((m, n_cols), out_dtype),
        scratch_shapes=[pltpu.VMEM((k, tn), BF16)],
        compiler_params=_params(("parallel", "arbitrary")),
        name="proj_matmul",
    )(x, w_stack, col_scale)


def _out_proj_ln_kernel(alpha, n_in, *refs):
    ys = refs[:n_in]
    w_ref, x_ref, g_ref, b_ref, of_ref, ob_ref, wb_ref = refs[n_in:]

    @pl.when(pl.program_id(0) == 0)
    def _():
        wb_ref[...] = w_ref[...].astype(BF16)

    acc = alpha * x_ref[...]
    k0 = 0
    for y_ref in ys:
        kw = y_ref.shape[1]
        acc = acc + jnp.dot(y_ref[...], wb_ref[k0:k0 + kw, :], preferred_element_type=F32)
        k0 += kw
    out = _layer_norm(acc, g_ref[...], b_ref[...])
    of_ref[...] = out
    ob_ref[...] = out.astype(BF16)


def _out_proj_ln(ys, w_stack, layer, x, g, b, alpha, tm=512):
    m, d = x.shape
    n_in = len(ys)
    k = w_stack.shape[1]
    in_specs = ([pl.BlockSpec((tm, y.shape[1]), lambda i: (i, 0)) for y in ys]
                + [pl.BlockSpec((None, k, d), lambda i: (layer, 0, 0), pipeline_mode=pl.Buffered(1)),
                   pl.BlockSpec((tm, d), lambda i: (i, 0)),
                   pl.BlockSpec((1, d), lambda i: (0, 0)),
                   pl.BlockSpec((1, d), lambda i: (0, 0))])
    return pl.pallas_call(
        functools.partial(_out_proj_ln_kernel, alpha, n_in),
        grid=(m // tm,),
        in_specs=in_specs,
        out_specs=[pl.BlockSpec((tm, d), lambda i: (i, 0)),
                   pl.BlockSpec((tm, d), lambda i: (i, 0))],
        out_shape=[jax.ShapeDtypeStruct((m, d), F32), jax.ShapeDtypeStruct((m, d), BF16)],
        scratch_shapes=[pltpu.VMEM((k, d), BF16)],
        compiler_params=_params(("arbitrary",)),
        name="out_proj_ln",
    )(*ys, w_stack, x, g, b)


def _ffn_kernel(alpha, x_ref, w1_ref, w2_ref, g_ref, b_ref, of_ref, ob_ref, xb_ref):
    j = pl.program_id(1)

    @pl.when(j == 0)
    def _():
        x = x_ref[...]
        xb_ref[...] = x.astype(BF16)
        of_ref[...] = alpha * x

    h = jnp.dot(xb_ref[...], w1_ref[...].astype(BF16), preferred_element_type=F32)
    h = jnp.maximum(h, 0.0)
    h = (h * h).astype(BF16)
    of_ref[...] += jnp.dot(h, w2_ref[...].astype(BF16), preferred_element_type=F32)

    @pl.when(j == pl.num_programs(1) - 1)
    def _():
        out = _layer_norm(of_ref[...], g_ref[...], b_ref[...])
        of_ref[...] = out
        ob_ref[...] = out.astype(BF16)


def _ffn_ln(x, w1, w2, layer, g, b, alpha, tm=1024, th=512):
    m, d = x.shape
    f = w1.shape[2]
    once = pl.Buffered(1)
    return pl.pallas_call(
        functools.partial(_ffn_kernel, alpha),
        grid=(m // tm, f // th),
        in_specs=[pl.BlockSpec((tm, d), lambda i, j: (i, 0), pipeline_mode=once),
                  pl.BlockSpec((None, d, th), lambda i, j: (layer, 0, j)),
                  pl.BlockSpec((None, th, d), lambda i, j: (layer, j, 0)),
                  pl.BlockSpec((1, d), lambda i, j: (0, 0)),
                  pl.BlockSpec((1, d), lambda i, j: (0, 0))],
        out_specs=[pl.BlockSpec((tm, d), lambda i, j: (i, 0), pipeline_mode=once),
                   pl.BlockSpec((tm, d), lambda i, j: (i, 0), pipeline_mode=once)],
        out_shape=[jax.ShapeDtypeStruct((m, d), F32), jax.ShapeDtypeStruct((m, d), BF16)],
        scratch_shapes=[pltpu.VMEM((tm, d), BF16)],
        compiler_params=_params(("parallel", "arbitrary")),
        name="ffn_ln",
    )(x, w1, w2, g, b)


def _kv_rows(dil):
    return dil * BAND_BLOCK + ATT_TILE


def _attn_kernel(q_ref, k_ref, v_ref, slope_ref, out_ref,
                 g4_ref, qs_ref, ks_ref, vs_ref, o_ref, l_ref, bias_ref, s_ref, e_ref):
    tile = pl.program_id(2)
    p = BAND_BLOCK
    t_len = ATT_TILE
    qi = lax.broadcasted_iota(jnp.int32, (p, 2 * p), 0)
    kj = lax.broadcasted_iota(jnp.int32, (p, 2 * p), 1)
    dist = p + qi - kj
    valid = jnp.logical_and(dist >= 0, dist <= p)
    in_prev_block = kj < p
    slope = slope_ref[0]
    for bi, (_, dil) in enumerate(DILATED_BRANCHES):
        bias_ref[bi] = jnp.where(valid, -(slope * (dist * dil).astype(F32)), -jnp.inf)

    def rows(start, size, dil):
        return pl.ds(start, size, stride=dil) if dil > 1 else pl.ds(start, size)

    kv_base = [sum(_kv_rows(d) for _, d in DILATED_BRANCHES[:bi]) for bi in range(len(DILATED_BRANCHES))]
    srcs = (q_ref, k_ref, v_ref)
    dil4 = DILATED_BRANCHES[1][1]
    w4 = t_len // dil4
    for x, src in enumerate(srcs):
        for r in range(dil4):
            g4_ref[x, r * w4:(r + 1) * w4, :] = src[rows(r, w4, dil4), :]

    def residue_rows(x, r, dil):
        if dil == 1:
            return srcs[x][...]
        if dil == dil4:
            return g4_ref[x, r * w4:(r + 1) * w4, :]
        return g4_ref[x, rows((r % dil4) * w4 + r // dil4, t_len // dil, dil // dil4), :]

    nt = (((1,), (1,)), ((), ()))
    for bi, (_, dil) in enumerate(DILATED_BRANCHES):
        wq = t_len // dil
        nq = wq // p
        wk = p + wq
        base = kv_base[bi]

        @pl.when(tile == 0)
        def _(dil=dil, wk=wk, base=base):
            for r in range(dil):
                for dst_ref in (ks_ref, vs_ref):
                    dst_ref[base + r * wk:base + r * wk + p, :] = jnp.zeros((p, A_HEAD_DIM), BF16)

        @pl.when(tile > 0)
        def _(dil=dil, wk=wk, wq=wq, base=base):
            for r in range(dil):
                for dst_ref in (ks_ref, vs_ref):
                    dst_ref[base + r * wk:base + r * wk + p, :] = dst_ref[base + r * wk + wq:base + (r + 1) * wk, :]

        for r in range(dil):
            qs_ref[r * wq:(r + 1) * wq, :] = residue_rows(0, r, dil).astype(BF16)
            ks_ref[base + r * wk + p:base + (r + 1) * wk, :] = residue_rows(1, r, dil).astype(BF16)
            vs_ref[base + r * wk + p:base + (r + 1) * wk, :] = residue_rows(2, r, dil).astype(BF16)

        def group(g, carry, dil=dil, nq=nq, wk=wk, bi=bi, base=base):
            starts = []
            for u in range(ATT_UNROLL):
                idx = g * ATT_UNROLL + u
                r = idx // nq
                n = idx - r * nq
                q0 = pl.multiple_of(idx * p, p)
                k0 = pl.multiple_of(base + r * wk + n * p, p)
                s = lax.dot_general(qs_ref[pl.ds(q0, p), :], ks_ref[pl.ds(k0, 2 * p), :], nt,
                                    preferred_element_type=F32) + bias_ref[bi]
                no_history = jnp.logical_and(tile == 0, n == 0)
                s_ref[u * p:(u + 1) * p, :] = jnp.where(jnp.logical_and(no_history, in_prev_block), -jnp.inf, s)
                starts.append((k0, r + dil * p * n))
            s = s_ref[...]
            m = jnp.max(s, axis=-1, keepdims=True)
            e = jnp.exp(s - m)
            l = jnp.sum(e, axis=-1, keepdims=True)
            e_ref[...] = e.astype(BF16)
            inv_l = 1.0 / l
            lse = m + jnp.log(l)
            for u, (k0, t0) in enumerate(starts):
                o = jnp.dot(e_ref[u * p:(u + 1) * p, :], vs_ref[pl.ds(k0, 2 * p), :], preferred_element_type=F32)
                o_ref[bi, rows(t0, p, dil), :] = o * inv_l[u * p:(u + 1) * p]
                l_ref[bi, rows(t0, p, dil), :] = jnp.broadcast_to(lse[u * p:(u + 1) * p], (p, A_HEAD_DIM))
            return carry

        lax.fori_loop(0, (dil * nq) // ATT_UNROLL, group, 0)

    chunk = 256
    for c0 in range(0, t_len, chunk):
        sl = slice(c0, c0 + chunk)
        l1, l2, l3 = l_ref[0, sl, :], l_ref[1, sl, :], l_ref[2, sl, :]
        mx = jnp.maximum(jnp.maximum(l1, l2), l3)
        e1, e2, e3 = jnp.exp(l1 - mx), jnp.exp(l2 - mx), jnp.exp(l3 - mx)
        mix = (e1 * o_ref[0, sl, :] + e2 * o_ref[1, sl, :] + e3 * o_ref[2, sl, :]) / (e1 + e2 + e3)
        out_ref[sl, :] = mix.astype(out_ref.dtype)


def _dilated_mixture_attention(proj, bsz, seq, n_heads):
    m = proj.shape[0]
    hd = A_HEAD_DIM
    t_len = ATT_TILE
    nt = seq // t_len
    slopes = 2.0 ** (-8.0 * jnp.arange(1, n_heads + 1, dtype=F32) / n_heads)
    slopes = jnp.broadcast_to(slopes[:, None, None], (n_heads, 1, 2 * BAND_BLOCK))

    kv_rows = sum(_kv_rows(dil) for _, dil in DILATED_BRANCHES)

    def head_block(col):
        return pl.BlockSpec((t_len, hd), lambda b, h, t: (b * nt + t, col * n_heads + h))

    return pl.pallas_call(
        _attn_kernel,
        grid=(bsz, n_heads, nt),
        in_specs=[head_block(0), head_block(1), head_block(2),
                  pl.BlockSpec((1, 1, 2 * BAND_BLOCK), lambda b, h, t: (h, 0, 0))],
        out_specs=pl.BlockSpec((t_len, hd), lambda b, h, t: (b * nt + t, h)),
        out_shape=jax.ShapeDtypeStruct((m, n_heads * hd), BF16),
        scratch_shapes=[pltpu.VMEM((3, t_len, hd), F32),
                        pltpu.VMEM((t_len, hd), BF16),
                        pltpu.VMEM((kv_rows, hd), BF16),
                        pltpu.VMEM((kv_rows, hd), BF16),
                        pltpu.VMEM((3, t_len, hd), F32),
                        pltpu.VMEM((3, t_len, hd), F32),
                        pltpu.VMEM((3, BAND_BLOCK, 2 * BAND_BLOCK), F32),
                        pltpu.VMEM((ATT_UNROLL * BAND_BLOCK, 2 * BAND_BLOCK), F32),
                        pltpu.VMEM((ATT_UNROLL * BAND_BLOCK, 2 * BAND_BLOCK), BF16)],
        compiler_params=_params(("parallel", "parallel", "arbitrary")),
        name="dilated_attention",
    )(proj, proj, proj, slopes)


def _conv_kernel(ts, ch, a_ref, g_ref, ha_ref, hg_ref, w_ref, cb_ref, lg_ref, lb_ref, o_ref, h_ref, hs_ref, c_ref):
    t = pl.program_id(1)
    hist = ha_ref[...] * jax.nn.sigmoid(hg_ref[...])
    h_ref[0:CONV_HALO, :] = jnp.where(t > 0, hist, 0.0)
    h_ref[CONV_HALO:, :] = a_ref[...] * jax.nn.sigmoid(g_ref[...])
    first = CONV_HALO - (CONV_WIDTH - 1)
    span = hs_ref.shape[1]
    for s in range(1, SUBLANES):
        hs_ref[s - 1] = h_ref[s:s + span, :]
    rows, lanes = 32, 512
    for r0 in range(0, ts, rows):
        for c0 in range(0, ch, lanes):
            acc = jnp.broadcast_to(cb_ref[:, c0:c0 + lanes], (rows, lanes))
            for j in range(CONV_WIDTH):
                a8, s = (first + j) // SUBLANES * SUBLANES, (first + j) % SUBLANES
                if s == 0:
                    tap = h_ref[r0 + a8:r0 + a8 + rows, c0:c0 + lanes]
                else:
                    tap = hs_ref[s - 1, r0 + a8:r0 + a8 + rows, c0:c0 + lanes]
                acc = acc + w_ref[j:j + 1, c0:c0 + lanes] * tap
            c_ref[r0:r0 + rows, c0:c0 + lanes] = acc
    y = _layer_norm(c_ref[...], lg_ref[...], lb_ref[...])
    o_ref[...] = (y * jax.nn.sigmoid(y)).astype(o_ref.dtype)


def _conformer_conv(proj, col0, bsz, seq, conv_w, conv_b, ln_g, ln_b, ts=256):
    m = proj.shape[0]
    ch = conv_w.shape[1]
    ca, cg = col0 // ch, col0 // ch + 1
    w_pad = jnp.zeros((CONV_HALO, ch), F32).at[:CONV_WIDTH].set(conv_w)
    nt = seq // ts
    per = ts // CONV_HALO

    def halo(col):
        return pl.BlockSpec((CONV_HALO, ch), lambda b, t: (jnp.maximum((b * nt + t) * per - 1, 0), col))

    return pl.pallas_call(
        functools.partial(_conv_kernel, ts, ch),
        grid=(bsz, nt),
        in_specs=[pl.BlockSpec((ts, ch), lambda b, t: (b * nt + t, ca)),
                  pl.BlockSpec((ts, ch), lambda b, t: (b * nt + t, cg)),
                  halo(ca), halo(cg),
                  pl.BlockSpec((CONV_HALO, ch), lambda b, t: (0, 0)),
                  pl.BlockSpec((1, ch), lambda b, t: (0, 0)),
                  pl.BlockSpec((1, ch), lambda b, t: (0, 0)),
                  pl.BlockSpec((1, ch), lambda b, t: (0, 0))],
        out_specs=pl.BlockSpec((ts, ch), lambda b, t: (b * nt + t, 0)),
        out_shape=jax.ShapeDtypeStruct((m, ch), BF16),
        scratch_shapes=[pltpu.VMEM((ts + CONV_HALO, ch), F32),
                        pltpu.VMEM((SUBLANES - 1, ts + CONV_HALO - SUBLANES, ch), F32),
                        pltpu.VMEM((ts, ch), F32)],
        compiler_params=_params(("parallel", "arbitrary")),
        name="conformer_conv",
    )(proj, proj, proj, proj, w_pad, conv_b.reshape(1, ch), ln_g.reshape(1, ch), ln_b.reshape(1, ch))


def _gates_kernel(x_ref, w_ref, bias_ref, gc_ref, gr_ref):
    length = x_ref.shape[0]
    pre = jnp.dot(x_ref[...], w_ref[...], preferred_element_type=F32) + bias_ref[...]
    log_f = jnp.minimum(pre, 0.0) - jnp.log1p(jnp.exp(-jnp.abs(pre)))
    ti = lax.broadcasted_iota(jnp.int32, (length, length), 0)
    si = lax.broadcasted_iota(jnp.int32, (length, length), 1)
    tril = (ti >= si).astype(F32)
    cum_f = jnp.dot(tril, log_f, preferred_element_type=F32, precision=lax.Precision.HIGHEST)
    lane = lax.broadcasted_iota(jnp.int32, pre.shape, 1)
    gates = jnp.where(lane < C_HEADS, pre, cum_f)
    gc_ref[...] = gates
    gr_ref[...] = gates.T


def _mlstm_gates(x, w_gate, bias):
    m, d = x.shape
    length = MLSTM_CHUNK
    return pl.pallas_call(
        _gates_kernel,
        grid=(m // length,),
        in_specs=[pl.BlockSpec((length, d), lambda i: (i, 0)),
                  pl.BlockSpec((d, GATE_LANES), lambda i: (0, 0)),
                  pl.BlockSpec((1, GATE_LANES), lambda i: (0, 0))],
        out_specs=[pl.BlockSpec((length, GATE_LANES), lambda i: (i, 0)),
                   pl.BlockSpec((GATE_LANES, length), lambda i: (0, i))],
        out_shape=[jax.ShapeDtypeStruct((m, GATE_LANES), F32),
                   jax.ShapeDtypeStruct((GATE_LANES, m), F32)],
        compiler_params=_params(("parallel",)),
        name="mlstm_gates",
    )(x, w_gate, bias)


def _mlstm_kernel(dk, dv, q_ref, k_ref, v_ref, o_ref, gc_ref, gr_ref, ng_ref, y_ref, c_ref, n_ref, m_ref):
    length = q_ref.shape[0]

    @pl.when(pl.program_id(1) == 0)
    def _():
        c_ref[...] = jnp.zeros_like(c_ref)
        n_ref[...] = jnp.zeros_like(n_ref)
        m_ref[...] = jnp.zeros_like(m_ref)

    ti = lax.broadcasted_iota(jnp.int32, (length, length), 0)
    si = lax.broadcasted_iota(jnp.int32, (length, length), 1)
    causal = ti >= si
    for h in range(C_HEADS):
        q = q_ref[:, h * dk:(h + 1) * dk]
        k = k_ref[:, h * dk:(h + 1) * dk]
        v = v_ref[:, h * dv:(h + 1) * dv]
        i_row = gr_ref[h:h + 1, :]
        b_row = gr_ref[C_HEADS + h:C_HEADS + h + 1, :]
        i_col = gc_ref[:, h:h + 1]
        b_col = gc_ref[:, C_HEADS + h:C_HEADS + h + 1]
        m_prev = m_ref[h:h + 1, 0:1]
        c_prev = c_ref[h]
        n_prev = n_ref[h:h + 1, :]

        dmat = jnp.where(causal, b_col - b_row + i_row, -jnp.inf)
        inter = b_col + m_prev
        m_t = jnp.maximum(inter, jnp.max(dmat, axis=-1, keepdims=True))
        w = jnp.exp(dmat - m_t)
        a = jnp.exp(inter - m_t)
        qk = lax.dot_general(q, k, (((1,), (1,)), ((), ())), preferred_element_type=F32)
        sqk = qk * w
        num = (a * jnp.dot(q, c_prev.astype(BF16), preferred_element_type=F32)
               + jnp.dot(sqk.astype(BF16), v, preferred_element_type=F32))
        den = (a * jnp.sum(q.astype(F32) * n_prev, axis=-1, keepdims=True)
               + jnp.sum(sqk, axis=-1, keepdims=True))
        hh = num / jnp.maximum(jnp.abs(den), jnp.exp(-m_t))

        b_last = b_col[length - 1:length, :]
        g = b_last - b_col + i_col
        m_new = jnp.maximum(b_last + m_prev, jnp.max(g, axis=0, keepdims=True))
        decay = jnp.exp(b_last + m_prev - m_new)
        kw = k.astype(F32) * jnp.exp(g - m_new)
        c_ref[h] = decay * c_prev + lax.dot_general(kw.astype(BF16), v, (((0,), (0,)), ((), ())),
                                                    preferred_element_type=F32)
        n_ref[h:h + 1, :] = decay * n_prev + jnp.sum(kw, axis=0, keepdims=True)
        m_ref[h:h + 1, :] = jnp.broadcast_to(m_new, (1, m_ref.shape[1]))

        mu = jnp.mean(hh, axis=-1, keepdims=True)
        hc = hh - mu
        var = jnp.mean(hc * hc, axis=-1, keepdims=True)
        hn = hc * lax.rsqrt(var + LN_EPS) * ng_ref[:, h * dv:(h + 1) * dv]
        y_ref[:, h * dv:(h + 1) * dv] = (jax.nn.sigmoid(o_ref[:, h * dv:(h + 1) * dv]) * hn).astype(y_ref.dtype)


def _mlstm(qkv, o, gates_col, gates_row, norm_g, bsz, seq, dk, dv):
    length = MLSTM_CHUNK
    nc = seq // length
    qk_w = C_HEADS * dk
    v_w = C_HEADS * dv
    v_blk = (2 * qk_w) // v_w
    return pl.pallas_call(
        functools.partial(_mlstm_kernel, dk, dv),
        grid=(bsz, nc),
        in_specs=[pl.BlockSpec((length, qk_w), lambda b, c: (b * nc + c, 0)),
                  pl.BlockSpec((length, qk_w), lambda b, c: (b * nc + c, 1)),
                  pl.BlockSpec((length, v_w), lambda b, c: (b * nc + c, v_blk)),
                  pl.BlockSpec((length, v_w), lambda b, c: (b * nc + c, 0)),
                  pl.BlockSpec((length, GATE_LANES), lambda b, c: (b * nc + c, 0)),
                  pl.BlockSpec((GATE_LANES, length), lambda b, c: (0, b * nc + c)),
                  pl.BlockSpec((1, v_w), lambda b, c: (0, 0))],
        out_specs=pl.BlockSpec((length, v_w), lambda b, c: (b * nc + c, 0)),
        out_shape=jax.ShapeDtypeStruct((bsz * seq, v_w), BF16),
        scratch_shapes=[pltpu.VMEM((C_HEADS, dk, dv), F32),
                        pltpu.VMEM((8, dk), F32),
                        pltpu.VMEM((8, GATE_LANES), F32)],
        compiler_params=_params(("parallel", "arbitrary")),
        name="mlstm_chunk",
    )(qkv, qkv, qkv, o, gates_col, gates_row, norm_g)


def _even_mixer(xb, x, w_in, j, conv_w, conv_b, conv_ln_g, conv_ln_b, w_out, ln_g, ln_b, alpha, bsz, seq):
    d = x.shape[1]
    a_width = d // 2
    n_heads = a_width // A_HEAD_DIM
    in_width = w_in.shape[2]
    col_scale = jnp.ones((1, in_width), F32).at[:, :a_width].set(A_HEAD_DIM ** -0.5)
    proj = _matmul(xb, w_in, j, 0, in_width, col_scale, F32)
    att = _dilated_mixture_attention(proj, bsz, seq, n_heads)
    conv = _conformer_conv(proj, 3 * a_width, bsz, seq, conv_w, conv_b, conv_ln_g, conv_ln_b)
    return _out_proj_ln([att, conv], w_out, j, x, ln_g.reshape(1, d), ln_b.reshape(1, d), alpha)


def _odd_mixer(xb, x, w_in, j, igate_b, fgate_b, norm_g, w_out, ln_g, ln_b, alpha, bsz, seq):
    d = x.shape[1]
    v_w = d
    dv = v_w // C_HEADS
    dk = dv // 2
    qk_w = C_HEADS * dk
    qkv_w = 2 * qk_w + v_w
    qkv_scale = jnp.ones((1, qkv_w), F32).at[:, :qk_w].set(dk ** -0.5)
    qkv = _matmul(xb, w_in, j, 0, qkv_w, qkv_scale, BF16)
    o = _matmul(xb, w_in, j, qkv_w, v_w, jnp.ones((1, v_w), F32), F32)
    w_gate = jnp.zeros((d, GATE_LANES), F32).at[:, :2 * C_HEADS].set(w_in[j, :, qkv_w + v_w:])
    bias = jnp.zeros((1, GATE_LANES), F32).at[0, :C_HEADS].set(igate_b).at[0, C_HEADS:2 * C_HEADS].set(fgate_b)
    gates_col, gates_row = _mlstm_gates(xb, w_gate.astype(BF16), bias)
    y = _mlstm(qkv, o, gates_col, gates_row, norm_g.reshape(1, v_w), bsz, seq, dk, dv)
    return _out_proj_ln([y], w_out, j, x, ln_g.reshape(1, d), ln_b.reshape(1, d), alpha)


def kernel(x, even_w_in, even_conv_w, even_conv_b, even_conv_ln_g, even_conv_ln_b, even_w_out,
           odd_w_in, odd_igate_b, odd_fgate_b, odd_norm_g, odd_w_out, mix_ln_g, mix_ln_b,
           ffn_w1, ffn_w2, ffn_ln_g, ffn_ln_b):
    bsz, seq, d = x.shape
    depth = mix_ln_g.shape[0]
    alpha = (2 * depth) ** 0.25
    xf = x.reshape(bsz * seq, d)
    xb = xf.astype(BF16)
    for layer in range(depth):
        j = layer // 2
        if layer % 2 == 0:
            xf, xb = _even_mixer(xb, xf, even_w_in, j, even_conv_w[j], even_conv_b[j], even_conv_ln_g[j],
                                 even_conv_ln_b[j], even_w_out, mix_ln_g[layer], mix_ln_b[layer],
                                 alpha, bsz, seq)
        else:
            xf, xb = _odd_mixer(xb, xf, odd_w_in, j, odd_igate_b[j], odd_fgate_b[j], odd_norm_g[j],
                                odd_w_out, mix_ln_g[layer], mix_ln_b[layer], alpha, bsz, seq)
        xf, xb = _ffn_ln(xf, ffn_w1, ffn_w2, layer,
                         ffn_ln_g[layer].reshape(1, d), ffn_ln_b[layer].reshape(1, d), alpha)
    return xf.reshape(bsz, seq, d)
```

```python
import functools

import jax
import jax.numpy as jnp
from jax import lax
from jax.experimental import pallas as pl
from jax.experimental.pallas import tpu as pltpu

LN_EPS = 1e-5
DILATED_BRANCHES = ((128, 1), (512, 4), (2048, 16))
BAND_BLOCK = 128
A_HEAD_DIM = 128
ATT_TILE = 2048
ATT_UNROLL = 16
CONV_WIDTH = 31
CONV_HALO = 32
C_HEADS = 4
MLSTM_CHUNK = 256
GATE_LANES = 128
SUBLANES = 8
V7X_VMEM_LIMIT = 56 * 1024 * 1024

BF16 = jnp.bfloat16
F32 = jnp.float32


def _params(semantics):
    return pltpu.CompilerParams(dimension_semantics=semantics, vmem_limit_bytes=V7X_VMEM_LIMIT)


def _layer_norm(z, g, b):
    mu = jnp.mean(z, axis=-1, keepdims=True)
    zc = z - mu
    var = jnp.mean(zc * zc, axis=-1, keepdims=True)
    return zc * lax.rsqrt(var + LN_EPS) * g + b


def _matmul_kernel(w_is_transposed, x_ref, w_ref, s_ref, o_ref, wb_ref):
    @pl.when(pl.program_id(1) == 0)
    def _():
        w = w_ref[...]
        wb_ref[...] = (w.T if w_is_transposed else w).astype(BF16)

    acc = jnp.dot(x_ref[...], wb_ref[...], preferred_element_type=F32)
    o_ref[...] = (acc * s_ref[...]).astype(o_ref.dtype)


def _matmul(x, w_stack, layer, col0, n_cols, col_scale, out_dtype, w_is_transposed=False, tm=1024, tn=1024):
    m, k = x.shape
    cb0 = col0 // tn
    if w_is_transposed:
        w_spec = pl.BlockSpec((None, tn, k), lambda j, i: (layer, cb0 + j, 0))
    else:
        w_spec = pl.BlockSpec((None, k, tn), lambda j, i: (layer, 0, cb0 + j))
    return pl.pallas_call(
        functools.partial(_matmul_kernel, w_is_transposed),
        grid=(n_cols // tn, m // tm),
        in_specs=[pl.BlockSpec((tm, k), lambda j, i: (i, 0)),
                  w_spec,
                  pl.BlockSpec((1, tn), lambda j, i: (0, j))],
        out_specs=pl.BlockSpec((tm, tn), lambda j, i: (i, j)),
        out_shape=jax.ShapeDtypeStruct((m, n_cols), out_dtype),
        scratch_shapes=[pltpu.VMEM((k, tn), BF16)],
        compiler_params=_params(("parallel", "arbitrary")),
        name="proj_matmul",
    )(x, w_stack, col_scale)


def _out_proj_ln_kernel(alpha, n_in, *refs):
    ys = refs[:n_in]
    w_ref, x_ref, g_ref, b_ref, of_ref, ob_ref, wb_ref = refs[n_in:]

    @pl.when(pl.program_id(0) == 0)
    def _():
        wb_ref[...] = w_ref[...].astype(BF16)

    acc = alpha * x_ref[...]
    k0 = 0
    for y_ref in ys:
        kw = y_ref.shape[1]
        acc = acc + jnp.dot(y_ref[...], wb_ref[k0:k0 + kw, :], preferred_element_type=F32)
        k0 += kw
    out = _layer_norm(acc, g_ref[...], b_ref[...])
    of_ref[...] = out
    ob_ref[...] = out.astype(BF16)


def _out_proj_ln(ys, w_stack, layer, x, g, b, alpha, tm=512):
    m, d = x.shape
    n_in = len(ys)
    k = w_stack.shape[1]
    in_specs = ([pl.BlockSpec((tm, y.shape[1]), lambda i: (i, 0)) for y in ys]
                + [pl.BlockSpec((None, k, d), lambda i: (layer, 0, 0), pipeline_mode=pl.Buffered(1)),
                   pl.BlockSpec((tm, d), lambda i: (i, 0)),
                   pl.BlockSpec((1, d), lambda i: (0, 0)),
                   pl.BlockSpec((1, d), lambda i: (0, 0))])
    return pl.pallas_call(
        functools.partial(_out_proj_ln_kernel, alpha, n_in),
        grid=(m // tm,),
        in_specs=in_specs,
        out_specs=[pl.BlockSpec((tm, d), lambda i: (i, 0)),
                   pl.BlockSpec((tm, d), lambda i: (i, 0))],
        out_shape=[jax.ShapeDtypeStruct((m, d), F32), jax.ShapeDtypeStruct((m, d), BF16)],
        scratch_shapes=[pltpu.VMEM((k, d), BF16)],
        compiler_params=_params(("arbitrary",)),
        name="out_proj_ln",
    )(*ys, w_stack, x, g, b)


def _ffn_kernel(alpha, x_ref, w1_ref, w2_ref, g_ref, b_ref, of_ref, ob_ref, xb_ref):
    j = pl.program_id(1)

    @pl.when(j == 0)
    def _():
        x = x_ref[...]
        xb_ref[...] = x.astype(BF16)
        of_ref[...] = alpha * x

    h = jnp.dot(xb_ref[...], w1_ref[...].astype(BF16), preferred_element_type=F32)
    h = jnp.maximum(h, 0.0)
    h = (h * h).astype(BF16)
    of_ref[...] += jnp.dot(h, w2_ref[...].astype(BF16), preferred_element_type=F32)

    @pl.when(j == pl.num_programs(1) - 1)
    def _():
        out = _layer_norm(of_ref[...], g_ref[...], b_ref[...])
        of_ref[...] = out
        ob_ref[...] = out.astype(BF16)


def _ffn_ln(x, w1, w2, layer, g, b, alpha, tm=1024, th=512):
    m, d = x.shape
    f = w1.shape[2]
    once = pl.Buffered(1)
    return pl.pallas_call(
        functools.partial(_ffn_kernel, alpha),
        grid=(m // tm, f // th),
        in_specs=[pl.BlockSpec((tm, d), lambda i, j: (i, 0)),
                  pl.BlockSpec((None, d, th), lambda i, j: (layer, 0, j)),
                  pl.BlockSpec((None, th, d), lambda i, j: (layer, j, 0)),
                  pl.BlockSpec((1, d), lambda i, j: (0, 0)),
                  pl.BlockSpec((1, d), lambda i, j: (0, 0))],
        out_specs=[pl.BlockSpec((tm, d), lambda i, j: (i, 0), pipeline_mode=once),
                   pl.BlockSpec((tm, d), lambda i, j: (i, 0), pipeline_mode=once)],
        out_shape=[jax.ShapeDtypeStruct((m, d), F32), jax.ShapeDtypeStruct((m, d), BF16)],
        scratch_shapes=[pltpu.VMEM((tm, d), BF16)],
        compiler_params=_params(("parallel", "arbitrary")),
        name="ffn_ln",
    )(x, w1, w2, g, b)


def _kv_rows(dil):
    return dil * BAND_BLOCK + ATT_TILE


def _attn_kernel(q_ref, k_ref, v_ref, slope_ref, out_ref,
                 g4_ref, qs_ref, ks_ref, vs_ref, o_ref, l_ref, bias_ref, s_ref, e_ref):
    tile = pl.program_id(2)
    p = BAND_BLOCK
    t_len = ATT_TILE
    qi = lax.broadcasted_iota(jnp.int32, (p, 2 * p), 0)
    kj = lax.broadcasted_iota(jnp.int32, (p, 2 * p), 1)
    dist = p + qi - kj
    valid = jnp.logical_and(dist >= 0, dist <= p)
    in_prev_block = kj < p
    slope = slope_ref[0]
    for bi, (_, dil) in enumerate(DILATED_BRANCHES):
        bias_ref[bi] = jnp.where(valid, -(slope * (dist * dil).astype(F32)), -jnp.inf)

    def rows(start, size, dil):
        return pl.ds(start, size, stride=dil) if dil > 1 else pl.ds(start, size)

    kv_base = [sum(_kv_rows(d) for _, d in DILATED_BRANCHES[:bi]) for bi in range(len(DILATED_BRANCHES))]
    srcs = (q_ref, k_ref, v_ref)
    dil4 = DILATED_BRANCHES[1][1]
    w4 = t_len // dil4
    for x, src in enumerate(srcs):
        for r in range(dil4):
            g4_ref[x, r * w4:(r + 1) * w4, :] = src[rows(r, w4, dil4), :]

    def residue_rows(x, r, dil):
        if dil == 1:
            return srcs[x][...]
        if dil == dil4:
            return g4_ref[x, r * w4:(r + 1) * w4, :]
        return g4_ref[x, rows((r % dil4) * w4 + r // dil4, t_len // dil, dil // dil4), :]

    nt = (((1,), (1,)), ((), ()))
    for bi, (_, dil) in enumerate(DILATED_BRANCHES):
        wq = t_len // dil
        nq = wq // p
        wk = p + wq
        base = kv_base[bi]

        @pl.when(tile == 0)
        def _(dil=dil, wk=wk, base=base):
            for r in range(dil):
                for dst_ref in (ks_ref, vs_ref):
                    dst_ref[base + r * wk:base + r * wk + p, :] = jnp.zeros((p, A_HEAD_DIM), BF16)

        @pl.when(tile > 0)
        def _(dil=dil, wk=wk, wq=wq, base=base):
            for r in range(dil):
                for dst_ref in (ks_ref, vs_ref):
                    dst_ref[base + r * wk:base + r * wk + p, :] = dst_ref[base + r * wk + wq:base + (r + 1) * wk, :]

        for r in range(dil):
            qs_ref[r * wq:(r + 1) * wq, :] = residue_rows(0, r, dil).astype(BF16)
            ks_ref[base + r * wk + p:base + (r + 1) * wk, :] = residue_rows(1, r, dil).astype(BF16)
            vs_ref[base + r * wk + p:base + (r + 1) * wk, :] = residue_rows(2, r, dil).astype(BF16)

        def group(g, carry, dil=dil, nq=nq, wk=wk, bi=bi, base=base):
            starts = []
            for u in range(ATT_UNROLL):
                idx = g * ATT_UNROLL + u
                r = idx // nq
                n = idx - r * nq
                q0 = pl.multiple_of(idx * p, p)
                k0 = pl.multiple_of(base + r * wk + n * p, p)
                s = lax.dot_general(qs_ref[pl.ds(q0, p), :], ks_ref[pl.ds(k0, 2 * p), :], nt,
                                    preferred_element_type=F32) + bias_ref[bi]
                no_history = jnp.logical_and(tile == 0, n == 0)
                s_ref[u * p:(u + 1) * p, :] = jnp.where(jnp.logical_and(no_history, in_prev_block), -jnp.inf, s)
                starts.append((k0, r + dil * p * n))
            s = s_ref[...]
            m = jnp.max(s, axis=-1, keepdims=True)
            e = jnp.exp(s - m)
            l = jnp.sum(e, axis=-1, keepdims=True)
            e_ref[...] = e.astype(BF16)
            inv_l = 1.0 / l
            lse = m + jnp.log(l)
            for u, (k0, t0) in enumerate(starts):
                o = jnp.dot(e_ref[u * p:(u + 1) * p, :], vs_ref[pl.ds(k0, 2 * p), :], preferred_element_type=F32)
                o_ref[bi, rows(t0, p, dil), :] = o * inv_l[u * p:(u + 1) * p]
                l_ref[bi, rows(t0, p, dil), :] = jnp.broadcast_to(lse[u * p:(u + 1) * p], (p, A_HEAD_DIM))
            return carry

        lax.fori_loop(0, (dil * nq) // ATT_UNROLL, group, 0)

    chunk = 256
    for c0 in range(0, t_len, chunk):
        sl = slice(c0, c0 + chunk)
        l1, l2, l3 = l_ref[0, sl, :], l_ref[1, sl, :], l_ref[2, sl, :]
        mx = jnp.maximum(jnp.maximum(l1, l2), l3)
        e1, e2, e3 = jnp.exp(l1 - mx), jnp.exp(l2 - mx), jnp.exp(l3 - mx)
        mix = (e1 * o_ref[0, sl, :] + e2 * o_ref[1, sl, :] + e3 * o_ref[2, sl, :]) / (e1 + e2 + e3)
        out_ref[sl, :] = mix.astype(out_ref.dtype)


def _dilated_mixture_attention(proj, bsz, seq, n_heads):
    m = proj.shape[0]
    hd = A_HEAD_DIM
    t_len = ATT_TILE
    nt = seq // t_len
    slopes = 2.0 ** (-8.0 * jnp.arange(1, n_heads + 1, dtype=F32) / n_heads)
    slopes = jnp.broadcast_to(slopes[:, None, None], (n_heads, 1, 2 * BAND_BLOCK))

    kv_rows = sum(_kv_rows(dil) for _, dil in DILATED_BRANCHES)

    def head_block(col):
        return pl.BlockSpec((t_len, hd), lambda b, h, t: (b * nt + t, col * n_heads + h))

    return pl.pallas_call(
        _attn_kernel,
        grid=(bsz, n_heads, nt),
        in_specs=[head_block(0), head_block(1), head_block(2),
                  pl.BlockSpec((1, 1, 2 * BAND_BLOCK), lambda b, h, t: (h, 0, 0))],
        out_specs=pl.BlockSpec((t_len, hd), lambda b, h, t: (b * nt + t, h)),
        out_shape=jax.ShapeDtypeStruct((m, n_heads * hd), BF16),
        scratch_shapes=[pltpu.VMEM((3, t_len, hd), F32),
                        pltpu.VMEM((t_len, hd), BF16),
                        pltpu.VMEM((kv_rows, hd), BF16),
                        pltpu.VMEM((kv_rows, hd), BF16),
                        pltpu.VMEM((3, t_len, hd), F32),
                        pltpu.VMEM((3, t_len, hd), F32),
                        pltpu.VMEM((3, BAND_BLOCK, 2 * BAND_BLOCK), F32),
                        pltpu.VMEM((ATT_UNROLL * BAND_BLOCK, 2 * BAND_BLOCK), F32),
                        pltpu.VMEM((ATT_UNROLL * BAND_BLOCK, 2 * BAND_BLOCK), BF16)],
        compiler_params=_params(("parallel", "parallel", "arbitrary")),
        name="dilated_attention",
    )(proj, proj, proj, slopes)


def _conv_kernel(ts, ch, a_ref, g_ref, ha_ref, hg_ref, w_ref, cb_ref, lg_ref, lb_ref, o_ref, h_ref, hs_ref, c_ref):
    t = pl.program_id(1)
    hist = ha_ref[...] * jax.nn.sigmoid(hg_ref[...])
    h_ref[0:CONV_HALO, :] = jnp.where(t > 0, hist, 0.0)
    h_ref[CONV_HALO:, :] = a_ref[...] * jax.nn.sigmoid(g_ref[...])
    first = CONV_HALO - (CONV_WIDTH - 1)
    h_rows = h_ref.shape[0]
    for s in range(1, SUBLANES):
        hs_ref[s - 1, SUBLANES - s:SUBLANES - s + h_rows, :] = h_ref[...]
    rows, lanes = 32, 512
    groups = rows // SUBLANES
    for r0 in range(0, ts, rows):
        for c0 in range(0, ch, lanes):
            acc = jnp.broadcast_to(cb_ref[:, c0:c0 + lanes], (groups, SUBLANES, lanes))
            for j in range(CONV_WIDTH):
                a8, s = (first + j) // SUBLANES * SUBLANES, (first + j) % SUBLANES
                if s == 0:
                    tap = h_ref[r0 + a8:r0 + a8 + rows, c0:c0 + lanes]
                else:
                    tap = hs_ref[s - 1, SUBLANES + r0 + a8:SUBLANES + r0 + a8 + rows, c0:c0 + lanes]
                acc = acc + w_ref[j, :, c0:c0 + lanes] * tap.reshape(groups, SUBLANES, lanes)
            c_ref[r0:r0 + rows, c0:c0 + lanes] = acc.reshape(rows, lanes)
    y = _layer_norm(c_ref[...], lg_ref[...], lb_ref[...])
    o_ref[...] = (y * jax.nn.sigmoid(y)).astype(o_ref.dtype)


def _conformer_conv(proj, col0, bsz, seq, conv_w, conv_b, ln_g, ln_b, ts=256):
    m = proj.shape[0]
    ch = conv_w.shape[1]
    ca, cg = col0 // ch, col0 // ch + 1
    w_rows = jnp.broadcast_to(conv_w[:, None, :], (CONV_WIDTH, SUBLANES, ch))
    nt = seq // ts
    per = ts // CONV_HALO

    def halo(col):
        return pl.BlockSpec((CONV_HALO, ch), lambda b, t: (jnp.maximum((b * nt + t) * per - 1, 0), col))

    return pl.pallas_call(
        functools.partial(_conv_kernel, ts, ch),
        grid=(bsz, nt),
        in_specs=[pl.BlockSpec((ts, ch), lambda b, t: (b * nt + t, ca)),
                  pl.BlockSpec((ts, ch), lambda b, t: (b * nt + t, cg)),
                  halo(ca), halo(cg),
                  pl.BlockSpec((CONV_WIDTH, SUBLANES, ch), lambda b, t: (0, 0, 0)),
                  pl.BlockSpec((1, ch), lambda b, t: (0, 0)),
                  pl.BlockSpec((1, ch), lambda b, t: (0, 0)),
                  pl.BlockSpec((1, ch), lambda b, t: (0, 0))],
        out_specs=pl.BlockSpec((ts, ch), lambda b, t: (b * nt + t, 0)),
        out_shape=jax.ShapeDtypeStruct((m, ch), BF16),
        scratch_shapes=[pltpu.VMEM((ts + CONV_HALO, ch), F32),
                        pltpu.VMEM((SUBLANES - 1, ts + CONV_HALO + SUBLANES, ch), F32),
                        pltpu.VMEM((ts, ch), F32)],
        compiler_params=_params(("parallel", "arbitrary")),
        name="conformer_conv",
    )(proj, proj, proj, proj, w_rows, conv_b.reshape(1, ch), ln_g.reshape(1, ch), ln_b.reshape(1, ch))


def _gates_kernel(x_ref, w_ref, bias_ref, gc_ref, gr_ref):
    length = x_ref.shape[0]
    w = w_ref[...].astype(BF16)
    w = jnp.concatenate([w, jnp.zeros((GATE_LANES - w.shape[0], w.shape[1]), BF16)], axis=0)
    pre = lax.dot_general(x_ref[...], w, (((1,), (1,)), ((), ())), preferred_element_type=F32) + bias_ref[...]
    log_f = jnp.minimum(pre, 0.0) - jnp.log1p(jnp.exp(-jnp.abs(pre)))
    ti = lax.broadcasted_iota(jnp.int32, (length, length), 0)
    si = lax.broadcasted_iota(jnp.int32, (length, length), 1)
    tril = (ti >= si).astype(F32)
    cum_f = jnp.dot(tril, log_f, preferred_element_type=F32, precision=lax.Precision.HIGHEST)
    lane = lax.broadcasted_iota(jnp.int32, pre.shape, 1)
    gates = jnp.where(lane < C_HEADS, pre, cum_f)
    gc_ref[...] = gates
    gr_ref[...] = gates.T


def _mlstm_gates(x, w_stack_t, layer, gate_row0, bias):
    m, d = x.shape
    length = MLSTM_CHUNK
    gate_rows = 2 * C_HEADS
    gate_block = gate_row0 // gate_rows
    return pl.pallas_call(
        _gates_kernel,
        grid=(m // length,),
        in_specs=[pl.BlockSpec((length, d), lambda i: (i, 0)),
                  pl.BlockSpec((None, gate_rows, d), lambda i: (layer, gate_block, 0)),
                  pl.BlockSpec((1, GATE_LANES), lambda i: (0, 0))],
        out_specs=[pl.BlockSpec((length, GATE_LANES), lambda i: (i, 0)),
                   pl.BlockSpec((GATE_LANES, length), lambda i: (0, i))],
        out_shape=[jax.ShapeDtypeStruct((m, GATE_LANES), F32),
                   jax.ShapeDtypeStruct((GATE_LANES, m), F32)],
        compiler_params=_params(("parallel",)),
        name="mlstm_gates",
    )(x, w_stack_t, bias)


def _mlstm_kernel(dk, dv, q_ref, k_ref, v_ref, o_ref, gc_ref, gr_ref, ng_ref, y_ref, c_ref, n_ref, m_ref):
    length = q_ref.shape[0]

    @pl.when(pl.program_id(1) == 0)
    def _():
        c_ref[...] = jnp.zeros_like(c_ref)
        n_ref[...] = jnp.zeros_like(n_ref)
        m_ref[...] = jnp.zeros_like(m_ref)

    ti = lax.broadcasted_iota(jnp.int32, (length, length), 0)
    si = lax.broadcasted_iota(jnp.int32, (length, length), 1)
    causal = ti >= si
    for h in range(C_HEADS):
        q = q_ref[:, h * dk:(h + 1) * dk]
        k = k_ref[:, h * dk:(h + 1) * dk]
        v = v_ref[:, h * dv:(h + 1) * dv]
        i_row = gr_ref[h:h + 1, :]
        b_row = gr_ref[C_HEADS + h:C_HEADS + h + 1, :]
        i_col = gc_ref[:, h:h + 1]
        b_col = gc_ref[:, C_HEADS + h:C_HEADS + h + 1]
        m_prev = m_ref[h:h + 1, 0:1]
        c_prev = c_ref[h]
        n_prev = n_ref[h:h + 1, :]

        dmat = jnp.where(causal, b_col - b_row + i_row, -jnp.inf)
        inter = b_col + m_prev
        m_t = jnp.maximum(inter, jnp.max(dmat, axis=-1, keepdims=True))
        w = jnp.exp(dmat - m_t)
        a = jnp.exp(inter - m_t)
        qk = lax.dot_general(q, k, (((1,), (1,)), ((), ())), preferred_element_type=F32)
        sqk = qk * w
        num = (a * jnp.dot(q, c_prev.astype(BF16), preferred_element_type=F32)
               + jnp.dot(sqk.astype(BF16), v, preferred_element_type=F32))
        den = (a * jnp.sum(q.astype(F32) * n_prev, axis=-1, keepdims=True)
               + jnp.sum(sqk, axis=-1, keepdims=True))
        hh = num / jnp.maximum(jnp.abs(den), jnp.exp(-m_t))

        b_last = b_col[length - 1:length, :]
        g = b_last - b_col + i_col
        m_new = jnp.maximum(b_last + m_prev, jnp.max(g, axis=0, keepdims=True))
        decay = jnp.exp(b_last + m_prev - m_new)
        kw = k.astype(F32) * jnp.exp(g - m_new)
        c_ref[h] = decay * c_prev + lax.dot_general(kw.astype(BF16), v, (((0,), (0,)), ((), ())),
                                                    preferred_element_type=F32)
        n_ref[h:h + 1, :] = decay * n_prev + jnp.sum(kw, axis=0, keepdims=True)
        m_ref[h:h + 1, :] = jnp.broadcast_to(m_new, (1, m_ref.shape[1]))

        mu = jnp.mean(hh, axis=-1, keepdims=True)
        hc = hh - mu
        var = jnp.mean(hc * hc, axis=-1, keepdims=True)
        hn = hc * lax.rsqrt(var + LN_EPS) * ng_ref[:, h * dv:(h + 1) * dv]
        y_ref[:, h * dv:(h + 1) * dv] = (jax.nn.sigmoid(o_ref[:, h * dv:(h + 1) * dv]) * hn).astype(y_ref.dtype)


def _mlstm(qkv, o, gates_col, gates_row, norm_g, bsz, seq, dk, dv):
    length = MLSTM_CHUNK
    nc = seq // length
    qk_w = C_HEADS * dk
    v_w = C_HEADS * dv
    v_blk = (2 * qk_w) // v_w
    return pl.pallas_call(
        functools.partial(_mlstm_kernel, dk, dv),
        grid=(bsz, nc),
        in_specs=[pl.BlockSpec((length, qk_w), lambda b, c: (b * nc + c, 0)),
                  pl.BlockSpec((length, qk_w), lambda b, c: (b * nc + c, 1)),
                  pl.BlockSpec((length, v_w), lambda b, c: (b * nc + c, v_blk)),
                  pl.BlockSpec((length, v_w), lambda b, c: (b * nc + c, 0)),
                  pl.BlockSpec((length, GATE_LANES), lambda b, c: (b * nc + c, 0)),
                  pl.BlockSpec((GATE_LANES, length), lambda b, c: (0, b * nc + c)),
                  pl.BlockSpec((1, v_w), lambda b, c: (0, 0))],
        out_specs=pl.BlockSpec((length, v_w), lambda b, c: (b * nc + c, 0)),
        out_shape=jax.ShapeDtypeStruct((bsz * seq, v_w), BF16),
        scratch_shapes=[pltpu.VMEM((C_HEADS, dk, dv), F32),
                        pltpu.VMEM((8, dk), F32),
                        pltpu.VMEM((8, GATE_LANES), F32)],
        compiler_params=_params(("parallel", "arbitrary")),
        name="mlstm_chunk",
    )(qkv, qkv, qkv, o, gates_col, gates_row, norm_g)


def _even_mixer(xb, x, w_in, j, conv_w, conv_b, conv_ln_g, conv_ln_b, w_out, ln_g, ln_b, alpha, bsz, seq):
    d = x.shape[1]
    a_width = d // 2
    n_heads = a_width // A_HEAD_DIM
    in_width = w_in.shape[2]
    col_scale = jnp.ones((1, in_width), F32).at[:, :a_width].set(A_HEAD_DIM ** -0.5)
    proj = _matmul(xb, w_in, j, 0, in_width, col_scale, F32)
    att = _dilated_mixture_attention(proj, bsz, seq, n_heads)
    conv = _conformer_conv(proj, 3 * a_width, bsz, seq, conv_w, conv_b, conv_ln_g, conv_ln_b)
    return _out_proj_ln([att, conv], w_out, j, x, ln_g.reshape(1, d), ln_b.reshape(1, d), alpha)


def _odd_mixer(xb, x, w_in_t, j, igate_b, fgate_b, norm_g, w_out, ln_g, ln_b, alpha, bsz, seq):
    d = x.shape[1]
    v_w = d
    dv = v_w // C_HEADS
    dk = dv // 2
    qk_w = C_HEADS * dk
    qkv_w = 2 * qk_w + v_w
    qkv_scale = jnp.ones((1, qkv_w), F32).at[:, :qk_w].set(dk ** -0.5)
    qkv = _matmul(xb, w_in_t, j, 0, qkv_w, qkv_scale, BF16, w_is_transposed=True)
    o = _matmul(xb, w_in_t, j, qkv_w, v_w, jnp.ones((1, v_w), F32), F32, w_is_transposed=True)
    bias = jnp.zeros((1, GATE_LANES), F32).at[0, :C_HEADS].set(igate_b).at[0, C_HEADS:2 * C_HEADS].set(fgate_b)
    gates_col, gates_row = _mlstm_gates(xb, w_in_t, j, qkv_w + v_w, bias)
    y = _mlstm(qkv, o, gates_col, gates_row, norm_g.reshape(1, v_w), bsz, seq, dk, dv)
    return _out_proj_ln([y], w_out, j, x, ln_g.reshape(1, d), ln_b.reshape(1, d), alpha)


def kernel(x, even_w_in, even_conv_w, even_conv_b, even_conv_ln_g, even_conv_ln_b, even_w_out,
           odd_w_in, odd_igate_b, odd_fgate_b, odd_norm_g, odd_w_out, mix_ln_g, mix_ln_b,
           ffn_w1, ffn_w2, ffn_ln_g, ffn_ln_b):
    bsz, seq, d = x.shape
    depth = mix_ln_g.shape[0]
    alpha = (2 * depth) ** 0.25
    xf = x.reshape(bsz * seq, d)
    xb = xf.astype(BF16)
    odd_w_in_t = jnp.swapaxes(odd_w_in, 1, 2)
    for layer in range(depth):
        j = layer // 2
        if layer % 2 == 0:
            xf, xb = _even_mixer(xb, xf, even_w_in, j, even_conv_w[j], even_conv_b[j], even_conv_ln_g[j],
                                 even_conv_ln_b[j], even_w_out, mix_ln_g[layer], mix_ln_b[layer],
                                 alpha, bsz, seq)
        else:
            xf, xb = _odd_mixer(xb, xf, odd_w_in_t, j, odd_igate_b[j], odd_fgate_b[j], odd_norm_g[j],
                                odd_w_out, mix_ln_g[layer], mix_ln_b[layer], alpha, bsz, seq)
        xf, xb = _ffn_ln(xf, ffn_w1, ffn_w2, layer,
                         ffn_ln_g[layer].reshape(1, d), ffn_ln_b[layer].reshape(1, d), alpha)
    return xf.reshape(bsz, seq, d)
```

```python
import functools

import jax
import jax.numpy as jnp
from jax import lax
from jax.experimental import pallas as pl
from jax.experimental.pallas import tpu as pltpu

LN_EPS = 1e-5
DILATED_BRANCHES = ((128, 1), (512, 4), (2048, 16))
BAND_BLOCK = 128
A_HEAD_DIM = 128
ATT_TILE = 2048
ATT_UNROLL = 16
CONV_WIDTH = 31
CONV_HALO = 32
C_HEADS = 4
MLSTM_CHUNK = 256
GATE_LANES = 128
SUBLANES = 8
V7X_VMEM_LIMIT = 56 * 1024 * 1024

BF16 = jnp.bfloat16
F32 = jnp.float32


def _params(semantics):
    return pltpu.CompilerParams(dimension_semantics=semantics, vmem_limit_bytes=V7X_VMEM_LIMIT)


def _layer_norm(z, g, b):
    mu = jnp.mean(z, axis=-1, keepdims=True)
    zc = z - mu
    var = jnp.mean(zc * zc, axis=-1, keepdims=True)
    return zc * lax.rsqrt(var + LN_EPS) * g + b


def _matmul_kernel(w_is_transposed, x_ref, w_ref, s_ref, o_ref, wb_ref):
    @pl.when(pl.program_id(1) == 0)
    def _():
        w = w_ref[...]
        wb_ref[...] = (w.T if w_is_transposed else w).astype(BF16)

    acc = jnp.dot(x_ref[...], wb_ref[...], preferred_element_type=F32)
    o_ref[...] = (acc * s_ref[...]).astype(o_ref.dtype)


def _matmul(x, w_stack, layer, col0, n_cols, col_scale, out_dtype, w_is_transposed=False, tm=1024, tn=1024):
    m, k = x.shape
    cb0 = col0 // tn
    if w_is_transposed:
        w_spec = pl.BlockSpec((None, tn, k), lambda j, i: (layer, cb0 + j, 0))
    else:
        w_spec = pl.BlockSpec((None, k, tn), lambda j, i: (layer, 0, cb0 + j))
    return pl.pallas_call(
        functools.partial(_matmul_kernel, w_is_transposed),
        grid=(n_cols // tn, m // tm),
        in_specs=[pl.BlockSpec((tm, k), lambda j, i: (i, 0)),
                  w_spec,
                  pl.BlockSpec((1, tn), lambda j, i: (0, j))],
        out_specs=pl.BlockSpec((tm, tn), lambda j, i: (i, j)),
        out_shape=jax.ShapeDtypeStruct((m, n_cols), out_dtype),
        scratch_shapes=[pltpu.VMEM((k, tn), BF16)],
        compiler_params=_params(("parallel", "arbitrary")),
        name="proj_matmul",
    )(x, w_stack, col_scale)


def _out_proj_ln_kernel(alpha, n_in, *refs):
    ys = refs[:n_in]
    w_ref, x_ref, g_ref, b_ref, of_ref, ob_ref, wb_ref = refs[n_in:]

    @pl.when(pl.program_id(0) == 0)
    def _():
        wb_ref[...] = w_ref[...].astype(BF16)

    acc = alpha * x_ref[...]
    k0 = 0
    for y_ref in ys:
        kw = y_ref.shape[1]
        acc = acc + jnp.dot(y_ref[...], wb_ref[k0:k0 + kw, :], preferred_element_type=F32)
        k0 += kw
    out = _layer_norm(acc, g_ref[...], b_ref[...])
    of_ref[...] = out
    ob_ref[...] = out.astype(BF16)


def _out_proj_ln(ys, w_stack, layer, x, g, b, alpha, tm=512):
    m, d = x.shape
    n_in = len(ys)
    k = w_stack.shape[1]
    in_specs = ([pl.BlockSpec((tm, y.shape[1]), lambda i: (i, 0)) for y in ys]
                + [pl.BlockSpec((None, k, d), lambda i: (layer, 0, 0), pipeline_mode=pl.Buffered(1)),
                   pl.BlockSpec((tm, d), lambda i: (i, 0)),
                   pl.BlockSpec((1, d), lambda i: (0, 0)),
                   pl.BlockSpec((1, d), lambda i: (0, 0))])
    return pl.pallas_call(
        functools.partial(_out_proj_ln_kernel, alpha, n_in),
        grid=(m // tm,),
        in_specs=in_specs,
        out_specs=[pl.BlockSpec((tm, d), lambda i: (i, 0)),
                   pl.BlockSpec((tm, d), lambda i: (i, 0))],
        out_shape=[jax.ShapeDtypeStruct((m, d), F32), jax.ShapeDtypeStruct((m, d), BF16)],
        scratch_shapes=[pltpu.VMEM((k, d), BF16)],
        compiler_params=_params(("arbitrary",)),
        name="out_proj_ln",
    )(*ys, w_stack, x, g, b)


def _ffn_kernel(alpha, layer, th, n_blocks, x_ref, w1_hbm, w2_hbm, g_ref, b_ref, of_ref, ob_ref,
                xb_ref, w1_buf, w2_buf, sem):
    i = pl.program_id(0)
    last_tile = pl.num_programs(0) - 1

    def w_copies(j, slot):
        col = pl.multiple_of(j * th, th)
        return (pltpu.make_async_copy(w1_hbm.at[layer, :, pl.ds(col, th)], w1_buf.at[slot], sem.at[0, slot]),
                pltpu.make_async_copy(w2_hbm.at[layer, pl.ds(col, th), :], w2_buf.at[slot], sem.at[1, slot]))

    def start(j, slot):
        for copy in w_copies(j, slot):
            copy.start()

    def wait(j, slot):
        for copy in w_copies(j, slot):
            copy.wait()

    @pl.when(i == 0)
    def _():
        start(0, 0)

    x = x_ref[...]
    xb_ref[...] = x.astype(BF16)
    of_ref[...] = alpha * x

    def block(j, slot):
        wait(j, slot)
        h = jnp.dot(xb_ref[...], w1_buf[slot].astype(BF16), preferred_element_type=F32)
        h = jnp.maximum(h, 0.0)
        h = (h * h).astype(BF16)
        of_ref[...] += jnp.dot(h, w2_buf[slot].astype(BF16), preferred_element_type=F32)

    def pair(p, carry):
        j = 2 * p
        start(j + 1, 1)
        block(j, 0)

        @pl.when(j + 2 < n_blocks)
        def _():
            start(j + 2, 0)

        @pl.when(jnp.logical_and(j + 2 == n_blocks, i < last_tile))
        def _():
            start(0, 0)

        block(j + 1, 1)
        return carry

    lax.fori_loop(0, n_blocks // 2, pair, 0)

    out = _layer_norm(of_ref[...], g_ref[...], b_ref[...])
    of_ref[...] = out
    ob_ref[...] = out.astype(BF16)


def _ffn_ln(x, w1, w2, layer, g, b, alpha, tm=1024, th=512):
    m, d = x.shape
    f = w1.shape[2]
    n_blocks = f // th
    assert n_blocks % 2 == 0
    once = pl.Buffered(1)
    return pl.pallas_call(
        functools.partial(_ffn_kernel, alpha, layer, th, n_blocks),
        grid=(m // tm,),
        in_specs=[pl.BlockSpec((tm, d), lambda i: (i, 0)),
                  pl.BlockSpec(memory_space=pl.ANY),
                  pl.BlockSpec(memory_space=pl.ANY),
                  pl.BlockSpec((1, d), lambda i: (0, 0)),
                  pl.BlockSpec((1, d), lambda i: (0, 0))],
        out_specs=[pl.BlockSpec((tm, d), lambda i: (i, 0), pipeline_mode=once),
                   pl.BlockSpec((tm, d), lambda i: (i, 0), pipeline_mode=once)],
        out_shape=[jax.ShapeDtypeStruct((m, d), F32), jax.ShapeDtypeStruct((m, d), BF16)],
        scratch_shapes=[pltpu.VMEM((tm, d), BF16),
                        pltpu.VMEM((2, d, th), F32),
                        pltpu.VMEM((2, th, d), F32),
                        pltpu.SemaphoreType.DMA((2, 2))],
        compiler_params=_params(("arbitrary",)),
        name="ffn_ln",
    )(x, w1, w2, g, b)


def _kv_rows(dil):
    return dil * BAND_BLOCK + ATT_TILE


def _attn_kernel(q_ref, k_ref, v_ref, slope_ref, out_ref,
                 g4_ref, qs_ref, ks_ref, vs_ref, o_ref, l_ref, bias_ref, s_ref, e_ref):
    tile = pl.program_id(2)
    p = BAND_BLOCK
    t_len = ATT_TILE
    qi = lax.broadcasted_iota(jnp.int32, (p, 2 * p), 0)
    kj = lax.broadcasted_iota(jnp.int32, (p, 2 * p), 1)
    dist = p + qi - kj
    valid = jnp.logical_and(dist >= 0, dist <= p)
    in_prev_block = kj < p
    slope = slope_ref[0]
    for bi, (_, dil) in enumerate(DILATED_BRANCHES):
        bias_ref[bi] = jnp.where(valid, -(slope * (dist * dil).astype(F32)), -jnp.inf)

    def rows(start, size, dil):
        return pl.ds(start, size, stride=dil) if dil > 1 else pl.ds(start, size)

    kv_base = [sum(_kv_rows(d) for _, d in DILATED_BRANCHES[:bi]) for bi in range(len(DILATED_BRANCHES))]
    srcs = (q_ref, k_ref, v_ref)
    dil4 = DILATED_BRANCHES[1][1]
    w4 = t_len // dil4
    for x, src in enumerate(srcs):
        for r in range(dil4):
            g4_ref[x, r * w4:(r + 1) * w4, :] = src[rows(r, w4, dil4), :]

    def residue_rows(x, r, dil):
        if dil == 1:
            return srcs[x][...]
        if dil == dil4:
            return g4_ref[x, r * w4:(r + 1) * w4, :]
        return g4_ref[x, rows((r % dil4) * w4 + r // dil4, t_len // dil, dil // dil4), :]

    nt = (((1,), (1,)), ((), ()))
    for bi, (_, dil) in enumerate(DILATED_BRANCHES):
        wq = t_len // dil
        nq = wq // p
        wk = p + wq
        base = kv_base[bi]

        @pl.when(tile == 0)
        def _(dil=dil, wk=wk, base=base):
            for r in range(dil):
                for dst_ref in (ks_ref, vs_ref):
                    dst_ref[base + r * wk:base + r * wk + p, :] = jnp.zeros((p, A_HEAD_DIM), BF16)

        @pl.when(tile > 0)
        def _(dil=dil, wk=wk, wq=wq, base=base):
            for r in range(dil):
                for dst_ref in (ks_ref, vs_ref):
                    dst_ref[base + r * wk:base + r * wk + p, :] = dst_ref[base + r * wk + wq:base + (r + 1) * wk, :]

        for r in range(dil):
            qs_ref[r * wq:(r + 1) * wq, :] = residue_rows(0, r, dil).astype(BF16)
            ks_ref[base + r * wk + p:base + (r + 1) * wk, :] = residue_rows(1, r, dil).astype(BF16)
            vs_ref[base + r * wk + p:base + (r + 1) * wk, :] = residue_rows(2, r, dil).astype(BF16)

        def group(g, carry, dil=dil, nq=nq, wk=wk, bi=bi, base=base):
            starts = []
            for u in range(ATT_UNROLL):
                idx = g * ATT_UNROLL + u
                r = idx // nq
                n = idx - r * nq
                q0 = pl.multiple_of(idx * p, p)
                k0 = pl.multiple_of(base + r * wk + n * p, p)
                s = lax.dot_general(qs_ref[pl.ds(q0, p), :], ks_ref[pl.ds(k0, 2 * p), :], nt,
                                    preferred_element_type=F32) + bias_ref[bi]
                no_history = jnp.logical_and(tile == 0, n == 0)
                s_ref[u * p:(u + 1) * p, :] = jnp.where(jnp.logical_and(no_history, in_prev_block), -jnp.inf, s)
                starts.append((k0, r + dil * p * n))
            s = s_ref[...]
            m = jnp.max(s, axis=-1, keepdims=True)
            e = jnp.exp(s - m)
            l = jnp.sum(e, axis=-1, keepdims=True)
            e_ref[...] = e.astype(BF16)
            inv_l = 1.0 / l
            lse = m + jnp.log(l)
            for u, (k0, t0) in enumerate(starts):
                o = jnp.dot(e_ref[u * p:(u + 1) * p, :], vs_ref[pl.ds(k0, 2 * p), :], preferred_element_type=F32)
                o_ref[bi, rows(t0, p, dil), :] = o * inv_l[u * p:(u + 1) * p]
                l_ref[bi, rows(t0, p, dil), :] = jnp.broadcast_to(lse[u * p:(u + 1) * p], (p, A_HEAD_DIM))
            return carry

        lax.fori_loop(0, (dil * nq) // ATT_UNROLL, group, 0)

    chunk = 256
    for c0 in range(0, t_len, chunk):
        sl = slice(c0, c0 + chunk)
        l1, l2, l3 = l_ref[0, sl, :], l_ref[1, sl, :], l_ref[2, sl, :]
        mx = jnp.maximum(jnp.maximum(l1, l2), l3)
        e1, e2, e3 = jnp.exp(l1 - mx), jnp.exp(l2 - mx), jnp.exp(l3 - mx)
        mix = (e1 * o_ref[0, sl, :] + e2 * o_ref[1, sl, :] + e3 * o_ref[2, sl, :]) / (e1 + e2 + e3)
        out_ref[sl, :] = mix.astype(out_ref.dtype)


def _dilated_mixture_attention(proj, bsz, seq, n_heads):
    m = proj.shape[0]
    hd = A_HEAD_DIM
    t_len = ATT_TILE
    nt = seq // t_len
    slopes = 2.0 ** (-8.0 * jnp.arange(1, n_heads + 1, dtype=F32) / n_heads)
    slopes = jnp.broadcast_to(slopes[:, None, None], (n_heads, 1, 2 * BAND_BLOCK))

    kv_rows = sum(_kv_rows(dil) for _, dil in DILATED_BRANCHES)

    def head_block(col):
        return pl.BlockSpec((t_len, hd), lambda b, h, t: (b * nt + t, col * n_heads + h))

    return pl.pallas_call(
        _attn_kernel,
        grid=(bsz, n_heads, nt),
        in_specs=[head_block(0), head_block(1), head_block(2),
                  pl.BlockSpec((1, 1, 2 * BAND_BLOCK), lambda b, h, t: (h, 0, 0))],
        out_specs=pl.BlockSpec((t_len, hd), lambda b, h, t: (b * nt + t, h)),
        out_shape=jax.ShapeDtypeStruct((m, n_heads * hd), BF16),
        scratch_shapes=[pltpu.VMEM((3, t_len, hd), F32),
                        pltpu.VMEM((t_len, hd), BF16),
                        pltpu.VMEM((kv_rows, hd), BF16),
                        pltpu.VMEM((kv_rows, hd), BF16),
                        pltpu.VMEM((3, t_len, hd), F32),
                        pltpu.VMEM((3, t_len, hd), F32),
                        pltpu.VMEM((3, BAND_BLOCK, 2 * BAND_BLOCK), F32),
                        pltpu.VMEM((ATT_UNROLL * BAND_BLOCK, 2 * BAND_BLOCK), F32),
                        pltpu.VMEM((ATT_UNROLL * BAND_BLOCK, 2 * BAND_BLOCK), BF16)],
        compiler_params=_params(("parallel", "parallel", "arbitrary")),
        name="dilated_attention",
    )(proj, proj, proj, slopes)


def _conv_kernel(ts, ch, a_ref, g_ref, ha_ref, hg_ref, w_ref, cb_ref, lg_ref, lb_ref, o_ref, h_ref, hs_ref, c_ref):
    t = pl.program_id(1)
    hist = ha_ref[...] * jax.nn.sigmoid(hg_ref[...])
    h_ref[0:CONV_HALO, :] = jnp.where(t > 0, hist, 0.0)
    h_ref[CONV_HALO:, :] = a_ref[...] * jax.nn.sigmoid(g_ref[...])
    first = CONV_HALO - (CONV_WIDTH - 1)
    h_rows = h_ref.shape[0]
    for s in range(1, SUBLANES):
        hs_ref[s - 1, SUBLANES - s:SUBLANES - s + h_rows, :] = h_ref[...]
    rows, lanes = 32, 512
    groups = rows // SUBLANES
    for r0 in range(0, ts, rows):
        for c0 in range(0, ch, lanes):
            acc = jnp.broadcast_to(cb_ref[:, c0:c0 + lanes], (groups, SUBLANES, lanes))
            for j in range(CONV_WIDTH):
                a8, s = (first + j) // SUBLANES * SUBLANES, (first + j) % SUBLANES
                if s == 0:
                    tap = h_ref[r0 + a8:r0 + a8 + rows, c0:c0 + lanes]
                else:
                    tap = hs_ref[s - 1, SUBLANES + r0 + a8:SUBLANES + r0 + a8 + rows, c0:c0 + lanes]
                acc = acc + w_ref[j, :, c0:c0 + lanes] * tap.reshape(groups, SUBLANES, lanes)
            c_ref[r0:r0 + rows, c0:c0 + lanes] = acc.reshape(rows, lanes)
    y = _layer_norm(c_ref[...], lg_ref[...], lb_ref[...])
    o_ref[...] = (y * jax.nn.sigmoid(y)).astype(o_ref.dtype)


def _conformer_conv(proj, col0, bsz, seq, conv_w, conv_b, ln_g, ln_b, ts=256):
    m = proj.shape[0]
    ch = conv_w.shape[1]
    ca, cg = col0 // ch, col0 // ch + 1
    w_rows = jnp.broadcast_to(conv_w[:, None, :], (CONV_WIDTH, SUBLANES, ch))
    nt = seq // ts
    per = ts // CONV_HALO

    def halo(col):
        return pl.BlockSpec((CONV_HALO, ch), lambda b, t: (jnp.maximum((b * nt + t) * per - 1, 0), col))

    return pl.pallas_call(
        functools.partial(_conv_kernel, ts, ch),
        grid=(bsz, nt),
        in_specs=[pl.BlockSpec((ts, ch), lambda b, t: (b * nt + t, ca)),
                  pl.BlockSpec((ts, ch), lambda b, t: (b * nt + t, cg)),
                  halo(ca), halo(cg),
                  pl.BlockSpec((CONV_WIDTH, SUBLANES, ch), lambda b, t: (0, 0, 0)),
                  pl.BlockSpec((1, ch), lambda b, t: (0, 0)),
                  pl.BlockSpec((1, ch), lambda b, t: (0, 0)),
                  pl.BlockSpec((1, ch), lambda b, t: (0, 0))],
        out_specs=pl.BlockSpec((ts, ch), lambda b, t: (b * nt + t, 0)),
        out_shape=jax.ShapeDtypeStruct((m, ch), BF16),
        scratch_shapes=[pltpu.VMEM((ts + CONV_HALO, ch), F32),
                        pltpu.VMEM((SUBLANES - 1, ts + CONV_HALO + SUBLANES, ch), F32),
                        pltpu.VMEM((ts, ch), F32)],
        compiler_params=_params(("parallel", "arbitrary")),
        name="conformer_conv",
    )(proj, proj, proj, proj, w_rows, conv_b.reshape(1, ch), ln_g.reshape(1, ch), ln_b.reshape(1, ch))


def _gates_kernel(x_ref, w_ref, bias_ref, gc_ref, gr_ref):
    length = x_ref.shape[0]
    w = w_ref[...].astype(BF16)
    w = jnp.concatenate([w, jnp.zeros((GATE_LANES - w.shape[0], w.shape[1]), BF16)], axis=0)
    pre = lax.dot_general(x_ref[...], w, (((1,), (1,)), ((), ())), preferred_element_type=F32) + bias_ref[...]
    log_f = jnp.minimum(pre, 0.0) - jnp.log1p(jnp.exp(-jnp.abs(pre)))
    ti = lax.broadcasted_iota(jnp.int32, (length, length), 0)
    si = lax.broadcasted_iota(jnp.int32, (length, length), 1)
    tril = (ti >= si).astype(F32)
    cum_f = jnp.dot(tril, log_f, preferred_element_type=F32, precision=lax.Precision.HIGHEST)
    lane = lax.broadcasted_iota(jnp.int32, pre.shape, 1)
    gates = jnp.where(lane < C_HEADS, pre, cum_f)
    gc_ref[...] = gates
    gr_ref[...] = gates.T


def _mlstm_gates(x, w_stack_t, layer, gate_row0, bias):
    m, d = x.shape
    length = MLSTM_CHUNK
    gate_rows = 2 * C_HEADS
    gate_block = gate_row0 // gate_rows
    return pl.pallas_call(
        _gates_kernel,
        grid=(m // length,),
        in_specs=[pl.BlockSpec((length, d), lambda i: (i, 0)),
                  pl.BlockSpec((None, gate_rows, d), lambda i: (layer, gate_block, 0)),
                  pl.BlockSpec((1, GATE_LANES), lambda i: (0, 0))],
        out_specs=[pl.BlockSpec((length, GATE_LANES), lambda i: (i, 0)),
                   pl.BlockSpec((GATE_LANES, length), lambda i: (0, i))],
        out_shape=[jax.ShapeDtypeStruct((m, GATE_LANES), F32),
                   jax.ShapeDtypeStruct((GATE_LANES, m), F32)],
        compiler_params=_params(("parallel",)),
        name="mlstm_gates",
    )(x, w_stack_t, bias)


def _mlstm_kernel(dk, dv, q_ref, k_ref, v_ref, o_ref, gc_ref, gr_ref, ng_ref, y_ref, c_ref, n_ref, m_ref):
    length = q_ref.shape[0]

    @pl.when(pl.program_id(1) == 0)
    def _():
        c_ref[...] = jnp.zeros_like(c_ref)
        n_ref[...] = jnp.zeros_like(n_ref)
        m_ref[...] = jnp.zeros_like(m_ref)

    ti = lax.broadcasted_iota(jnp.int32, (length, length), 0)
    si = lax.broadcasted_iota(jnp.int32, (length, length), 1)
    causal = ti >= si
    for h in range(C_HEADS):
        q = q_ref[:, h * dk:(h + 1) * dk]
        k = k_ref[:, h * dk:(h + 1) * dk]
        v = v_ref[:, h * dv:(h + 1) * dv]
        i_row = gr_ref[h:h + 1, :]
        b_row = gr_ref[C_HEADS + h:C_HEADS + h + 1, :]
        i_col = gc_ref[:, h:h + 1]
        b_col = gc_ref[:, C_HEADS + h:C_HEADS + h + 1]
        m_prev = m_ref[h:h + 1, 0:1]
        c_prev = c_ref[h]
        n_prev = n_ref[h:h + 1, :]

        dmat = jnp.where(causal, b_col - b_row + i_row, -jnp.inf)
        inter = b_col + m_prev
        m_t = jnp.maximum(inter, jnp.max(dmat, axis=-1, keepdims=True))
        w = jnp.exp(dmat - m_t)
        a = jnp.exp(inter - m_t)
        qk = lax.dot_general(q, k, (((1,), (1,)), ((), ())), preferred_element_type=F32)
        sqk = qk * w
        num = (a * jnp.dot(q, c_prev.astype(BF16), preferred_element_type=F32)
               + jnp.dot(sqk.astype(BF16), v, preferred_element_type=F32))
        den = (a * jnp.sum(q.astype(F32) * n_prev, axis=-1, keepdims=True)
               + jnp.sum(sqk, axis=-1, keepdims=True))
        hh = num / jnp.maximum(jnp.abs(den), jnp.exp(-m_t))

        b_last = b_col[length - 1:length, :]
        g = b_last - b_col + i_col
        m_new = jnp.maximum(b_last + m_prev, jnp.max(g, axis=0, keepdims=True))
        decay = jnp.exp(b_last + m_prev - m_new)
        kw = k.astype(F32) * jnp.exp(g - m_new)
        c_ref[h] = decay * c_prev + lax.dot_general(kw.astype(BF16), v, (((0,), (0,)), ((), ())),
                                                    preferred_element_type=F32)
        n_ref[h:h + 1, :] = decay * n_prev + jnp.sum(kw, axis=0, keepdims=True)
        m_ref[h:h + 1, :] = jnp.broadcast_to(m_new, (1, m_ref.shape[1]))

        mu = jnp.mean(hh, axis=-1, keepdims=True)
        hc = hh - mu
        var = jnp.mean(hc * hc, axis=-1, keepdims=True)
        hn = hc * lax.rsqrt(var + LN_EPS) * ng_ref[:, h * dv:(h + 1) * dv]
        y_ref[:, h * dv:(h + 1) * dv] = (jax.nn.sigmoid(o_ref[:, h * dv:(h + 1) * dv]) * hn).astype(y_ref.dtype)


def _mlstm(qkv, o, gates_col, gates_row, norm_g, bsz, seq, dk, dv):
    length = MLSTM_CHUNK
    nc = seq // length
    qk_w = C_HEADS * dk
    v_w = C_HEADS * dv
    v_blk = (2 * qk_w) // v_w
    return pl.pallas_call(
        functools.partial(_mlstm_kernel, dk, dv),
        grid=(bsz, nc),
        in_specs=[pl.BlockSpec((length, qk_w), lambda b, c: (b * nc + c, 0)),
                  pl.BlockSpec((length, qk_w), lambda b, c: (b * nc + c, 1)),
                  pl.BlockSpec((length, v_w), lambda b, c: (b * nc + c, v_blk)),
                  pl.BlockSpec((length, v_w), lambda b, c: (b * nc + c, 0)),
                  pl.BlockSpec((length, GATE_LANES), lambda b, c: (b * nc + c, 0)),
                  pl.BlockSpec((GATE_LANES, length), lambda b, c: (0, b * nc + c)),
                  pl.BlockSpec((1, v_w), lambda b, c: (0, 0))],
        out_specs=pl.BlockSpec((length, v_w), lambda b, c: (b * nc + c, 0)),
        out_shape=jax.ShapeDtypeStruct((bsz * seq, v_w), BF16),
        scratch_shapes=[pltpu.VMEM((C_HEADS, dk, dv), F32),
                        pltpu.VMEM((8, dk), F32),
                        pltpu.VMEM((8, GATE_LANES), F32)],
        compiler_params=_params(("parallel", "arbitrary")),
        name="mlstm_chunk",
    )(qkv, qkv, qkv, o, gates_col, gates_row, norm_g)


def _even_mixer(xb, x, w_in, j, conv_w, conv_b, conv_ln_g, conv_ln_b, w_out, ln_g, ln_b, alpha, bsz, seq):
    d = x.shape[1]
    a_width = d // 2
    n_heads = a_width // A_HEAD_DIM
    in_width = w_in.shape[2]
    col_scale = jnp.ones((1, in_width), F32).at[:, :a_width].set(A_HEAD_DIM ** -0.5)
    proj = _matmul(xb, w_in, j, 0, in_width, col_scale, F32)
    att = _dilated_mixture_attention(proj, bsz, seq, n_heads)
    conv = _conformer_conv(proj, 3 * a_width, bsz, seq, conv_w, conv_b, conv_ln_g, conv_ln_b)
    return _out_proj_ln([att, conv], w_out, j, x, ln_g.reshape(1, d), ln_b.reshape(1, d), alpha)


def _odd_mixer(xb, x, w_in_t, j, igate_b, fgate_b, norm_g, w_out, ln_g, ln_b, alpha, bsz, seq):
    d = x.shape[1]
    v_w = d
    dv = v_w // C_HEADS
    dk = dv // 2
    qk_w = C_HEADS * dk
    qkv_w = 2 * qk_w + v_w
    qkv_scale = jnp.ones((1, qkv_w), F32).at[:, :qk_w].set(dk ** -0.5)
    qkv = _matmul(xb, w_in_t, j, 0, qkv_w, qkv_scale, BF16, w_is_transposed=True)
    o = _matmul(xb, w_in_t, j, qkv_w, v_w, jnp.ones((1, v_w), F32), F32, w_is_transposed=True)
    bias = jnp.zeros((1, GATE_LANES), F32).at[0, :C_HEADS].set(igate_b).at[0, C_HEADS:2 * C_HEADS].set(fgate_b)
    gates_col, gates_row = _mlstm_gates(xb, w_in_t, j, qkv_w + v_w, bias)
    y = _mlstm(qkv, o, gates_col, gates_row, norm_g.reshape(1, v_w), bsz, seq, dk, dv)
    return _out_proj_ln([y], w_out, j, x, ln_g.reshape(1, d), ln_b.reshape(1, d), alpha)


def kernel(x, even_w_in, even_conv_w, even_conv_b, even_conv_ln_g, even_conv_ln_b, even_w_out,
           odd_w_in, odd_igate_b, odd_fgate_b, odd_norm_g, odd_w_out, mix_ln_g, mix_ln_b,
           ffn_w1, ffn_w2, ffn_ln_g, ffn_ln_b):
    bsz, seq, d = x.shape
    depth = mix_ln_g.shape[0]
    alpha = (2 * depth) ** 0.25
    xf = x.reshape(bsz * seq, d)
    xb = xf.astype(BF16)
    odd_w_in_t = jnp.swapaxes(odd_w_in, 1, 2)
    for layer in range(depth):
        j = layer // 2
        if layer % 2 == 0:
            xf, xb = _even_mixer(xb, xf, even_w_in, j, even_conv_w[j], even_conv_b[j], even_conv_ln_g[j],
                                 even_conv_ln_b[j], even_w_out, mix_ln_g[layer], mix_ln_b[layer],
                                 alpha, bsz, seq)
        else:
            xf, xb = _odd_mixer(xb, xf, odd_w_in_t, j, odd_igate_b[j], odd_fgate_b[j], odd_norm_g[j],
                                odd_w_out, mix_ln_g[layer], mix_ln_b[layer], alpha, bsz, seq)
        xf, xb = _ffn_ln(xf, ffn_w1, ffn_w2, layer,
                         ffn_ln_g[layer].reshape(1, d), ffn_ln_b[layer].reshape(1, d), alpha)
    return xf.reshape(bsz, seq, d)
```

```python
import functools

import jax
import jax.numpy as jnp
from jax import lax
from jax.experimental import pallas as pl
from jax.experimental.pallas import tpu as pltpu

LN_EPS = 1e-5
DILATED_BRANCHES = ((128, 1), (512, 4), (2048, 16))
BAND_BLOCK = 128
A_HEAD_DIM = 128
ATT_TILE = 2048
ATT_UNROLL = 16
CONV_WIDTH = 31
CONV_HALO = 32
C_HEADS = 4
MLSTM_CHUNK = 256
GATE_LANES = 128
SUBLANES = 8
FFN_DMA_CHUNKS = 4
V7X_VMEM_LIMIT = 56 * 1024 * 1024
V7X_VMEM_LIMIT_FFN = 58 * 1024 * 1024

BF16 = jnp.bfloat16
F32 = jnp.float32


def _params(semantics, vmem_limit=V7X_VMEM_LIMIT):
    return pltpu.CompilerParams(dimension_semantics=semantics, vmem_limit_bytes=vmem_limit)


def _layer_norm(z, g, b):
    mu = jnp.mean(z, axis=-1, keepdims=True)
    zc = z - mu
    var = jnp.mean(zc * zc, axis=-1, keepdims=True)
    return zc * lax.rsqrt(var + LN_EPS) * g + b


def _matmul_kernel(w_is_transposed, x_ref, w_ref, s_ref, o_ref, wb_ref):
    @pl.when(pl.program_id(1) == 0)
    def _():
        w = w_ref[...]
        wb_ref[...] = (w.T if w_is_transposed else w).astype(BF16)

    acc = jnp.dot(x_ref[...], wb_ref[...], preferred_element_type=F32)
    o_ref[...] = (acc * s_ref[...]).astype(o_ref.dtype)


def _matmul(x, w_stack, layer, col0, n_cols, col_scale, out_dtype, w_is_transposed=False, tm=1024, tn=1024):
    m, k = x.shape
    cb0 = col0 // tn
    if w_is_transposed:
        w_spec = pl.BlockSpec((None, tn, k), lambda j, i: (layer, cb0 + j, 0))
    else:
        w_spec = pl.BlockSpec((None, k, tn), lambda j, i: (layer, 0, cb0 + j))
    return pl.pallas_call(
        functools.partial(_matmul_kernel, w_is_transposed),
        grid=(n_cols // tn, m // tm),
        in_specs=[pl.BlockSpec((tm, k), lambda j, i: (i, 0)),
                  w_spec,
                  pl.BlockSpec((1, tn), lambda j, i: (0, j))],
        out_specs=pl.BlockSpec((tm, tn), lambda j, i: (i, j)),
        out_shape=jax.ShapeDtypeStruct((m, n_cols), out_dtype),
        scratch_shapes=[pltpu.VMEM((k, tn), BF16)],
        compiler_params=_params(("parallel", "arbitrary")),
        name="proj_matmul",
    )(x, w_stack, col_scale)


def _out_proj_ln_kernel(alpha, n_in, *refs):
    ys = refs[:n_in]
    w_ref, x_ref, g_ref, b_ref, of_ref, ob_ref, wb_ref = refs[n_in:]

    @pl.when(pl.program_id(0) == 0)
    def _():
        wb_ref[...] = w_ref[...].astype(BF16)

    acc = alpha * x_ref[...]
    k0 = 0
    for y_ref in ys:
        kw = y_ref.shape[1]
        acc = acc + jnp.dot(y_ref[...], wb_ref[k0:k0 + kw, :], preferred_element_type=F32)
        k0 += kw
    out = _layer_norm(acc, g_ref[...], b_ref[...])
    of_ref[...] = out
    ob_ref[...] = out.astype(BF16)


def _out_proj_ln(ys, w_stack, layer, x, g, b, alpha, tm=512):
    m, d = x.shape
    n_in = len(ys)
    k = w_stack.shape[1]
    in_specs = ([pl.BlockSpec((tm, y.shape[1]), lambda i: (i, 0)) for y in ys]
                + [pl.BlockSpec((None, k, d), lambda i: (layer, 0, 0), pipeline_mode=pl.Buffered(1)),
                   pl.BlockSpec((tm, d), lambda i: (i, 0)),
                   pl.BlockSpec((1, d), lambda i: (0, 0)),
                   pl.BlockSpec((1, d), lambda i: (0, 0))])
    return pl.pallas_call(
        functools.partial(_out_proj_ln_kernel, alpha, n_in),
        grid=(m // tm,),
        in_specs=in_specs,
        out_specs=[pl.BlockSpec((tm, d), lambda i: (i, 0)),
                   pl.BlockSpec((tm, d), lambda i: (i, 0))],
        out_shape=[jax.ShapeDtypeStruct((m, d), F32), jax.ShapeDtypeStruct((m, d), BF16)],
        scratch_shapes=[pltpu.VMEM((k, d), BF16)],
        compiler_params=_params(("arbitrary",)),
        name="out_proj_ln",
    )(*ys, w_stack, x, g, b)


def _ffn_kernel(alpha, layer, tm, th, n_blocks, x_hbm, w1_hbm, w2_hbm, g_ref, b_ref, of_hbm, ob_hbm,
                x_buf, xb_ref, acc_ref, of_buf, ob_buf, w1_buf, w2_buf, sem_x, sem_w, sem_o):
    i = pl.program_id(0)
    last_tile = pl.num_programs(0) - 1
    chunks = FFN_DMA_CHUNKS

    def w_copies(j, slot):
        col = pl.multiple_of(j * th, th)
        r1 = w1_buf.shape[1] // chunks
        r2 = th // chunks
        copies = []
        for c in range(chunks):
            copies.append(pltpu.make_async_copy(
                w1_hbm.at[layer, pl.ds(c * r1, r1), pl.ds(col, th)],
                w1_buf.at[slot, pl.ds(c * r1, r1), :], sem_w.at[0, slot, c]))
            copies.append(pltpu.make_async_copy(
                w2_hbm.at[layer, pl.ds(col + c * r2, r2), :],
                w2_buf.at[slot, pl.ds(c * r2, r2), :], sem_w.at[1, slot, c]))
        return copies

    def x_copies(tile):
        rows = tm // chunks
        row0 = pl.multiple_of(tile * tm, tm)
        return [pltpu.make_async_copy(x_hbm.at[pl.ds(row0 + c * rows, rows), :],
                                      x_buf.at[pl.ds(c * rows, rows), :], sem_x.at[c]) for c in range(chunks)]

    def out_copies(tile):
        rows = tm // chunks
        row0 = pl.multiple_of(tile * tm, tm)
        copies = []
        for c in range(chunks):
            copies.append(pltpu.make_async_copy(of_buf.at[pl.ds(c * rows, rows), :],
                                                of_hbm.at[pl.ds(row0 + c * rows, rows), :], sem_o.at[0, c]))
            copies.append(pltpu.make_async_copy(ob_buf.at[pl.ds(c * rows, rows), :],
                                                ob_hbm.at[pl.ds(row0 + c * rows, rows), :], sem_o.at[1, c]))
        return copies

    def start(copies):
        for copy in copies:
            copy.start()

    def wait(copies):
        for copy in copies:
            copy.wait()

    @pl.when(i == 0)
    def _():
        start(x_copies(0))
        start(w_copies(0, 0))

    wait(x_copies(i))
    x = x_buf[...]
    xb_ref[...] = x.astype(BF16)
    acc_ref[...] = alpha * x

    @pl.when(i < last_tile)
    def _():
        start(x_copies(i + 1))

    def block(j, slot):
        wait(w_copies(j, slot))
        h = jnp.dot(xb_ref[...], w1_buf[slot].astype(BF16), preferred_element_type=F32)
        h = jnp.maximum(h, 0.0)
        h = (h * h).astype(BF16)
        acc_ref[...] += jnp.dot(h, w2_buf[slot].astype(BF16), preferred_element_type=F32)

    def pair(p, carry):
        j = 2 * p
        start(w_copies(j + 1, 1))
        block(j, 0)

        @pl.when(j + 2 < n_blocks)
        def _():
            start(w_copies(j + 2, 0))

        @pl.when(jnp.logical_and(j + 2 == n_blocks, i < last_tile))
        def _():
            start(w_copies(0, 0))

        block(j + 1, 1)
        return carry

    lax.fori_loop(0, n_blocks // 2, pair, 0)

    @pl.when(i > 0)
    def _():
        wait(out_copies(i - 1))

    out = _layer_norm(acc_ref[...], g_ref[...], b_ref[...])
    of_buf[...] = out
    ob_buf[...] = out.astype(BF16)
    start(out_copies(i))

    @pl.when(i == last_tile)
    def _():
        wait(out_copies(i))


def _ffn_ln(x, w1, w2, layer, g, b, alpha, tm=1024, th=512):
    m, d = x.shape
    f = w1.shape[2]
    n_blocks = f // th
    assert n_blocks % 2 == 0 and m % tm == 0
    hbm = pl.BlockSpec(memory_space=pl.ANY)
    return pl.pallas_call(
        functools.partial(_ffn_kernel, alpha, layer, tm, th, n_blocks),
        grid=(m // tm,),
        in_specs=[hbm, hbm, hbm,
                  pl.BlockSpec((1, d), lambda i: (0, 0)),
                  pl.BlockSpec((1, d), lambda i: (0, 0))],
        out_specs=[hbm, hbm],
        out_shape=[jax.ShapeDtypeStruct((m, d), F32), jax.ShapeDtypeStruct((m, d), BF16)],
        scratch_shapes=[pltpu.VMEM((tm, d), F32),
                        pltpu.VMEM((tm, d), BF16),
                        pltpu.VMEM((tm, d), F32),
                        pltpu.VMEM((tm, d), F32),
                        pltpu.VMEM((tm, d), BF16),
                        pltpu.VMEM((2, d, th), F32),
                        pltpu.VMEM((2, th, d), F32),
                        pltpu.SemaphoreType.DMA((FFN_DMA_CHUNKS,)),
                        pltpu.SemaphoreType.DMA((2, 2, FFN_DMA_CHUNKS)),
                        pltpu.SemaphoreType.DMA((2, FFN_DMA_CHUNKS))],
        compiler_params=_params(("arbitrary",), V7X_VMEM_LIMIT_FFN),
        name="ffn_ln",
    )(x, w1, w2, g, b)


def _kv_rows(dil):
    return dil * BAND_BLOCK + ATT_TILE


def _attn_kernel(q_ref, k_ref, v_ref, slope_ref, out_ref,
                 g4_ref, qs_ref, ks_ref, vs_ref, o_ref, l_ref, bias_ref, s_ref, e_ref):
    tile = pl.program_id(2)
    p = BAND_BLOCK
    t_len = ATT_TILE
    qi = lax.broadcasted_iota(jnp.int32, (p, 2 * p), 0)
    kj = lax.broadcasted_iota(jnp.int32, (p, 2 * p), 1)
    dist = p + qi - kj
    valid = jnp.logical_and(dist >= 0, dist <= p)
    in_prev_block = kj < p
    slope = slope_ref[0]
    for bi, (_, dil) in enumerate(DILATED_BRANCHES):
        bias_ref[bi] = jnp.where(valid, -(slope * (dist * dil).astype(F32)), -jnp.inf)

    def rows(start, size, dil):
        return pl.ds(start, size, stride=dil) if dil > 1 else pl.ds(start, size)

    kv_base = [sum(_kv_rows(d) for _, d in DILATED_BRANCHES[:bi]) for bi in range(len(DILATED_BRANCHES))]
    srcs = (q_ref, k_ref, v_ref)
    dil4 = DILATED_BRANCHES[1][1]
    w4 = t_len // dil4
    for x, src in enumerate(srcs):
        for r in range(dil4):
            g4_ref[x, r * w4:(r + 1) * w4, :] = src[rows(r, w4, dil4), :]

    def residue_rows(x, r, dil):
        if dil == 1:
            return srcs[x][...]
        if dil == dil4:
            return g4_ref[x, r * w4:(r + 1) * w4, :]
        return g4_ref[x, rows((r % dil4) * w4 + r // dil4, t_len // dil, dil // dil4), :]

    nt = (((1,), (1,)), ((), ()))
    for bi, (_, dil) in enumerate(DILATED_BRANCHES):
        wq = t_len // dil
        nq = wq // p
        wk = p + wq
        base = kv_base[bi]

        @pl.when(tile == 0)
        def _(dil=dil, wk=wk, base=base):
            for r in range(dil):
                for dst_ref in (ks_ref, vs_ref):
                    dst_ref[base + r * wk:base + r * wk + p, :] = jnp.zeros((p, A_HEAD_DIM), BF16)

        @pl.when(tile > 0)
        def _(dil=dil, wk=wk, wq=wq, base=base):
            for r in range(dil):
                for dst_ref in (ks_ref, vs_ref):
                    dst_ref[base + r * wk:base + r * wk + p, :] = dst_ref[base + r * wk + wq:base + (r + 1) * wk, :]

        for r in range(dil):
            qs_ref[r * wq:(r + 1) * wq, :] = residue_rows(0, r, dil).astype(BF16)
            ks_ref[base + r * wk + p:base + (r + 1) * wk, :] = residue_rows(1, r, dil).astype(BF16)
            vs_ref[base + r * wk + p:base + (r + 1) * wk, :] = residue_rows(2, r, dil).astype(BF16)

        def group(g, carry, dil=dil, nq=nq, wk=wk, bi=bi, base=base):
            starts = []
            for u in range(ATT_UNROLL):
                idx = g * ATT_UNROLL + u
                r = idx // nq
                n = idx - r * nq
                q0 = pl.multiple_of(idx * p, p)
                k0 = pl.multiple_of(base + r * wk + n * p, p)
                s = lax.dot_general(qs_ref[pl.ds(q0, p), :], ks_ref[pl.ds(k0, 2 * p), :], nt,
                                    preferred_element_type=F32) + bias_ref[bi]
                no_history = jnp.logical_and(tile == 0, n == 0)
                s_ref[u * p:(u + 1) * p, :] = jnp.where(jnp.logical_and(no_history, in_prev_block), -jnp.inf, s)
                starts.append((k0, r + dil * p * n))
            s = s_ref[...]
            m = jnp.max(s, axis=-1, keepdims=True)
            e = jnp.exp(s - m)
            l = jnp.sum(e, axis=-1, keepdims=True)
            e_ref[...] = e.astype(BF16)
            inv_l = 1.0 / l
            lse = m + jnp.log(l)
            for u, (k0, t0) in enumerate(starts):
                o = jnp.dot(e_ref[u * p:(u + 1) * p, :], vs_ref[pl.ds(k0, 2 * p), :], preferred_element_type=F32)
                o_ref[bi, rows(t0, p, dil), :] = o * inv_l[u * p:(u + 1) * p]
                l_ref[bi, rows(t0, p, dil), :] = jnp.broadcast_to(lse[u * p:(u + 1) * p], (p, A_HEAD_DIM))
            return carry

        lax.fori_loop(0, (dil * nq) // ATT_UNROLL, group, 0)

    chunk = 256
    for c0 in range(0, t_len, chunk):
        sl = slice(c0, c0 + chunk)
        l1, l2, l3 = l_ref[0, sl, :], l_ref[1, sl, :], l_ref[2, sl, :]
        mx = jnp.maximum(jnp.maximum(l1, l2), l3)
        e1, e2, e3 = jnp.exp(l1 - mx), jnp.exp(l2 - mx), jnp.exp(l3 - mx)
        mix = (e1 * o_ref[0, sl, :] + e2 * o_ref[1, sl, :] + e3 * o_ref[2, sl, :]) / (e1 + e2 + e3)
        out_ref[sl, :] = mix.astype(out_ref.dtype)


def _dilated_mixture_attention(proj, bsz, seq, n_heads):
    m = proj.shape[0]
    hd = A_HEAD_DIM
    t_len = ATT_TILE
    nt = seq // t_len
    slopes = 2.0 ** (-8.0 * jnp.arange(1, n_heads + 1, dtype=F32) / n_heads)
    slopes = jnp.broadcast_to(slopes[:, None, None], (n_heads, 1, 2 * BAND_BLOCK))

    kv_rows = sum(_kv_rows(dil) for _, dil in DILATED_BRANCHES)

    def head_block(col):
        return pl.BlockSpec((t_len, hd), lambda b, h, t: (b * nt + t, col * n_heads + h))

    return pl.pallas_call(
        _attn_kernel,
        grid=(bsz, n_heads, nt),
        in_specs=[head_block(0), head_block(1), head_block(2),
                  pl.BlockSpec((1, 1, 2 * BAND_BLOCK), lambda b, h, t: (h, 0, 0))],
        out_specs=pl.BlockSpec((t_len, hd), lambda b, h, t: (b * nt + t, h)),
        out_shape=jax.ShapeDtypeStruct((m, n_heads * hd), BF16),
        scratch_shapes=[pltpu.VMEM((3, t_len, hd), F32),
                        pltpu.VMEM((t_len, hd), BF16),
                        pltpu.VMEM((kv_rows, hd), BF16),
                        pltpu.VMEM((kv_rows, hd), BF16),
                        pltpu.VMEM((3, t_len, hd), F32),
                        pltpu.VMEM((3, t_len, hd), F32),
                        pltpu.VMEM((3, BAND_BLOCK, 2 * BAND_BLOCK), F32),
                        pltpu.VMEM((ATT_UNROLL * BAND_BLOCK, 2 * BAND_BLOCK), F32),
                        pltpu.VMEM((ATT_UNROLL * BAND_BLOCK, 2 * BAND_BLOCK), BF16)],
        compiler_params=_params(("parallel", "parallel", "arbitrary")),
        name="dilated_attention",
    )(proj, proj, proj, slopes)


def _conv_kernel(ts, ch, a_ref, g_ref, ha_ref, hg_ref, w_ref, cb_ref, lg_ref, lb_ref, o_ref, h_ref, hs_ref, c_ref):
    t = pl.program_id(1)
    hist = ha_ref[...] * jax.nn.sigmoid(hg_ref[...])
    h_ref[0:CONV_HALO, :] = jnp.where(t > 0, hist, 0.0)
    h_ref[CONV_HALO:, :] = a_ref[...] * jax.nn.sigmoid(g_ref[...])
    first = CONV_HALO - (CONV_WIDTH - 1)
    h_rows = h_ref.shape[0]
    for s in range(1, SUBLANES):
        hs_ref[s - 1, SUBLANES - s:SUBLANES - s + h_rows, :] = h_ref[...]
    rows, lanes = 32, 512
    groups = rows // SUBLANES
    for r0 in range(0, ts, rows):
        for c0 in range(0, ch, lanes):
            acc = jnp.broadcast_to(cb_ref[:, c0:c0 + lanes], (groups, SUBLANES, lanes))
            for j in range(CONV_WIDTH):
                a8, s = (first + j) // SUBLANES * SUBLANES, (first + j) % SUBLANES
                if s == 0:
                    tap = h_ref[r0 + a8:r0 + a8 + rows, c0:c0 + lanes]
                else:
                    tap = hs_ref[s - 1, SUBLANES + r0 + a8:SUBLANES + r0 + a8 + rows, c0:c0 + lanes]
                acc = acc + w_ref[j, :, c0:c0 + lanes] * tap.reshape(groups, SUBLANES, lanes)
            c_ref[r0:r0 + rows, c0:c0 + lanes] = acc.reshape(rows, lanes)
    y = _layer_norm(c_ref[...], lg_ref[...], lb_ref[...])
    o_ref[...] = (y * jax.nn.sigmoid(y)).astype(o_ref.dtype)


def _conformer_conv(proj, col0, bsz, seq, conv_w, conv_b, ln_g, ln_b, ts=256):
    m = proj.shape[0]
    ch = conv_w.shape[1]
    ca, cg = col0 // ch, col0 // ch + 1
    w_rows = jnp.broadcast_to(conv_w[:, None, :], (CONV_WIDTH, SUBLANES, ch))
    nt = seq // ts
    per = ts // CONV_HALO

    def halo(col):
        return pl.BlockSpec((CONV_HALO, ch), lambda b, t: (jnp.maximum((b * nt + t) * per - 1, 0), col))

    return pl.pallas_call(
        functools.partial(_conv_kernel, ts, ch),
        grid=(bsz, nt),
        in_specs=[pl.BlockSpec((ts, ch), lambda b, t: (b * nt + t, ca)),
                  pl.BlockSpec((ts, ch), lambda b, t: (b * nt + t, cg)),
                  halo(ca), halo(cg),
                  pl.BlockSpec((CONV_WIDTH, SUBLANES, ch), lambda b, t: (0, 0, 0)),
                  pl.BlockSpec((1, ch), lambda b, t: (0, 0)),
                  pl.BlockSpec((1, ch), lambda b, t: (0, 0)),
                  pl.BlockSpec((1, ch), lambda b, t: (0, 0))],
        out_specs=pl.BlockSpec((ts, ch), lambda b, t: (b * nt + t, 0)),
        out_shape=jax.ShapeDtypeStruct((m, ch), BF16),
        scratch_shapes=[pltpu.VMEM((ts + CONV_HALO, ch), F32),
                        pltpu.VMEM((SUBLANES - 1, ts + CONV_HALO + SUBLANES, ch), F32),
                        pltpu.VMEM((ts, ch), F32)],
        compiler_params=_params(("parallel", "arbitrary")),
        name="conformer_conv",
    )(proj, proj, proj, proj, w_rows, conv_b.reshape(1, ch), ln_g.reshape(1, ch), ln_b.reshape(1, ch))


def _gates_kernel(x_ref, w_ref, bias_ref, gc_ref, gr_ref):
    length = x_ref.shape[0]
    w = w_ref[...].astype(BF16)
    w = jnp.concatenate([w, jnp.zeros((GATE_LANES - w.shape[0], w.shape[1]), BF16)], axis=0)
    pre = lax.dot_general(x_ref[...], w, (((1,), (1,)), ((), ())), preferred_element_type=F32) + bias_ref[...]
    log_f = jnp.minimum(pre, 0.0) - jnp.log1p(jnp.exp(-jnp.abs(pre)))
    ti = lax.broadcasted_iota(jnp.int32, (length, length), 0)
    si = lax.broadcasted_iota(jnp.int32, (length, length), 1)
    tril = (ti >= si).astype(F32)
    cum_f = jnp.dot(tril, log_f, preferred_element_type=F32, precision=lax.Precision.HIGHEST)
    lane = lax.broadcasted_iota(jnp.int32, pre.shape, 1)
    gates = jnp.where(lane < C_HEADS, pre, cum_f)
    gc_ref[...] = gates
    gr_ref[...] = gates.T


def _mlstm_gates(x, w_stack_t, layer, gate_row0, bias):
    m, d = x.shape
    length = MLSTM_CHUNK
    gate_rows = 2 * C_HEADS
    gate_block = gate_row0 // gate_rows
    return pl.pallas_call(
        _gates_kernel,
        grid=(m // length,),
        in_specs=[pl.BlockSpec((length, d), lambda i: (i, 0)),
                  pl.BlockSpec((None, gate_rows, d), lambda i: (layer, gate_block, 0)),
                  pl.BlockSpec((1, GATE_LANES), lambda i: (0, 0))],
        out_specs=[pl.BlockSpec((length, GATE_LANES), lambda i: (i, 0)),
                   pl.BlockSpec((GATE_LANES, length), lambda i: (0, i))],
        out_shape=[jax.ShapeDtypeStruct((m, GATE_LANES), F32),
                   jax.ShapeDtypeStruct((GATE_LANES, m), F32)],
        compiler_params=_params(("parallel",)),
        name="mlstm_gates",
    )(x, w_stack_t, bias)


def _mlstm_kernel(dk, dv, q_ref, k_ref, v_ref, o_ref, gc_ref, gr_ref, ng_ref, y_ref, c_ref, n_ref, m_ref):
    length = q_ref.shape[0]

    @pl.when(pl.program_id(1) == 0)
    def _():
        c_ref[...] = jnp.zeros_like(c_ref)
        n_ref[...] = jnp.zeros_like(n_ref)
        m_ref[...] = jnp.zeros_like(m_ref)

    ti = lax.broadcasted_iota(jnp.int32, (length, length), 0)
    si = lax.broadcasted_iota(jnp.int32, (length, length), 1)
    causal = ti >= si
    for h in range(C_HEADS):
        q = q_ref[:, h * dk:(h + 1) * dk]
        k = k_ref[:, h * dk:(h + 1) * dk]
        v = v_ref[:, h * dv:(h + 1) * dv]
        i_row = gr_ref[h:h + 1, :]
        b_row = gr_ref[C_HEADS + h:C_HEADS + h + 1, :]
        i_col = gc_ref[:, h:h + 1]
        b_col = gc_ref[:, C_HEADS + h:C_HEADS + h + 1]
        m_prev = m_ref[h:h + 1, 0:1]
        c_prev = c_ref[h]
        n_prev = n_ref[h:h + 1, :]

        dmat = jnp.where(causal, b_col - b_row + i_row, -jnp.inf)
        inter = b_col + m_prev
        m_t = jnp.maximum(inter, jnp.max(dmat, axis=-1, keepdims=True))
        w = jnp.exp(dmat - m_t)
        a = jnp.exp(inter - m_t)
        qk = lax.dot_general(q, k, (((1,), (1,)), ((), ())), preferred_element_type=F32)
        sqk = qk * w
        num = (a * jnp.dot(q, c_prev.astype(BF16), preferred_element_type=F32)
               + jnp.dot(sqk.astype(BF16), v, preferred_element_type=F32))
        den = (a * jnp.sum(q.astype(F32) * n_prev, axis=-1, keepdims=True)
               + jnp.sum(sqk, axis=-1, keepdims=True))
        hh = num / jnp.maximum(jnp.abs(den), jnp.exp(-m_t))

        b_last = b_col[length - 1:length, :]
        g = b_last - b_col + i_col
        m_new = jnp.maximum(b_last + m_prev, jnp.max(g, axis=0, keepdims=True))
        decay = jnp.exp(b_last + m_prev - m_new)
        kw = k.astype(F32) * jnp.exp(g - m_new)
        c_ref[h] = decay * c_prev + lax.dot_general(kw.astype(BF16), v, (((0,), (0,)), ((), ())),
                                                    preferred_element_type=F32)
        n_ref[h:h + 1, :] = decay * n_prev + jnp.sum(kw, axis=0, keepdims=True)
        m_ref[h:h + 1, :] = jnp.broadcast_to(m_new, (1, m_ref.shape[1]))

        mu = jnp.mean(hh, axis=-1, keepdims=True)
        hc = hh - mu
        var = jnp.mean(hc * hc, axis=-1, keepdims=True)
        hn = hc * lax.rsqrt(var + LN_EPS) * ng_ref[:, h * dv:(h + 1) * dv]
        y_ref[:, h * dv:(h + 1) * dv] = (jax.nn.sigmoid(o_ref[:, h * dv:(h + 1) * dv]) * hn).astype(y_ref.dtype)


def _mlstm(qkv, o, gates_col, gates_row, norm_g, bsz, seq, dk, dv):
    length = MLSTM_CHUNK
    nc = seq // length
    qk_w = C_HEADS * dk
    v_w = C_HEADS * dv
    v_blk = (2 * qk_w) // v_w
    return pl.pallas_call(
        functools.partial(_mlstm_kernel, dk, dv),
        grid=(bsz, nc),
        in_specs=[pl.BlockSpec((length, qk_w), lambda b, c: (b * nc + c, 0)),
                  pl.BlockSpec((length, qk_w), lambda b, c: (b * nc + c, 1)),
                  pl.BlockSpec((length, v_w), lambda b, c: (b * nc + c, v_blk)),
                  pl.BlockSpec((length, v_w), lambda b, c: (b * nc + c, 0)),
                  pl.BlockSpec((length, GATE_LANES), lambda b, c: (b * nc + c, 0)),
                  pl.BlockSpec((GATE_LANES, length), lambda b, c: (0, b * nc + c)),
                  pl.BlockSpec((1, v_w), lambda b, c: (0, 0))],
        out_specs=pl.BlockSpec((length, v_w), lambda b, c: (b * nc + c, 0)),
        out_shape=jax.ShapeDtypeStruct((bsz * seq, v_w), BF16),
        scratch_shapes=[pltpu.VMEM((C_HEADS, dk, dv), F32),
                        pltpu.VMEM((8, dk), F32),
                        pltpu.VMEM((8, GATE_LANES), F32)],
        compiler_params=_params(("parallel", "arbitrary")),
        name="mlstm_chunk",
    )(qkv, qkv, qkv, o, gates_col, gates_row, norm_g)


def _even_mixer(xb, x, w_in, j, conv_w, conv_b, conv_ln_g, conv_ln_b, w_out, ln_g, ln_b, alpha, bsz, seq):
    d = x.shape[1]
    a_width = d // 2
    n_heads = a_width // A_HEAD_DIM
    in_width = w_in.shape[2]
    col_scale = jnp.ones((1, in_width), F32).at[:, :a_width].set(A_HEAD_DIM ** -0.5)
    proj = _matmul(xb, w_in, j, 0, in_width, col_scale, F32)
    att = _dilated_mixture_attention(proj, bsz, seq, n_heads)
    conv = _conformer_conv(proj, 3 * a_width, bsz, seq, conv_w, conv_b, conv_ln_g, conv_ln_b)
    return _out_proj_ln([att, conv], w_out, j, x, ln_g.reshape(1, d), ln_b.reshape(1, d), alpha)


def _odd_mixer(xb, x, w_in_t, j, igate_b, fgate_b, norm_g, w_out, ln_g, ln_b, alpha, bsz, seq):
    d = x.shape[1]
    v_w = d
    dv = v_w // C_HEADS
    dk = dv // 2
    qk_w = C_HEADS * dk
    qkv_w = 2 * qk_w + v_w
    qkv_scale = jnp.ones((1, qkv_w), F32).at[:, :qk_w].set(dk ** -0.5)
    qkv = _matmul(xb, w_in_t, j, 0, qkv_w, qkv_scale, BF16, w_is_transposed=True)
    o = _matmul(xb, w_in_t, j, qkv_w, v_w, jnp.ones((1, v_w), F32), F32, w_is_transposed=True)
    bias = jnp.zeros((1, GATE_LANES), F32).at[0, :C_HEADS].set(igate_b).at[0, C_HEADS:2 * C_HEADS].set(fgate_b)
    gates_col, gates_row = _mlstm_gates(xb, w_in_t, j, qkv_w + v_w, bias)
    y = _mlstm(qkv, o, gates_col, gates_row, norm_g.reshape(1, v_w), bsz, seq, dk, dv)
    return _out_proj_ln([y], w_out, j, x, ln_g.reshape(1, d), ln_b.reshape(1, d), alpha)


def kernel(x, even_w_in, even_conv_w, even_conv_b, even_conv_ln_g, even_conv_ln_b, even_w_out,
           odd_w_in, odd_igate_b, odd_fgate_b, odd_norm_g, odd_w_out, mix_ln_g, mix_ln_b,
           ffn_w1, ffn_w2, ffn_ln_g, ffn_ln_b):
    bsz, seq, d = x.shape
    depth = mix_ln_g.shape[0]
    alpha = (2 * depth) ** 0.25
    xf = x.reshape(bsz * seq, d)
    xb = xf.astype(BF16)
    odd_w_in_t = jnp.swapaxes(odd_w_in, 1, 2)
    for layer in range(depth):
        j = layer // 2
        if layer % 2 == 0:
            xf, xb = _even_mixer(xb, xf, even_w_in, j, even_conv_w[j], even_conv_b[j], even_conv_ln_g[j],
                                 even_conv_ln_b[j], even_w_out, mix_ln_g[layer], mix_ln_b[layer],
                                 alpha, bsz, seq)
        else:
            xf, xb = _odd_mixer(xb, xf, odd_w_in_t, j, odd_igate_b[j], odd_fgate_b[j], odd_norm_g[j],
                                odd_w_out, mix_ln_g[layer], mix_ln_b[layer], alpha, bsz, seq)
        xf, xb = _ffn_ln(xf, ffn_w1, ffn_w2, layer,
                         ffn_ln_g[layer].reshape(1, d), ffn_ln_b[layer].reshape(1, d), alpha)
    return xf.reshape(bsz, seq, d)
```

```python
import functools

import jax
import jax.numpy as jnp
from jax import lax
from jax.experimental import pallas as pl
from jax.experimental.pallas import tpu as pltpu

LN_EPS = 1e-5
DILATED_BRANCHES = ((128, 1), (512, 4), (2048, 16))
BAND_BLOCK = 128
A_HEAD_DIM = 128
ATT_TILE = 2048
ATT_UNROLL = 16
CONV_WIDTH = 31
CONV_HALO = 32
C_HEADS = 4
MLSTM_CHUNK = 256
GATE_LANES = 128
SUBLANES = 8
FFN_DMA_CHUNKS = 4
QKV_ROW_CHUNKS = 2
V7X_VMEM_LIMIT = 56 * 1024 * 1024
V7X_VMEM_LIMIT_FFN = 58 * 1024 * 1024

BF16 = jnp.bfloat16
F32 = jnp.float32


def _params(semantics, vmem_limit=V7X_VMEM_LIMIT):
    return pltpu.CompilerParams(dimension_semantics=semantics, vmem_limit_bytes=vmem_limit)


def _layer_norm(z, g, b):
    mu = jnp.mean(z, axis=-1, keepdims=True)
    zc = z - mu
    var = jnp.mean(zc * zc, axis=-1, keepdims=True)
    return zc * lax.rsqrt(var + LN_EPS) * g + b


def _matmul_kernel(w_is_transposed, x_ref, w_ref, s_ref, o_ref, wb_ref):
    @pl.when(pl.program_id(1) == 0)
    def _():
        w = w_ref[...]
        wb_ref[...] = (w.T if w_is_transposed else w).astype(BF16)

    acc = jnp.dot(x_ref[...], wb_ref[...], preferred_element_type=F32)
    o_ref[...] = (acc * s_ref[...]).astype(o_ref.dtype)


def _matmul(x, w_stack, layer, col0, n_cols, col_scale, out_dtype, w_is_transposed=False, tm=1024, tn=1024):
    m, k = x.shape
    cb0 = col0 // tn
    if w_is_transposed:
        w_spec = pl.BlockSpec((None, tn, k), lambda j, i: (layer, cb0 + j, 0))
    else:
        w_spec = pl.BlockSpec((None, k, tn), lambda j, i: (layer, 0, cb0 + j))
    return pl.pallas_call(
        functools.partial(_matmul_kernel, w_is_transposed),
        grid=(n_cols // tn, m // tm),
        in_specs=[pl.BlockSpec((tm, k), lambda j, i: (i, 0)),
                  w_spec,
                  pl.BlockSpec((1, tn), lambda j, i: (0, j))],
        out_specs=pl.BlockSpec((tm, tn), lambda j, i: (i, j)),
        out_shape=jax.ShapeDtypeStruct((m, n_cols), out_dtype),
        scratch_shapes=[pltpu.VMEM((k, tn), BF16)],
        compiler_params=_params(("parallel", "arbitrary")),
        name="proj_matmul",
    )(x, w_stack, col_scale)


def _out_proj_ln_kernel(alpha, n_in, *refs):
    ys = refs[:n_in]
    w_ref, x_ref, g_ref, b_ref, of_ref, ob_ref, wb_ref = refs[n_in:]

    @pl.when(pl.program_id(0) == 0)
    def _():
        wb_ref[...] = w_ref[...].astype(BF16)

    acc = alpha * x_ref[...]
    k0 = 0
    for y_ref in ys:
        kw = y_ref.shape[1]
        acc = acc + jnp.dot(y_ref[...], wb_ref[k0:k0 + kw, :], preferred_element_type=F32)
        k0 += kw
    out = _layer_norm(acc, g_ref[...], b_ref[...])
    of_ref[...] = out
    ob_ref[...] = out.astype(BF16)


def _out_proj_ln(ys, w_stack, layer, x, g, b, alpha, tm=512):
    m, d = x.shape
    n_in = len(ys)
    k = w_stack.shape[1]
    in_specs = ([pl.BlockSpec((tm, y.shape[1]), lambda i: (i, 0)) for y in ys]
                + [pl.BlockSpec((None, k, d), lambda i: (layer, 0, 0), pipeline_mode=pl.Buffered(1)),
                   pl.BlockSpec((tm, d), lambda i: (i, 0)),
                   pl.BlockSpec((1, d), lambda i: (0, 0)),
                   pl.BlockSpec((1, d), lambda i: (0, 0))])
    return pl.pallas_call(
        functools.partial(_out_proj_ln_kernel, alpha, n_in),
        grid=(m // tm,),
        in_specs=in_specs,
        out_specs=[pl.BlockSpec((tm, d), lambda i: (i, 0)),
                   pl.BlockSpec((tm, d), lambda i: (i, 0))],
        out_shape=[jax.ShapeDtypeStruct((m, d), F32), jax.ShapeDtypeStruct((m, d), BF16)],
        scratch_shapes=[pltpu.VMEM((k, d), BF16)],
        compiler_params=_params(("arbitrary",)),
        name="out_proj_ln",
    )(*ys, w_stack, x, g, b)


def _ffn_kernel(alpha, layer, tm, th, n_blocks, x_hbm, w1_hbm, w2_hbm, g_ref, b_ref, of_hbm, ob_hbm,
                x_buf, xb_ref, acc_ref, of_buf, ob_buf, w1_buf, w2_buf, sem_x, sem_w, sem_o):
    i = pl.program_id(0)
    last_tile = pl.num_programs(0) - 1
    chunks = FFN_DMA_CHUNKS

    def w_copies(j, slot):
        col = pl.multiple_of(j * th, th)
        r1 = w1_buf.shape[1] // chunks
        r2 = th // chunks
        copies = []
        for c in range(chunks):
            copies.append(pltpu.make_async_copy(
                w1_hbm.at[layer, pl.ds(c * r1, r1), pl.ds(col, th)],
                w1_buf.at[slot, pl.ds(c * r1, r1), :], sem_w.at[0, slot, c]))
            copies.append(pltpu.make_async_copy(
                w2_hbm.at[layer, pl.ds(col + c * r2, r2), :],
                w2_buf.at[slot, pl.ds(c * r2, r2), :], sem_w.at[1, slot, c]))
        return copies

    def x_copies(tile):
        rows = tm // chunks
        row0 = pl.multiple_of(tile * tm, tm)
        return [pltpu.make_async_copy(x_hbm.at[pl.ds(row0 + c * rows, rows), :],
                                      x_buf.at[pl.ds(c * rows, rows), :], sem_x.at[c]) for c in range(chunks)]

    def out_copies(tile):
        rows = tm // chunks
        row0 = pl.multiple_of(tile * tm, tm)
        copies = []
        for c in range(chunks):
            copies.append(pltpu.make_async_copy(of_buf.at[pl.ds(c * rows, rows), :],
                                                of_hbm.at[pl.ds(row0 + c * rows, rows), :], sem_o.at[0, c]))
            copies.append(pltpu.make_async_copy(ob_buf.at[pl.ds(c * rows, rows), :],
                                                ob_hbm.at[pl.ds(row0 + c * rows, rows), :], sem_o.at[1, c]))
        return copies

    def start(copies):
        for copy in copies:
            copy.start()

    def wait(copies):
        for copy in copies:
            copy.wait()

    @pl.when(i == 0)
    def _():
        start(x_copies(0))
        start(w_copies(0, 0))

    wait(x_copies(i))
    x = x_buf[...]
    xb_ref[...] = x.astype(BF16)
    acc_ref[...] = alpha * x

    @pl.when(i < last_tile)
    def _():
        start(x_copies(i + 1))

    def block(j, slot):
        wait(w_copies(j, slot))
        h = jnp.dot(xb_ref[...], w1_buf[slot].astype(BF16), preferred_element_type=F32)
        h = jnp.maximum(h, 0.0)
        h = (h * h).astype(BF16)
        acc_ref[...] += jnp.dot(h, w2_buf[slot].astype(BF16), preferred_element_type=F32)

    def pair(p, carry):
        j = 2 * p
        start(w_copies(j + 1, 1))
        block(j, 0)

        @pl.when(j + 2 < n_blocks)
        def _():
            start(w_copies(j + 2, 0))

        @pl.when(jnp.logical_and(j + 2 == n_blocks, i < last_tile))
        def _():
            start(w_copies(0, 0))

        block(j + 1, 1)
        return carry

    lax.fori_loop(0, n_blocks // 2, pair, 0)

    @pl.when(i > 0)
    def _():
        wait(out_copies(i - 1))

    out = _layer_norm(acc_ref[...], g_ref[...], b_ref[...])
    of_buf[...] = out
    ob_buf[...] = out.astype(BF16)
    start(out_copies(i))

    @pl.when(i == last_tile)
    def _():
        wait(out_copies(i))


def _ffn_ln(x, w1, w2, layer, g, b, alpha, tm=1024, th=512):
    m, d = x.shape
    f = w1.shape[2]
    n_blocks = f // th
    assert n_blocks % 2 == 0 and m % tm == 0
    hbm = pl.BlockSpec(memory_space=pl.ANY)
    return pl.pallas_call(
        functools.partial(_ffn_kernel, alpha, layer, tm, th, n_blocks),
        grid=(m // tm,),
        in_specs=[hbm, hbm, hbm,
                  pl.BlockSpec((1, d), lambda i: (0, 0)),
                  pl.BlockSpec((1, d), lambda i: (0, 0))],
        out_specs=[hbm, hbm],
        out_shape=[jax.ShapeDtypeStruct((m, d), F32), jax.ShapeDtypeStruct((m, d), BF16)],
        scratch_shapes=[pltpu.VMEM((tm, d), F32),
                        pltpu.VMEM((tm, d), BF16),
                        pltpu.VMEM((tm, d), F32),
                        pltpu.VMEM((tm, d), F32),
                        pltpu.VMEM((tm, d), BF16),
                        pltpu.VMEM((2, d, th), F32),
                        pltpu.VMEM((2, th, d), F32),
                        pltpu.SemaphoreType.DMA((FFN_DMA_CHUNKS,)),
                        pltpu.SemaphoreType.DMA((2, 2, FFN_DMA_CHUNKS)),
                        pltpu.SemaphoreType.DMA((2, FFN_DMA_CHUNKS))],
        compiler_params=_params(("arbitrary",), V7X_VMEM_LIMIT_FFN),
        name="ffn_ln",
    )(x, w1, w2, g, b)


def _kv_rows(dil):
    return dil * BAND_BLOCK + ATT_TILE


def _attn_kernel(q_ref, k_ref, v_ref, slope_ref, out_ref,
                 g4_ref, qs_ref, ks_ref, vs_ref, o_ref, l_ref, bias_ref, s_ref, e_ref):
    tile = pl.program_id(2)
    p = BAND_BLOCK
    t_len = ATT_TILE
    qi = lax.broadcasted_iota(jnp.int32, (p, 2 * p), 0)
    kj = lax.broadcasted_iota(jnp.int32, (p, 2 * p), 1)
    dist = p + qi - kj
    valid = jnp.logical_and(dist >= 0, dist <= p)
    in_prev_block = kj < p
    slope = slope_ref[0]
    for bi, (_, dil) in enumerate(DILATED_BRANCHES):
        bias_ref[bi] = jnp.where(valid, -(slope * (dist * dil).astype(F32)), -jnp.inf)

    def rows(start, size, dil):
        return pl.ds(start, size, stride=dil) if dil > 1 else pl.ds(start, size)

    kv_base = [sum(_kv_rows(d) for _, d in DILATED_BRANCHES[:bi]) for bi in range(len(DILATED_BRANCHES))]
    srcs = (q_ref, k_ref, v_ref)
    dil4 = DILATED_BRANCHES[1][1]
    w4 = t_len // dil4
    for x, src in enumerate(srcs):
        for r in range(dil4):
            g4_ref[x, r * w4:(r + 1) * w4, :] = src[rows(r, w4, dil4), :]

    def residue_rows(x, r, dil):
        if dil == 1:
            return srcs[x][...]
        if dil == dil4:
            return g4_ref[x, r * w4:(r + 1) * w4, :]
        return g4_ref[x, rows((r % dil4) * w4 + r // dil4, t_len // dil, dil // dil4), :]

    nt = (((1,), (1,)), ((), ()))
    for bi, (_, dil) in enumerate(DILATED_BRANCHES):
        wq = t_len // dil
        nq = wq // p
        wk = p + wq
        base = kv_base[bi]

        @pl.when(tile == 0)
        def _(dil=dil, wk=wk, base=base):
            for r in range(dil):
                for dst_ref in (ks_ref, vs_ref):
                    dst_ref[base + r * wk:base + r * wk + p, :] = jnp.zeros((p, A_HEAD_DIM), BF16)

        @pl.when(tile > 0)
        def _(dil=dil, wk=wk, wq=wq, base=base):
            for r in range(dil):
                for dst_ref in (ks_ref, vs_ref):
                    dst_ref[base + r * wk:base + r * wk + p, :] = dst_ref[base + r * wk + wq:base + (r + 1) * wk, :]

        for r in range(dil):
            qs_ref[r * wq:(r + 1) * wq, :] = residue_rows(0, r, dil).astype(BF16)
            ks_ref[base + r * wk + p:base + (r + 1) * wk, :] = residue_rows(1, r, dil).astype(BF16)
            vs_ref[base + r * wk + p:base + (r + 1) * wk, :] = residue_rows(2, r, dil).astype(BF16)

        def group(g, carry, dil=dil, nq=nq, wk=wk, bi=bi, base=base):
            starts = []
            for u in range(ATT_UNROLL):
                idx = g * ATT_UNROLL + u
                r = idx // nq
                n = idx - r * nq
                q0 = pl.multiple_of(idx * p, p)
                k0 = pl.multiple_of(base + r * wk + n * p, p)
                s = lax.dot_general(qs_ref[pl.ds(q0, p), :], ks_ref[pl.ds(k0, 2 * p), :], nt,
                                    preferred_element_type=F32) + bias_ref[bi]
                no_history = jnp.logical_and(tile == 0, n == 0)
                s_ref[u * p:(u + 1) * p, :] = jnp.where(jnp.logical_and(no_history, in_prev_block), -jnp.inf, s)
                starts.append((k0, r + dil * p * n))
            s = s_ref[...]
            m = jnp.max(s, axis=-1, keepdims=True)
            e = jnp.exp(s - m)
            l = jnp.sum(e, axis=-1, keepdims=True)
            e_ref[...] = e.astype(BF16)
            inv_l = 1.0 / l
            lse = m + jnp.log(l)
            for u, (k0, t0) in enumerate(starts):
                o = jnp.dot(e_ref[u * p:(u + 1) * p, :], vs_ref[pl.ds(k0, 2 * p), :], preferred_element_type=F32)
                o_ref[bi, rows(t0, p, dil), :] = o * inv_l[u * p:(u + 1) * p]
                l_ref[bi, rows(t0, p, dil), :] = jnp.broadcast_to(lse[u * p:(u + 1) * p], (p, A_HEAD_DIM))
            return carry

        lax.fori_loop(0, (dil * nq) // ATT_UNROLL, group, 0)

    chunk = 256
    for c0 in range(0, t_len, chunk):
        sl = slice(c0, c0 + chunk)
        l1, l2, l3 = l_ref[0, sl, :], l_ref[1, sl, :], l_ref[2, sl, :]
        mx = jnp.maximum(jnp.maximum(l1, l2), l3)
        e1, e2, e3 = jnp.exp(l1 - mx), jnp.exp(l2 - mx), jnp.exp(l3 - mx)
        mix = (e1 * o_ref[0, sl, :] + e2 * o_ref[1, sl, :] + e3 * o_ref[2, sl, :]) / (e1 + e2 + e3)
        out_ref[sl, :] = mix.astype(out_ref.dtype)


def _dilated_mixture_attention(proj, bsz, seq, n_heads):
    m = proj.shape[0]
    hd = A_HEAD_DIM
    t_len = ATT_TILE
    nt = seq // t_len
    slopes = 2.0 ** (-8.0 * jnp.arange(1, n_heads + 1, dtype=F32) / n_heads)
    slopes = jnp.broadcast_to(slopes[:, None, None], (n_heads, 1, 2 * BAND_BLOCK))

    kv_rows = sum(_kv_rows(dil) for _, dil in DILATED_BRANCHES)

    def head_block(col):
        return pl.BlockSpec((t_len, hd), lambda b, h, t: (b * nt + t, col * n_heads + h))

    return pl.pallas_call(
        _attn_kernel,
        grid=(bsz, n_heads, nt),
        in_specs=[head_block(0), head_block(1), head_block(2),
                  pl.BlockSpec((1, 1, 2 * BAND_BLOCK), lambda b, h, t: (h, 0, 0))],
        out_specs=pl.BlockSpec((t_len, hd), lambda b, h, t: (b * nt + t, h)),
        out_shape=jax.ShapeDtypeStruct((m, n_heads * hd), BF16),
        scratch_shapes=[pltpu.VMEM((3, t_len, hd), F32),
                        pltpu.VMEM((t_len, hd), BF16),
                        pltpu.VMEM((kv_rows, hd), BF16),
                        pltpu.VMEM((kv_rows, hd), BF16),
                        pltpu.VMEM((3, t_len, hd), F32),
                        pltpu.VMEM((3, t_len, hd), F32),
                        pltpu.VMEM((3, BAND_BLOCK, 2 * BAND_BLOCK), F32),
                        pltpu.VMEM((ATT_UNROLL * BAND_BLOCK, 2 * BAND_BLOCK), F32),
                        pltpu.VMEM((ATT_UNROLL * BAND_BLOCK, 2 * BAND_BLOCK), BF16)],
        compiler_params=_params(("parallel", "parallel", "arbitrary")),
        name="dilated_attention",
    )(proj, proj, proj, slopes)


def _conv_tile(has_history, a_ref, g_ref, ha_ref, hg_ref, w_ref, cb_ref, lg_ref, lb_ref, o_ref, h_ref, hs_ref, c_ref,
               n_parts, side_work):
    ts, ch = a_ref.shape
    hist = ha_ref[...] * jax.nn.sigmoid(hg_ref[...])
    h_ref[0:CONV_HALO, :] = jnp.where(has_history, hist, 0.0)
    h_ref[CONV_HALO:, :] = a_ref[...] * jax.nn.sigmoid(g_ref[...])
    first = CONV_HALO - (CONV_WIDTH - 1)
    h_rows = h_ref.shape[0]
    for s in range(1, SUBLANES):
        hs_ref[s - 1, SUBLANES - s:SUBLANES - s + h_rows, :] = h_ref[...]
    rows, lanes = 32, 512
    groups = rows // SUBLANES
    part_rows = ts // n_parts

    def part(p, carry):
        base = pl.multiple_of(p * part_rows, part_rows)
        for r0 in range(0, part_rows, rows):
            for c0 in range(0, ch, lanes):
                acc = jnp.broadcast_to(cb_ref[:, c0:c0 + lanes], (groups, SUBLANES, lanes))
                for j in range(CONV_WIDTH):
                    a8, s = (first + j) // SUBLANES * SUBLANES, (first + j) % SUBLANES
                    if s == 0:
                        tap = h_ref[pl.ds(base + r0 + a8, rows), c0:c0 + lanes]
                    else:
                        tap = hs_ref[s - 1, pl.ds(base + SUBLANES + r0 + a8, rows), c0:c0 + lanes]
                    acc = acc + w_ref[j, :, c0:c0 + lanes] * tap.reshape(groups, SUBLANES, lanes)
                c_ref[pl.ds(base + r0, rows), c0:c0 + lanes] = acc.reshape(rows, lanes)
        side_work(p)
        return carry

    lax.fori_loop(0, n_parts, part, 0)
    y = _layer_norm(c_ref[...], lg_ref[...], lb_ref[...])
    o_ref[...] = (y * jax.nn.sigmoid(y)).astype(o_ref.dtype)


def _qkv_conv_kernel(n_row_tiles, tiles_per_seq, x_ref, w_ref, s_ref, a_ref, g_ref, ha_ref, hg_ref, cw_ref, cb_ref,
                     lg_ref, lb_ref, o_ref, conv_ref, wb_ref, h_ref, hs_ref, c_ref):
    j, i = pl.program_id(0), pl.program_id(1)

    @pl.when(i == 0)
    def _():
        wb_ref[...] = w_ref[...].astype(BF16)

    conv_tile = j * n_row_tiles + i
    rm = x_ref.shape[0] // QKV_ROW_CHUNKS

    def matmul_chunk(c):
        r0 = pl.multiple_of(c * rm, rm)
        acc = jnp.dot(x_ref[pl.ds(r0, rm), :], wb_ref[...], preferred_element_type=F32)
        o_ref[pl.ds(r0, rm), :] = acc * s_ref[...]

    _conv_tile(conv_tile % tiles_per_seq != 0, a_ref, g_ref, ha_ref, hg_ref, cw_ref, cb_ref, lg_ref, lb_ref,
               conv_ref, h_ref, hs_ref, c_ref, QKV_ROW_CHUNKS, matmul_chunk)


def _qkv_proj_conv(x, w_stack, layer, n_cols, col_scale, u, seq, conv_w, conv_b, ln_g, ln_b, tm=1024, tn=768):
    m, k = x.shape
    ch = conv_w.shape[1]
    n_col_blocks, n_row_tiles = n_cols // tn, m // tm
    ts = m // (n_col_blocks * n_row_tiles)
    assert n_cols % tn == 0 and m % tm == 0 and ts % CONV_HALO == 0 and seq % ts == 0
    per = ts // CONV_HALO
    w_rows = jnp.broadcast_to(conv_w[:, None, :], (CONV_WIDTH, SUBLANES, ch))

    def tile(col):
        return pl.BlockSpec((ts, ch), lambda j, i: (j * n_row_tiles + i, col))

    def halo(col):
        return pl.BlockSpec((CONV_HALO, ch), lambda j, i: (jnp.maximum((j * n_row_tiles + i) * per - 1, 0), col))

    def const(shape):
        return pl.BlockSpec(shape, lambda j, i: (0,) * len(shape))

    return pl.pallas_call(
        functools.partial(_qkv_conv_kernel, n_row_tiles, seq // ts),
        grid=(n_col_blocks, n_row_tiles),
        in_specs=[pl.BlockSpec((tm, k), lambda j, i: (i, 0)),
                  pl.BlockSpec((None, k, tn), lambda j, i: (layer, 0, j)),
                  pl.BlockSpec((1, tn), lambda j, i: (0, j)),
                  tile(0), tile(1), halo(0), halo(1),
                  const((CONV_WIDTH, SUBLANES, ch)), const((1, ch)), const((1, ch)), const((1, ch))],
        out_specs=[pl.BlockSpec((tm, tn), lambda j, i: (i, j)),
                   pl.BlockSpec((ts, ch), lambda j, i: (j * n_row_tiles + i, 0))],
        out_shape=[jax.ShapeDtypeStruct((m, n_cols), F32), jax.ShapeDtypeStruct((m, ch), BF16)],
        scratch_shapes=[pltpu.VMEM((k, tn), BF16),
                        pltpu.VMEM((ts + CONV_HALO, ch), F32),
                        pltpu.VMEM((SUBLANES - 1, ts + CONV_HALO + SUBLANES, ch), F32),
                        pltpu.VMEM((ts, ch), F32)],
        compiler_params=_params(("parallel", "arbitrary")),
        name="qkv_proj_conv",
    )(x, w_stack, col_scale, u, u, u, u, w_rows, conv_b.reshape(1, ch), ln_g.reshape(1, ch), ln_b.reshape(1, ch))


def _gates_kernel(x_ref, w_ref, bias_ref, gc_ref, gr_ref):
    length = x_ref.shape[0]
    w = w_ref[...].astype(BF16)
    w = jnp.concatenate([w, jnp.zeros((GATE_LANES - w.shape[0], w.shape[1]), BF16)], axis=0)
    pre = lax.dot_general(x_ref[...], w, (((1,), (1,)), ((), ())), preferred_element_type=F32) + bias_ref[...]
    log_f = jnp.minimum(pre, 0.0) - jnp.log1p(jnp.exp(-jnp.abs(pre)))
    ti = lax.broadcasted_iota(jnp.int32, (length, length), 0)
    si = lax.broadcasted_iota(jnp.int32, (length, length), 1)
    tril = (ti >= si).astype(F32)
    cum_f = jnp.dot(tril, log_f, preferred_element_type=F32, precision=lax.Precision.HIGHEST)
    lane = lax.broadcasted_iota(jnp.int32, pre.shape, 1)
    gates = jnp.where(lane < C_HEADS, pre, cum_f)
    gc_ref[...] = gates
    gr_ref[...] = gates.T


def _mlstm_gates(x, w_stack_t, layer, gate_row0, bias):
    m, d = x.shape
    length = MLSTM_CHUNK
    gate_rows = 2 * C_HEADS
    gate_block = gate_row0 // gate_rows
    return pl.pallas_call(
        _gates_kernel,
        grid=(m // length,),
        in_specs=[pl.BlockSpec((length, d), lambda i: (i, 0)),
                  pl.BlockSpec((None, gate_rows, d), lambda i: (layer, gate_block, 0)),
                  pl.BlockSpec((1, GATE_LANES), lambda i: (0, 0))],
        out_specs=[pl.BlockSpec((length, GATE_LANES), lambda i: (i, 0)),
                   pl.BlockSpec((GATE_LANES, length), lambda i: (0, i))],
        out_shape=[jax.ShapeDtypeStruct((m, GATE_LANES), F32),
                   jax.ShapeDtypeStruct((GATE_LANES, m), F32)],
        compiler_params=_params(("parallel",)),
        name="mlstm_gates",
    )(x, w_stack_t, bias)


def _mlstm_kernel(dk, dv, q_ref, k_ref, v_ref, o_ref, gc_ref, gr_ref, ng_ref, y_ref, c_ref, n_ref, m_ref):
    length = q_ref.shape[0]

    @pl.when(pl.program_id(1) == 0)
    def _():
        c_ref[...] = jnp.zeros_like(c_ref)
        n_ref[...] = jnp.zeros_like(n_ref)
        m_ref[...] = jnp.zeros_like(m_ref)

    ti = lax.broadcasted_iota(jnp.int32, (length, length), 0)
    si = lax.broadcasted_iota(jnp.int32, (length, length), 1)
    causal = ti >= si
    for h in range(C_HEADS):
        q = q_ref[:, h * dk:(h + 1) * dk]
        k = k_ref[:, h * dk:(h + 1) * dk]
        v = v_ref[:, h * dv:(h + 1) * dv]
        i_row = gr_ref[h:h + 1, :]
        b_row = gr_ref[C_HEADS + h:C_HEADS + h + 1, :]
        i_col = gc_ref[:, h:h + 1]
        b_col = gc_ref[:, C_HEADS + h:C_HEADS + h + 1]
        m_prev = m_ref[h:h + 1, 0:1]
        c_prev = c_ref[h]
        n_prev = n_ref[h:h + 1, :]

        dmat = jnp.where(causal, b_col - b_row + i_row, -jnp.inf)
        inter = b_col + m_prev
        m_t = jnp.maximum(inter, jnp.max(dmat, axis=-1, keepdims=True))
        w = jnp.exp(dmat - m_t)
        a = jnp.exp(inter - m_t)
        qk = lax.dot_general(q, k, (((1,), (1,)), ((), ())), preferred_element_type=F32)
        sqk = qk * w
        num = (a * jnp.dot(q, c_prev.astype(BF16), preferred_element_type=F32)
               + jnp.dot(sqk.astype(BF16), v, preferred_element_type=F32))
        den = (a * jnp.sum(q.astype(F32) * n_prev, axis=-1, keepdims=True)
               + jnp.sum(sqk, axis=-1, keepdims=True))
        hh = num / jnp.maximum(jnp.abs(den), jnp.exp(-m_t))

        b_last = b_col[length - 1:length, :]
        g = b_last - b_col + i_col
        m_new = jnp.maximum(b_last + m_prev, jnp.max(g, axis=0, keepdims=True))
        decay = jnp.exp(b_last + m_prev - m_new)
        kw = k.astype(F32) * jnp.exp(g - m_new)
        c_ref[h] = decay * c_prev + lax.dot_general(kw.astype(BF16), v, (((0,), (0,)), ((), ())),
                                                    preferred_element_type=F32)
        n_ref[h:h + 1, :] = decay * n_prev + jnp.sum(kw, axis=0, keepdims=True)
        m_ref[h:h + 1, :] = jnp.broadcast_to(m_new, (1, m_ref.shape[1]))

        mu = jnp.mean(hh, axis=-1, keepdims=True)
        hc = hh - mu
        var = jnp.mean(hc * hc, axis=-1, keepdims=True)
        hn = hc * lax.rsqrt(var + LN_EPS) * ng_ref[:, h * dv:(h + 1) * dv]
        y_ref[:, h * dv:(h + 1) * dv] = (jax.nn.sigmoid(o_ref[:, h * dv:(h + 1) * dv]) * hn).astype(y_ref.dtype)


def _mlstm(qkv, o, gates_col, gates_row, norm_g, bsz, seq, dk, dv):
    length = MLSTM_CHUNK
    nc = seq // length
    qk_w = C_HEADS * dk
    v_w = C_HEADS * dv
    v_blk = (2 * qk_w) // v_w
    return pl.pallas_call(
        functools.partial(_mlstm_kernel, dk, dv),
        grid=(bsz, nc),
        in_specs=[pl.BlockSpec((length, qk_w), lambda b, c: (b * nc + c, 0)),
                  pl.BlockSpec((length, qk_w), lambda b, c: (b * nc + c, 1)),
                  pl.BlockSpec((length, v_w), lambda b, c: (b * nc + c, v_blk)),
                  pl.BlockSpec((length, v_w), lambda b, c: (b * nc + c, 0)),
                  pl.BlockSpec((length, GATE_LANES), lambda b, c: (b * nc + c, 0)),
                  pl.BlockSpec((GATE_LANES, length), lambda b, c: (0, b * nc + c)),
                  pl.BlockSpec((1, v_w), lambda b, c: (0, 0))],
        out_specs=pl.BlockSpec((length, v_w), lambda b, c: (b * nc + c, 0)),
        out_shape=jax.ShapeDtypeStruct((bsz * seq, v_w), BF16),
        scratch_shapes=[pltpu.VMEM((C_HEADS, dk, dv), F32),
                        pltpu.VMEM((8, dk), F32),
                        pltpu.VMEM((8, GATE_LANES), F32)],
        compiler_params=_params(("parallel", "arbitrary")),
        name="mlstm_chunk",
    )(qkv, qkv, qkv, o, gates_col, gates_row, norm_g)


def _even_mixer(xb, x, w_in, j, conv_w, conv_b, conv_ln_g, conv_ln_b, w_out, ln_g, ln_b, alpha, bsz, seq):
    d = x.shape[1]
    a_width = d // 2
    n_heads = a_width // A_HEAD_DIM
    in_width = w_in.shape[2]
    qkv_w = 3 * a_width
    qkv_scale = jnp.ones((1, qkv_w), F32).at[:, :a_width].set(A_HEAD_DIM ** -0.5)
    u = _matmul(xb, w_in, j, qkv_w, in_width - qkv_w, jnp.ones((1, in_width - qkv_w), F32), F32)
    qkv, conv = _qkv_proj_conv(xb, w_in, j, qkv_w, qkv_scale, u, seq, conv_w, conv_b, conv_ln_g, conv_ln_b)
    att = _dilated_mixture_attention(qkv, bsz, seq, n_heads)
    return _out_proj_ln([att, conv], w_out, j, x, ln_g.reshape(1, d), ln_b.reshape(1, d), alpha)


def _odd_mixer(xb, x, w_in_t, j, igate_b, fgate_b, norm_g, w_out, ln_g, ln_b, alpha, bsz, seq):
    d = x.shape[1]
    v_w = d
    dv = v_w // C_HEADS
    dk = dv // 2
    qk_w = C_HEADS * dk
    qkv_w = 2 * qk_w + v_w
    qkv_scale = jnp.ones((1, qkv_w), F32).at[:, :qk_w].set(dk ** -0.5)
    qkv = _matmul(xb, w_in_t, j, 0, qkv_w, qkv_scale, BF16, w_is_transposed=True)
    o = _matmul(xb, w_in_t, j, qkv_w, v_w, jnp.ones((1, v_w), F32), F32, w_is_transposed=True)
    bias = jnp.zeros((1, GATE_LANES), F32).at[0, :C_HEADS].set(igate_b).at[0, C_HEADS:2 * C_HEADS].set(fgate_b)
    gates_col, gates_row = _mlstm_gates(xb, w_in_t, j, qkv_w + v_w, bias)
    y = _mlstm(qkv, o, gates_col, gates_row, norm_g.reshape(1, v_w), bsz, seq, dk, dv)
    return _out_proj_ln([y], w_out, j, x, ln_g.reshape(1, d), ln_b.reshape(1, d), alpha)


def kernel(x, even_w_in, even_conv_w, even_conv_b, even_conv_ln_g, even_conv_ln_b, even_w_out,
           odd_w_in, odd_igate_b, odd_fgate_b, odd_norm_g, odd_w_out, mix_ln_g, mix_ln_b,
           ffn_w1, ffn_w2, ffn_ln_g, ffn_ln_b):
    bsz, seq, d = x.shape
    depth = mix_ln_g.shape[0]
    alpha = (2 * depth) ** 0.25
    xf = x.reshape(bsz * seq, d)
    xb = xf.astype(BF16)
    odd_w_in_t = jnp.swapaxes(odd_w_in, 1, 2)
    for layer in range(depth):
        j = layer // 2
        if layer % 2 == 0:
            xf, xb = _even_mixer(xb, xf, even_w_in, j, even_conv_w[j], even_conv_b[j], even_conv_ln_g[j],
                                 even_conv_ln_b[j], even_w_out, mix_ln_g[layer], mix_ln_b[layer],
                                 alpha, bsz, seq)
        else:
            xf, xb = _odd_mixer(xb, xf, odd_w_in_t, j, odd_igate_b[j], odd_fgate_b[j], odd_norm_g[j],
                                odd_w_out, mix_ln_g[layer], mix_ln_b[layer], alpha, bsz, seq)
        xf, xb = _ffn_ln(xf, ffn_w1, ffn_w2, layer,
                         ffn_ln_g[layer].reshape(1, d), ffn_ln_b[layer].reshape(1, d), alpha)
    return xf.reshape(bsz, seq, d)
```

```python
import functools

import jax
import jax.numpy as jnp
from jax import lax
from jax.experimental import pallas as pl
from jax.experimental.pallas import tpu as pltpu

LN_EPS = 1e-5
DILATED_BRANCHES = ((128, 1), (512, 4), (2048, 16))
BAND_BLOCK = 128
A_HEAD_DIM = 128
ATT_TILE = 2048
ATT_UNROLL = 16
CONV_WIDTH = 31
CONV_HALO = 32
C_HEADS = 4
MLSTM_CHUNK = 256
GATE_LANES = 128
SUBLANES = 8
FFN_DMA_CHUNKS = 4
QKV_ROW_CHUNKS = 2
V7X_VMEM_LIMIT = 56 * 1024 * 1024
V7X_VMEM_LIMIT_FFN = 58 * 1024 * 1024

BF16 = jnp.bfloat16
F32 = jnp.float32


def _params(semantics, vmem_limit=V7X_VMEM_LIMIT):
    return pltpu.CompilerParams(dimension_semantics=semantics, vmem_limit_bytes=vmem_limit)


def _layer_norm(z, g, b):
    mu = jnp.mean(z, axis=-1, keepdims=True)
    zc = z - mu
    var = jnp.mean(zc * zc, axis=-1, keepdims=True)
    return zc * lax.rsqrt(var + LN_EPS) * g + b


def _matmul_kernel(w_is_transposed, emit_bf16_x, x_ref, w_ref, s_ref, o_ref, *rest):
    wb_ref = rest[-1]

    def convert_weights():
        w = w_ref[...]
        wb_ref[...] = (w.T if w_is_transposed else w).astype(BF16)

    if emit_bf16_x:
        convert_weights()
        x = x_ref[...].astype(BF16)
        rest[0][...] = x
    else:
        pl.when(pl.program_id(1) == 0)(convert_weights)
        x = x_ref[...]
    acc = jnp.dot(x, wb_ref[...], preferred_element_type=F32)
    o_ref[...] = (acc * s_ref[...]).astype(o_ref.dtype)


def _matmul(x, w_stack, layer, col0, n_cols, col_scale, out_dtype, w_is_transposed=False, emit_bf16_x=False,
            tm=1024, tn=1024):
    m, k = x.shape
    cb0 = col0 // tn
    n_col_blocks, n_row_tiles = n_cols // tn, m // tm
    if emit_bf16_x:
        grid = (n_row_tiles, n_col_blocks)

        def at(f):
            return lambda i, j: f(i, j)
    else:
        grid = (n_col_blocks, n_row_tiles)

        def at(f):
            return lambda j, i: f(i, j)

    if w_is_transposed:
        w_spec = pl.BlockSpec((None, tn, k), at(lambda i, j: (layer, cb0 + j, 0)))
    else:
        w_spec = pl.BlockSpec((None, k, tn), at(lambda i, j: (layer, 0, cb0 + j)))
    out_specs = [pl.BlockSpec((tm, tn), at(lambda i, j: (i, j)))]
    out_shape = [jax.ShapeDtypeStruct((m, n_cols), out_dtype)]
    if emit_bf16_x:
        out_specs.append(pl.BlockSpec((tm, k), at(lambda i, j: (i, 0))))
        out_shape.append(jax.ShapeDtypeStruct((m, k), BF16))
    results = pl.pallas_call(
        functools.partial(_matmul_kernel, w_is_transposed, emit_bf16_x),
        grid=grid,
        in_specs=[pl.BlockSpec((tm, k), at(lambda i, j: (i, 0))),
                  w_spec,
                  pl.BlockSpec((1, tn), at(lambda i, j: (0, j)))],
        out_specs=out_specs,
        out_shape=out_shape,
        scratch_shapes=[pltpu.VMEM((k, tn), BF16)],
        compiler_params=_params(("parallel", "arbitrary")),
        name="proj_matmul",
    )(x, w_stack, col_scale)
    return results if emit_bf16_x else results[0]


def _out_proj_ln_kernel(alpha, n_in, *refs):
    ys = refs[:n_in]
    w_ref, x_ref, g_ref, b_ref, of_ref, ob_ref, wb_ref = refs[n_in:]

    @pl.when(pl.program_id(0) == 0)
    def _():
        wb_ref[...] = w_ref[...].astype(BF16)

    acc = alpha * x_ref[...]
    k0 = 0
    for y_ref in ys:
        kw = y_ref.shape[1]
        acc = acc + jnp.dot(y_ref[...], wb_ref[k0:k0 + kw, :], preferred_element_type=F32)
        k0 += kw
    out = _layer_norm(acc, g_ref[...], b_ref[...])
    of_ref[...] = out
    ob_ref[...] = out.astype(BF16)


def _out_proj_ln(ys, w_stack, layer, x, g, b, alpha, tm=512):
    m, d = x.shape
    n_in = len(ys)
    k = w_stack.shape[1]
    in_specs = ([pl.BlockSpec((tm, y.shape[1]), lambda i: (i, 0)) for y in ys]
                + [pl.BlockSpec((None, k, d), lambda i: (layer, 0, 0), pipeline_mode=pl.Buffered(1)),
                   pl.BlockSpec((tm, d), lambda i: (i, 0)),
                   pl.BlockSpec((1, d), lambda i: (0, 0)),
                   pl.BlockSpec((1, d), lambda i: (0, 0))])
    return pl.pallas_call(
        functools.partial(_out_proj_ln_kernel, alpha, n_in),
        grid=(m // tm,),
        in_specs=in_specs,
        out_specs=[pl.BlockSpec((tm, d), lambda i: (i, 0)),
                   pl.BlockSpec((tm, d), lambda i: (i, 0))],
        out_shape=[jax.ShapeDtypeStruct((m, d), F32), jax.ShapeDtypeStruct((m, d), BF16)],
        scratch_shapes=[pltpu.VMEM((k, d), BF16)],
        compiler_params=_params(("arbitrary",)),
        name="out_proj_ln",
    )(*ys, w_stack, x, g, b)


def _ffn_kernel(alpha, layer, tm, th, n_blocks, x_hbm, w1_hbm, w2_hbm, g_ref, b_ref, of_hbm, ob_hbm,
                x_buf, xb_ref, acc_ref, of_buf, ob_buf, w1_buf, w2_buf, sem_x, sem_w, sem_o):
    i = pl.program_id(0)
    last_tile = pl.num_programs(0) - 1
    chunks = FFN_DMA_CHUNKS

    def w_copies(j, slot):
        col = pl.multiple_of(j * th, th)
        r1 = w1_buf.shape[1] // chunks
        r2 = th // chunks
        copies = []
        for c in range(chunks):
            copies.append(pltpu.make_async_copy(
                w1_hbm.at[layer, pl.ds(c * r1, r1), pl.ds(col, th)],
                w1_buf.at[slot, pl.ds(c * r1, r1), :], sem_w.at[0, slot, c]))
            copies.append(pltpu.make_async_copy(
                w2_hbm.at[layer, pl.ds(col + c * r2, r2), :],
                w2_buf.at[slot, pl.ds(c * r2, r2), :], sem_w.at[1, slot, c]))
        return copies

    def x_copies(tile):
        rows = tm // chunks
        row0 = pl.multiple_of(tile * tm, tm)
        return [pltpu.make_async_copy(x_hbm.at[pl.ds(row0 + c * rows, rows), :],
                                      x_buf.at[pl.ds(c * rows, rows), :], sem_x.at[c]) for c in range(chunks)]

    def out_copies(tile):
        rows = tm // chunks
        row0 = pl.multiple_of(tile * tm, tm)
        copies = []
        for c in range(chunks):
            copies.append(pltpu.make_async_copy(of_buf.at[pl.ds(c * rows, rows), :],
                                                of_hbm.at[pl.ds(row0 + c * rows, rows), :], sem_o.at[0, c]))
            copies.append(pltpu.make_async_copy(ob_buf.at[pl.ds(c * rows, rows), :],
                                                ob_hbm.at[pl.ds(row0 + c * rows, rows), :], sem_o.at[1, c]))
        return copies

    def start(copies):
        for copy in copies:
            copy.start()

    def wait(copies):
        for copy in copies:
            copy.wait()

    @pl.when(i == 0)
    def _():
        start(x_copies(0))
        start(w_copies(0, 0))

    wait(x_copies(i))
    x = x_buf[...]
    xb_ref[...] = x.astype(BF16)
    acc_ref[...] = alpha * x

    @pl.when(i < last_tile)
    def _():
        start(x_copies(i + 1))

    def block(j, slot):
        wait(w_copies(j, slot))
        h = jnp.dot(xb_ref[...], w1_buf[slot].astype(BF16), preferred_element_type=F32)
        h = jnp.maximum(h, 0.0)
        h = (h * h).astype(BF16)
        acc_ref[...] += jnp.dot(h, w2_buf[slot].astype(BF16), preferred_element_type=F32)

    def pair(p, carry):
        j = 2 * p
        start(w_copies(j + 1, 1))
        block(j, 0)

        @pl.when(j + 2 < n_blocks)
        def _():
            start(w_copies(j + 2, 0))

        @pl.when(jnp.logical_and(j + 2 == n_blocks, i < last_tile))
        def _():
            start(w_copies(0, 0))

        block(j + 1, 1)
        return carry

    lax.fori_loop(0, n_blocks // 2, pair, 0)

    @pl.when(i > 0)
    def _():
        wait(out_copies(i - 1))

    out = _layer_norm(acc_ref[...], g_ref[...], b_ref[...])
    of_buf[...] = out
    ob_buf[...] = out.astype(BF16)
    start(out_copies(i))

    @pl.when(i == last_tile)
    def _():
        wait(out_copies(i))


def _ffn_ln(x, w1, w2, layer, g, b, alpha, tm=1024, th=512):
    m, d = x.shape
    f = w1.shape[2]
    n_blocks = f // th
    assert n_blocks % 2 == 0 and m % tm == 0
    hbm = pl.BlockSpec(memory_space=pl.ANY)
    return pl.pallas_call(
        functools.partial(_ffn_kernel, alpha, layer, tm, th, n_blocks),
        grid=(m // tm,),
        in_specs=[hbm, hbm, hbm,
                  pl.BlockSpec((1, d), lambda i: (0, 0)),
                  pl.BlockSpec((1, d), lambda i: (0, 0))],
        out_specs=[hbm, hbm],
        out_shape=[jax.ShapeDtypeStruct((m, d), F32), jax.ShapeDtypeStruct((m, d), BF16)],
        scratch_shapes=[pltpu.VMEM((tm, d), F32),
                        pltpu.VMEM((tm, d), BF16),
                        pltpu.VMEM((tm, d), F32),
                        pltpu.VMEM((tm, d), F32),
                        pltpu.VMEM((tm, d), BF16),
                        pltpu.VMEM((2, d, th), F32),
                        pltpu.VMEM((2, th, d), F32),
                        pltpu.SemaphoreType.DMA((FFN_DMA_CHUNKS,)),
                        pltpu.SemaphoreType.DMA((2, 2, FFN_DMA_CHUNKS)),
                        pltpu.SemaphoreType.DMA((2, FFN_DMA_CHUNKS))],
        compiler_params=_params(("arbitrary",), V7X_VMEM_LIMIT_FFN),
        name="ffn_ln",
    )(x, w1, w2, g, b)


def _kv_rows(dil):
    return dil * BAND_BLOCK + ATT_TILE


def _attn_kernel(q_ref, k_ref, v_ref, slope_ref, out_ref,
                 g4_ref, qs_ref, ks_ref, vs_ref, o_ref, l_ref, bias_ref, s_ref, e_ref):
    tile = pl.program_id(2)
    p = BAND_BLOCK
    t_len = ATT_TILE
    qi = lax.broadcasted_iota(jnp.int32, (p, 2 * p), 0)
    kj = lax.broadcasted_iota(jnp.int32, (p, 2 * p), 1)
    dist = p + qi - kj
    valid = jnp.logical_and(dist >= 0, dist <= p)
    in_prev_block = kj < p
    slope = slope_ref[0]
    for bi, (_, dil) in enumerate(DILATED_BRANCHES):
        bias_ref[bi] = jnp.where(valid, -(slope * (dist * dil).astype(F32)), -jnp.inf)

    def rows(start, size, dil):
        return pl.ds(start, size, stride=dil) if dil > 1 else pl.ds(start, size)

    kv_base = [sum(_kv_rows(d) for _, d in DILATED_BRANCHES[:bi]) for bi in range(len(DILATED_BRANCHES))]
    srcs = (q_ref, k_ref, v_ref)
    dil4 = DILATED_BRANCHES[1][1]
    w4 = t_len // dil4
    for x, src in enumerate(srcs):
        for r in range(dil4):
            g4_ref[x, r * w4:(r + 1) * w4, :] = src[rows(r, w4, dil4), :]

    def residue_rows(x, r, dil):
        if dil == 1:
            return srcs[x][...]
        if dil == dil4:
            return g4_ref[x, r * w4:(r + 1) * w4, :]
        return g4_ref[x, rows((r % dil4) * w4 + r // dil4, t_len // dil, dil // dil4), :]

    nt = (((1,), (1,)), ((), ()))
    for bi, (_, dil) in enumerate(DILATED_BRANCHES):
        wq = t_len // dil
        nq = wq // p
        wk = p + wq
        base = kv_base[bi]

        @pl.when(tile == 0)
        def _(dil=dil, wk=wk, base=base):
            for r in range(dil):
                for dst_ref in (ks_ref, vs_ref):
                    dst_ref[base + r * wk:base + r * wk + p, :] = jnp.zeros((p, A_HEAD_DIM), BF16)

        @pl.when(tile > 0)
        def _(dil=dil, wk=wk, wq=wq, base=base):
            for r in range(dil):
                for dst_ref in (ks_ref, vs_ref):
                    dst_ref[base + r * wk:base + r * wk + p, :] = dst_ref[base + r * wk + wq:base + (r + 1) * wk, :]

        for r in range(dil):
            qs_ref[r * wq:(r + 1) * wq, :] = residue_rows(0, r, dil).astype(BF16)
            ks_ref[base + r * wk + p:base + (r + 1) * wk, :] = residue_rows(1, r, dil).astype(BF16)
            vs_ref[base + r * wk + p:base + (r + 1) * wk, :] = residue_rows(2, r, dil).astype(BF16)

        def group(g, carry, dil=dil, nq=nq, wk=wk, bi=bi, base=base):
            starts = []
            for u in range(ATT_UNROLL):
                idx = g * ATT_UNROLL + u
                r = idx // nq
                n = idx - r * nq
                q0 = pl.multiple_of(idx * p, p)
                k0 = pl.multiple_of(base + r * wk + n * p, p)
                s = lax.dot_general(qs_ref[pl.ds(q0, p), :], ks_ref[pl.ds(k0, 2 * p), :], nt,
                                    preferred_element_type=F32) + bias_ref[bi]
                no_history = jnp.logical_and(tile == 0, n == 0)
                s_ref[u * p:(u + 1) * p, :] = jnp.where(jnp.logical_and(no_history, in_prev_block), -jnp.inf, s)
                starts.append((k0, r + dil * p * n))
            s = s_ref[...]
            m = jnp.max(s, axis=-1, keepdims=True)
            e = jnp.exp(s - m)
            l = jnp.sum(e, axis=-1, keepdims=True)
            e_ref[...] = e.astype(BF16)
            inv_l = 1.0 / l
            lse = m + jnp.log(l)
            for u, (k0, t0) in enumerate(starts):
                o = jnp.dot(e_ref[u * p:(u + 1) * p, :], vs_ref[pl.ds(k0, 2 * p), :], preferred_element_type=F32)
                o_ref[bi, rows(t0, p, dil), :] = o * inv_l[u * p:(u + 1) * p]
                l_ref[bi, rows(t0, p, dil), :] = jnp.broadcast_to(lse[u * p:(u + 1) * p], (p, A_HEAD_DIM))
            return carry

        lax.fori_loop(0, (dil * nq) // ATT_UNROLL, group, 0)

    chunk = 256
    for c0 in range(0, t_len, chunk):
        sl = slice(c0, c0 + chunk)
        l1, l2, l3 = l_ref[0, sl, :], l_ref[1, sl, :], l_ref[2, sl, :]
        mx = jnp.maximum(jnp.maximum(l1, l2), l3)
        e1, e2, e3 = jnp.exp(l1 - mx), jnp.exp(l2 - mx), jnp.exp(l3 - mx)
        mix = (e1 * o_ref[0, sl, :] + e2 * o_ref[1, sl, :] + e3 * o_ref[2, sl, :]) / (e1 + e2 + e3)
        out_ref[sl, :] = mix.astype(out_ref.dtype)


def _dilated_mixture_attention(proj, bsz, seq, n_heads):
    m = proj.shape[0]
    hd = A_HEAD_DIM
    t_len = ATT_TILE
    nt = seq // t_len
    slopes = 2.0 ** (-8.0 * jnp.arange(1, n_heads + 1, dtype=F32) / n_heads)
    slopes = jnp.broadcast_to(slopes[:, None, None], (n_heads, 1, 2 * BAND_BLOCK))

    kv_rows = sum(_kv_rows(dil) for _, dil in DILATED_BRANCHES)

    def head_block(col):
        return pl.BlockSpec((t_len, hd), lambda b, h, t: (b * nt + t, col * n_heads + h))

    return pl.pallas_call(
        _attn_kernel,
        grid=(bsz, n_heads, nt),
        in_specs=[head_block(0), head_block(1), head_block(2),
                  pl.BlockSpec((1, 1, 2 * BAND_BLOCK), lambda b, h, t: (h, 0, 0))],
        out_specs=pl.BlockSpec((t_len, hd), lambda b, h, t: (b * nt + t, h)),
        out_shape=jax.ShapeDtypeStruct((m, n_heads * hd), BF16),
        scratch_shapes=[pltpu.VMEM((3, t_len, hd), F32),
                        pltpu.VMEM((t_len, hd), BF16),
                        pltpu.VMEM((kv_rows, hd), BF16),
                        pltpu.VMEM((kv_rows, hd), BF16),
                        pltpu.VMEM((3, t_len, hd), F32),
                        pltpu.VMEM((3, t_len, hd), F32),
                        pltpu.VMEM((3, BAND_BLOCK, 2 * BAND_BLOCK), F32),
                        pltpu.VMEM((ATT_UNROLL * BAND_BLOCK, 2 * BAND_BLOCK), F32),
                        pltpu.VMEM((ATT_UNROLL * BAND_BLOCK, 2 * BAND_BLOCK), BF16)],
        compiler_params=_params(("parallel", "parallel", "arbitrary")),
        name="dilated_attention",
    )(proj, proj, proj, slopes)


def _conv_tile(has_history, a_ref, g_ref, ha_ref, hg_ref, w_ref, cb_ref, lg_ref, lb_ref, o_ref, h_ref, hs_ref, c_ref,
               n_parts, side_work):
    ts, ch = a_ref.shape
    hist = ha_ref[...] * jax.nn.sigmoid(hg_ref[...])
    h_ref[0:CONV_HALO, :] = jnp.where(has_history, hist, 0.0)
    h_ref[CONV_HALO:, :] = a_ref[...] * jax.nn.sigmoid(g_ref[...])
    first = CONV_HALO - (CONV_WIDTH - 1)
    h_rows = h_ref.shape[0]
    for s in range(1, SUBLANES):
        hs_ref[s - 1, SUBLANES - s:SUBLANES - s + h_rows, :] = h_ref[...]
    rows, lanes = 32, 512
    groups = rows // SUBLANES
    part_rows = ts // n_parts

    def part(p, carry):
        base = pl.multiple_of(p * part_rows, part_rows)
        for r0 in range(0, part_rows, rows):
            for c0 in range(0, ch, lanes):
                acc = jnp.broadcast_to(cb_ref[:, c0:c0 + lanes], (groups, SUBLANES, lanes))
                for j in range(CONV_WIDTH):
                    a8, s = (first + j) // SUBLANES * SUBLANES, (first + j) % SUBLANES
                    if s == 0:
                        tap = h_ref[pl.ds(base + r0 + a8, rows), c0:c0 + lanes]
                    else:
                        tap = hs_ref[s - 1, pl.ds(base + SUBLANES + r0 + a8, rows), c0:c0 + lanes]
                    acc = acc + w_ref[j, :, c0:c0 + lanes] * tap.reshape(groups, SUBLANES, lanes)
                c_ref[pl.ds(base + r0, rows), c0:c0 + lanes] = acc.reshape(rows, lanes)
        side_work(p)
        return carry

    lax.fori_loop(0, n_parts, part, 0)
    y = _layer_norm(c_ref[...], lg_ref[...], lb_ref[...])
    o_ref[...] = (y * jax.nn.sigmoid(y)).astype(o_ref.dtype)


def _qkv_conv_kernel(n_row_tiles, tiles_per_seq, x_ref, w_ref, s_ref, a_ref, g_ref, ha_ref, hg_ref, cw_ref, cb_ref,
                     lg_ref, lb_ref, o_ref, conv_ref, wb_ref, h_ref, hs_ref, c_ref):
    j, i = pl.program_id(0), pl.program_id(1)

    @pl.when(i == 0)
    def _():
        wb_ref[...] = w_ref[...].astype(BF16)

    conv_tile = j * n_row_tiles + i
    rm = x_ref.shape[0] // QKV_ROW_CHUNKS

    def matmul_chunk(c):
        r0 = pl.multiple_of(c * rm, rm)
        acc = jnp.dot(x_ref[pl.ds(r0, rm), :], wb_ref[...], preferred_element_type=F32)
        o_ref[pl.ds(r0, rm), :] = acc * s_ref[...]

    _conv_tile(conv_tile % tiles_per_seq != 0, a_ref, g_ref, ha_ref, hg_ref, cw_ref, cb_ref, lg_ref, lb_ref,
               conv_ref, h_ref, hs_ref, c_ref, QKV_ROW_CHUNKS, matmul_chunk)


def _qkv_proj_conv(x, w_stack, layer, n_cols, col_scale, u, seq, conv_w, conv_b, ln_g, ln_b, tm=1024, tn=768):
    m, k = x.shape
    ch = conv_w.shape[1]
    n_col_blocks, n_row_tiles = n_cols // tn, m // tm
    ts = m // (n_col_blocks * n_row_tiles)
    assert n_cols % tn == 0 and m % tm == 0 and ts % CONV_HALO == 0 and seq % ts == 0
    per = ts // CONV_HALO
    w_rows = jnp.broadcast_to(conv_w[:, None, :], (CONV_WIDTH, SUBLANES, ch))

    def tile(col):
        return pl.BlockSpec((ts, ch), lambda j, i: (j * n_row_tiles + i, col))

    def halo(col):
        return pl.BlockSpec((CONV_HALO, ch), lambda j, i: (jnp.maximum((j * n_row_tiles + i) * per - 1, 0), col))

    def const(shape):
        return pl.BlockSpec(shape, lambda j, i: (0,) * len(shape))

    return pl.pallas_call(
        functools.partial(_qkv_conv_kernel, n_row_tiles, seq // ts),
        grid=(n_col_blocks, n_row_tiles),
        in_specs=[pl.BlockSpec((tm, k), lambda j, i: (i, 0)),
                  pl.BlockSpec((None, k, tn), lambda j, i: (layer, 0, j)),
                  pl.BlockSpec((1, tn), lambda j, i: (0, j)),
                  tile(0), tile(1), halo(0), halo(1),
                  const((CONV_WIDTH, SUBLANES, ch)), const((1, ch)), const((1, ch)), const((1, ch))],
        out_specs=[pl.BlockSpec((tm, tn), lambda j, i: (i, j)),
                   pl.BlockSpec((ts, ch), lambda j, i: (j * n_row_tiles + i, 0))],
        out_shape=[jax.ShapeDtypeStruct((m, n_cols), F32), jax.ShapeDtypeStruct((m, ch), BF16)],
        scratch_shapes=[pltpu.VMEM((k, tn), BF16),
                        pltpu.VMEM((ts + CONV_HALO, ch), F32),
                        pltpu.VMEM((SUBLANES - 1, ts + CONV_HALO + SUBLANES, ch), F32),
                        pltpu.VMEM((ts, ch), F32)],
        compiler_params=_params(("parallel", "arbitrary")),
        name="qkv_proj_conv",
    )(x, w_stack, col_scale, u, u, u, u, w_rows, conv_b.reshape(1, ch), ln_g.reshape(1, ch), ln_b.reshape(1, ch))


def _gates_kernel(x_ref, w_ref, bias_ref, gc_ref, gr_ref):
    length = x_ref.shape[0]
    w = w_ref[...].astype(BF16)
    w = jnp.concatenate([w, jnp.zeros((GATE_LANES - w.shape[0], w.shape[1]), BF16)], axis=0)
    pre = lax.dot_general(x_ref[...], w, (((1,), (1,)), ((), ())), preferred_element_type=F32) + bias_ref[...]
    log_f = jnp.minimum(pre, 0.0) - jnp.log1p(jnp.exp(-jnp.abs(pre)))
    ti = lax.broadcasted_iota(jnp.int32, (length, length), 0)
    si = lax.broadcasted_iota(jnp.int32, (length, length), 1)
    tril = (ti >= si).astype(F32)
    cum_f = jnp.dot(tril, log_f, preferred_element_type=F32, precision=lax.Precision.HIGHEST)
    lane = lax.broadcasted_iota(jnp.int32, pre.shape, 1)
    gates = jnp.where(lane < C_HEADS, pre, cum_f)
    gc_ref[...] = gates
    gr_ref[...] = gates.T


def _mlstm_gates(x, w_stack_t, layer, gate_row0, bias):
    m, d = x.shape
    length = MLSTM_CHUNK
    gate_rows = 2 * C_HEADS
    gate_block = gate_row0 // gate_rows
    return pl.pallas_call(
        _gates_kernel,
        grid=(m // length,),
        in_specs=[pl.BlockSpec((length, d), lambda i: (i, 0)),
                  pl.BlockSpec((None, gate_rows, d), lambda i: (layer, gate_block, 0)),
                  pl.BlockSpec((1, GATE_LANES), lambda i: (0, 0))],
        out_specs=[pl.BlockSpec((length, GATE_LANES), lambda i: (i, 0)),
                   pl.BlockSpec((GATE_LANES, length), lambda i: (0, i))],
        out_shape=[jax.ShapeDtypeStruct((m, GATE_LANES), F32),
                   jax.ShapeDtypeStruct((GATE_LANES, m), F32)],
        compiler_params=_params(("parallel",)),
        name="mlstm_gates",
    )(x, w_stack_t, bias)


def _mlstm_kernel(dk, dv, q_ref, k_ref, v_ref, o_ref, gc_ref, gr_ref, ng_ref, y_ref, c_ref, n_ref, m_ref):
    length = q_ref.shape[0]

    @pl.when(pl.program_id(1) == 0)
    def _():
        c_ref[...] = jnp.zeros_like(c_ref)
        n_ref[...] = jnp.zeros_like(n_ref)
        m_ref[...] = jnp.zeros_like(m_ref)

    ti = lax.broadcasted_iota(jnp.int32, (length, length), 0)
    si = lax.broadcasted_iota(jnp.int32, (length, length), 1)
    causal = ti >= si
    for h in range(C_HEADS):
        q = q_ref[:, h * dk:(h + 1) * dk]
        k = k_ref[:, h * dk:(h + 1) * dk]
        v = v_ref[:, h * dv:(h + 1) * dv]
        i_row = gr_ref[h:h + 1, :]
        b_row = gr_ref[C_HEADS + h:C_HEADS + h + 1, :]
        i_col = gc_ref[:, h:h + 1]
        b_col = gc_ref[:, C_HEADS + h:C_HEADS + h + 1]
        m_prev = m_ref[h:h + 1, 0:1]
        c_prev = c_ref[h]
        n_prev = n_ref[h:h + 1, :]

        dmat = jnp.where(causal, b_col - b_row + i_row, -jnp.inf)
        inter = b_col + m_prev
        m_t = jnp.maximum(inter, jnp.max(dmat, axis=-1, keepdims=True))
        w = jnp.exp(dmat - m_t)
        a = jnp.exp(inter - m_t)
        qk = lax.dot_general(q, k, (((1,), (1,)), ((), ())), preferred_element_type=F32)
        sqk = qk * w
        num = (a * jnp.dot(q, c_prev.astype(BF16), preferred_element_type=F32)
               + jnp.dot(sqk.astype(BF16), v, preferred_element_type=F32))
        den = (a * jnp.sum(q.astype(F32) * n_prev, axis=-1, keepdims=True)
               + jnp.sum(sqk, axis=-1, keepdims=True))
        hh = num / jnp.maximum(jnp.abs(den), jnp.exp(-m_t))

        b_last = b_col[length - 1:length, :]
        g = b_last - b_col + i_col
        m_new = jnp.maximum(b_last + m_prev, jnp.max(g, axis=0, keepdims=True))
        decay = jnp.exp(b_last + m_prev - m_new)
        kw = k.astype(F32) * jnp.exp(g - m_new)
        c_ref[h] = decay * c_prev + lax.dot_general(kw.astype(BF16), v, (((0,), (0,)), ((), ())),
                                                    preferred_element_type=F32)
        n_ref[h:h + 1, :] = decay * n_prev + jnp.sum(kw, axis=0, keepdims=True)
        m_ref[h:h + 1, :] = jnp.broadcast_to(m_new, (1, m_ref.shape[1]))

        mu = jnp.mean(hh, axis=-1, keepdims=True)
        hc = hh - mu
        var = jnp.mean(hc * hc, axis=-1, keepdims=True)
        hn = hc * lax.rsqrt(var + LN_EPS) * ng_ref[:, h * dv:(h + 1) * dv]
        y_ref[:, h * dv:(h + 1) * dv] = (jax.nn.sigmoid(o_ref[:, h * dv:(h + 1) * dv]) * hn).astype(y_ref.dtype)


def _mlstm(qkv, o, gates_col, gates_row, norm_g, bsz, seq, dk, dv):
    length = MLSTM_CHUNK
    nc = seq // length
    qk_w = C_HEADS * dk
    v_w = C_HEADS * dv
    v_blk = (2 * qk_w) // v_w
    return pl.pallas_call(
        functools.partial(_mlstm_kernel, dk, dv),
        grid=(bsz, nc),
        in_specs=[pl.BlockSpec((length, qk_w), lambda b, c: (b * nc + c, 0)),
                  pl.BlockSpec((length, qk_w), lambda b, c: (b * nc + c, 1)),
                  pl.BlockSpec((length, v_w), lambda b, c: (b * nc + c, v_blk)),
                  pl.BlockSpec((length, v_w), lambda b, c: (b * nc + c, 0)),
                  pl.BlockSpec((length, GATE_LANES), lambda b, c: (b * nc + c, 0)),
                  pl.BlockSpec((GATE_LANES, length), lambda b, c: (0, b * nc + c)),
                  pl.BlockSpec((1, v_w), lambda b, c: (0, 0))],
        out_specs=pl.BlockSpec((length, v_w), lambda b, c: (b * nc + c, 0)),
        out_shape=jax.ShapeDtypeStruct((bsz * seq, v_w), BF16),
        scratch_shapes=[pltpu.VMEM((C_HEADS, dk, dv), F32),
                        pltpu.VMEM((8, dk), F32),
                        pltpu.VMEM((8, GATE_LANES), F32)],
        compiler_params=_params(("parallel", "arbitrary")),
        name="mlstm_chunk",
    )(qkv, qkv, qkv, o, gates_col, gates_row, norm_g)


def _even_mixer(xb, x, w_in, j, conv_w, conv_b, conv_ln_g, conv_ln_b, w_out, ln_g, ln_b, alpha, bsz, seq):
    d = x.shape[1]
    a_width = d // 2
    n_heads = a_width // A_HEAD_DIM
    in_width = w_in.shape[2]
    qkv_w = 3 * a_width
    qkv_scale = jnp.ones((1, qkv_w), F32).at[:, :a_width].set(A_HEAD_DIM ** -0.5)
    u_scale = jnp.ones((1, in_width - qkv_w), F32)
    if xb is None:
        u, xb = _matmul(x, w_in, j, qkv_w, in_width - qkv_w, u_scale, F32, emit_bf16_x=True)
    else:
        u = _matmul(xb, w_in, j, qkv_w, in_width - qkv_w, u_scale, F32)
    qkv, conv = _qkv_proj_conv(xb, w_in, j, qkv_w, qkv_scale, u, seq, conv_w, conv_b, conv_ln_g, conv_ln_b)
    att = _dilated_mixture_attention(qkv, bsz, seq, n_heads)
    return _out_proj_ln([att, conv], w_out, j, x, ln_g.reshape(1, d), ln_b.reshape(1, d), alpha)


def _odd_mixer(xb, x, w_in_t, j, igate_b, fgate_b, norm_g, w_out, ln_g, ln_b, alpha, bsz, seq):
    d = x.shape[1]
    v_w = d
    dv = v_w // C_HEADS
    dk = dv // 2
    qk_w = C_HEADS * dk
    qkv_w = 2 * qk_w + v_w
    qkv_scale = jnp.ones((1, qkv_w), F32).at[:, :qk_w].set(dk ** -0.5)
    qkv = _matmul(xb, w_in_t, j, 0, qkv_w, qkv_scale, BF16, w_is_transposed=True)
    o = _matmul(xb, w_in_t, j, qkv_w, v_w, jnp.ones((1, v_w), F32), F32, w_is_transposed=True)
    bias = jnp.zeros((1, GATE_LANES), F32).at[0, :C_HEADS].set(igate_b).at[0, C_HEADS:2 * C_HEADS].set(fgate_b)
    gates_col, gates_row = _mlstm_gates(xb, w_in_t, j, qkv_w + v_w, bias)
    y = _mlstm(qkv, o, gates_col, gates_row, norm_g.reshape(1, v_w), bsz, seq, dk, dv)
    return _out_proj_ln([y], w_out, j, x, ln_g.reshape(1, d), ln_b.reshape(1, d), alpha)


def kernel(x, even_w_in, even_conv_w, even_conv_b, even_conv_ln_g, even_conv_ln_b, even_w_out,
           odd_w_in, odd_igate_b, odd_fgate_b, odd_norm_g, odd_w_out, mix_ln_g, mix_ln_b,
           ffn_w1, ffn_w2, ffn_ln_g, ffn_ln_b):
    bsz, seq, d = x.shape
    depth = mix_ln_g.shape[0]
    alpha = (2 * depth) ** 0.25
    xf = x.reshape(bsz * seq, d)
    xb = None
    odd_w_in_t = jnp.swapaxes(odd_w_in, 1, 2)
    for layer in range(depth):
        j = layer // 2
        if layer % 2 == 0:
            xf, xb = _even_mixer(xb, xf, even_w_in, j, even_conv_w[j], even_conv_b[j], even_conv_ln_g[j],
                                 even_conv_ln_b[j], even_w_out, mix_ln_g[layer], mix_ln_b[layer],
                                 alpha, bsz, seq)
        else:
            xf, xb = _odd_mixer(xb, xf, odd_w_in_t, j, odd_igate_b[j], odd_fgate_b[j], odd_norm_g[j],
                                odd_w_out, mix_ln_g[layer], mix_ln_b[layer], alpha, bsz, seq)
        xf, xb = _ffn_ln(xf, ffn_w1, ffn_w2, layer,
                         ffn_ln_g[layer].reshape(1, d), ffn_ln_b[layer].reshape(1, d), alpha)
    return xf.reshape(bsz, seq, d)
```

```python
import functools

import jax
import jax.numpy as jnp
from jax import lax
from jax.experimental import pallas as pl
from jax.experimental.pallas import tpu as pltpu

LN_EPS = 1e-5
DILATED_BRANCHES = ((128, 1), (512, 4), (2048, 16))
BAND_BLOCK = 128
A_HEAD_DIM = 128
ATT_TILE = 2048
ATT_UNROLL = 16
CONV_WIDTH = 31
CONV_HALO = 32
C_HEADS = 4
MLSTM_CHUNK = 256
GATE_LANES = 128
GATE_CHUNKS_PER_STEP = 4
SUBLANES = 8
FFN_DMA_CHUNKS = 4
QKV_ROW_CHUNKS = 2
V7X_VMEM_LIMIT = 56 * 1024 * 1024
V7X_VMEM_LIMIT_FFN = 58 * 1024 * 1024

BF16 = jnp.bfloat16
F32 = jnp.float32


def _params(semantics, vmem_limit=V7X_VMEM_LIMIT):
    return pltpu.CompilerParams(dimension_semantics=semantics, vmem_limit_bytes=vmem_limit)


def _layer_norm(z, g, b):
    mu = jnp.mean(z, axis=-1, keepdims=True)
    zc = z - mu
    var = jnp.mean(zc * zc, axis=-1, keepdims=True)
    return zc * lax.rsqrt(var + LN_EPS) * g + b


def _matmul_kernel(w_is_transposed, x_ref, w_ref, s_ref, o_ref, wb_ref):
    @pl.when(pl.program_id(1) == 0)
    def _():
        w = w_ref[...]
        wb_ref[...] = (w.T if w_is_transposed else w).astype(BF16)

    acc = jnp.dot(x_ref[...], wb_ref[...], preferred_element_type=F32)
    o_ref[...] = (acc * s_ref[...]).astype(o_ref.dtype)


def _matmul(x, w_stack, layer, col0, n_cols, col_scale, out_dtype, w_is_transposed=False, tm=1024, tn=1024):
    m, k = x.shape
    cb0 = col0 // tn
    if w_is_transposed:
        w_spec = pl.BlockSpec((None, tn, k), lambda j, i: (layer, cb0 + j, 0))
    else:
        w_spec = pl.BlockSpec((None, k, tn), lambda j, i: (layer, 0, cb0 + j))
    return pl.pallas_call(
        functools.partial(_matmul_kernel, w_is_transposed),
        grid=(n_cols // tn, m // tm),
        in_specs=[pl.BlockSpec((tm, k), lambda j, i: (i, 0)),
                  w_spec,
                  pl.BlockSpec((1, tn), lambda j, i: (0, j))],
        out_specs=pl.BlockSpec((tm, tn), lambda j, i: (i, j)),
        out_shape=jax.ShapeDtypeStruct((m, n_cols), out_dtype),
        scratch_shapes=[pltpu.VMEM((k, tn), BF16)],
        compiler_params=_params(("parallel", "arbitrary")),
        name="proj_matmul",
    )(x, w_stack, col_scale)


def _out_proj_ln_kernel(alpha, n_in, *refs):
    ys = refs[:n_in]
    w_ref, x_ref, g_ref, b_ref, of_ref, ob_ref, wb_ref = refs[n_in:]

    @pl.when(pl.program_id(0) == 0)
    def _():
        wb_ref[...] = w_ref[...].astype(BF16)

    acc = alpha * x_ref[...]
    k0 = 0
    for y_ref in ys:
        kw = y_ref.shape[1]
        acc = acc + jnp.dot(y_ref[...], wb_ref[k0:k0 + kw, :], preferred_element_type=F32)
        k0 += kw
    out = _layer_norm(acc, g_ref[...], b_ref[...])
    of_ref[...] = out
    ob_ref[...] = out.astype(BF16)


def _out_proj_ln(ys, w_stack, layer, x, g, b, alpha, tm=512):
    m, d = x.shape
    n_in = len(ys)
    k = w_stack.shape[1]
    in_specs = ([pl.BlockSpec((tm, y.shape[1]), lambda i: (i, 0)) for y in ys]
                + [pl.BlockSpec((None, k, d), lambda i: (layer, 0, 0), pipeline_mode=pl.Buffered(1)),
                   pl.BlockSpec((tm, d), lambda i: (i, 0)),
                   pl.BlockSpec((1, d), lambda i: (0, 0)),
                   pl.BlockSpec((1, d), lambda i: (0, 0))])
    return pl.pallas_call(
        functools.partial(_out_proj_ln_kernel, alpha, n_in),
        grid=(m // tm,),
        in_specs=in_specs,
        out_specs=[pl.BlockSpec((tm, d), lambda i: (i, 0)),
                   pl.BlockSpec((tm, d), lambda i: (i, 0))],
        out_shape=[jax.ShapeDtypeStruct((m, d), F32), jax.ShapeDtypeStruct((m, d), BF16)],
        scratch_shapes=[pltpu.VMEM((k, d), BF16)],
        compiler_params=_params(("arbitrary",)),
        name="out_proj_ln",
    )(*ys, w_stack, x, g, b)


def _ffn_kernel(alpha, layer, tm, th, n_blocks, x_hbm, w1_hbm, w2_hbm, g_ref, b_ref, of_hbm, ob_hbm,
                x_buf, xb_ref, acc_ref, of_buf, ob_buf, w1_buf, w2_buf, sem_x, sem_w, sem_o):
    i = pl.program_id(0)
    last_tile = pl.num_programs(0) - 1
    chunks = FFN_DMA_CHUNKS

    def w_copies(j, slot):
        col = pl.multiple_of(j * th, th)
        r1 = w1_buf.shape[1] // chunks
        r2 = th // chunks
        copies = []
        for c in range(chunks):
            copies.append(pltpu.make_async_copy(
                w1_hbm.at[layer, pl.ds(c * r1, r1), pl.ds(col, th)],
                w1_buf.at[slot, pl.ds(c * r1, r1), :], sem_w.at[0, slot, c]))
            copies.append(pltpu.make_async_copy(
                w2_hbm.at[layer, pl.ds(col + c * r2, r2), :],
                w2_buf.at[slot, pl.ds(c * r2, r2), :], sem_w.at[1, slot, c]))
        return copies

    def x_copies(tile):
        rows = tm // chunks
        row0 = pl.multiple_of(tile * tm, tm)
        return [pltpu.make_async_copy(x_hbm.at[pl.ds(row0 + c * rows, rows), :],
                                      x_buf.at[pl.ds(c * rows, rows), :], sem_x.at[c]) for c in range(chunks)]

    def out_copies(tile):
        rows = tm // chunks
        row0 = pl.multiple_of(tile * tm, tm)
        copies = []
        for c in range(chunks):
            copies.append(pltpu.make_async_copy(of_buf.at[pl.ds(c * rows, rows), :],
                                                of_hbm.at[pl.ds(row0 + c * rows, rows), :], sem_o.at[0, c]))
            copies.append(pltpu.make_async_copy(ob_buf.at[pl.ds(c * rows, rows), :],
                                                ob_hbm.at[pl.ds(row0 + c * rows, rows), :], sem_o.at[1, c]))
        return copies

    def start(copies):
        for copy in copies:
            copy.start()

    def wait(copies):
        for copy in copies:
            copy.wait()

    @pl.when(i == 0)
    def _():
        start(x_copies(0))
        start(w_copies(0, 0))

    wait(x_copies(i))
    x = x_buf[...]
    xb_ref[...] = x.astype(BF16)
    acc_ref[...] = alpha * x

    @pl.when(i < last_tile)
    def _():
        start(x_copies(i + 1))

    def block(j, slot):
        wait(w_copies(j, slot))
        h = jnp.dot(xb_ref[...], w1_buf[slot].astype(BF16), preferred_element_type=F32)
        h = jnp.maximum(h, 0.0)
        h = (h * h).astype(BF16)
        acc_ref[...] += jnp.dot(h, w2_buf[slot].astype(BF16), preferred_element_type=F32)

    def pair(p, carry):
        j = 2 * p
        start(w_copies(j + 1, 1))
        block(j, 0)

        @pl.when(j + 2 < n_blocks)
        def _():
            start(w_copies(j + 2, 0))

        @pl.when(jnp.logical_and(j + 2 == n_blocks, i < last_tile))
        def _():
            start(w_copies(0, 0))

        block(j + 1, 1)
        return carry

    lax.fori_loop(0, n_blocks // 2, pair, 0)

    @pl.when(i > 0)
    def _():
        wait(out_copies(i - 1))

    out = _layer_norm(acc_ref[...], g_ref[...], b_ref[...])
    of_buf[...] = out
    ob_buf[...] = out.astype(BF16)
    start(out_copies(i))

    @pl.when(i == last_tile)
    def _():
        wait(out_copies(i))


def _ffn_ln(x, w1, w2, layer, g, b, alpha, tm=1024, th=512):
    m, d = x.shape
    f = w1.shape[2]
    n_blocks = f // th
    assert n_blocks % 2 == 0 and m % tm == 0
    hbm = pl.BlockSpec(memory_space=pl.ANY)
    return pl.pallas_call(
        functools.partial(_ffn_kernel, alpha, layer, tm, th, n_blocks),
        grid=(m // tm,),
        in_specs=[hbm, hbm, hbm,
                  pl.BlockSpec((1, d), lambda i: (0, 0)),
                  pl.BlockSpec((1, d), lambda i: (0, 0))],
        out_specs=[hbm, hbm],
        out_shape=[jax.ShapeDtypeStruct((m, d), F32), jax.ShapeDtypeStruct((m, d), BF16)],
        scratch_shapes=[pltpu.VMEM((tm, d), F32),
                        pltpu.VMEM((tm, d), BF16),
                        pltpu.VMEM((tm, d), F32),
                        pltpu.VMEM((tm, d), F32),
                        pltpu.VMEM((tm, d), BF16),
                        pltpu.VMEM((2, d, th), F32),
                        pltpu.VMEM((2, th, d), F32),
                        pltpu.SemaphoreType.DMA((FFN_DMA_CHUNKS,)),
                        pltpu.SemaphoreType.DMA((2, 2, FFN_DMA_CHUNKS)),
                        pltpu.SemaphoreType.DMA((2, FFN_DMA_CHUNKS))],
        compiler_params=_params(("arbitrary",), V7X_VMEM_LIMIT_FFN),
        name="ffn_ln",
    )(x, w1, w2, g, b)


def _kv_rows(dil):
    return dil * BAND_BLOCK + ATT_TILE


def _attn_kernel(q_ref, k_ref, v_ref, slope_ref, out_ref,
                 g4_ref, qs_ref, ks_ref, vs_ref, o_ref, l_ref, bias_ref, s_ref, e_ref):
    tile = pl.program_id(2)
    p = BAND_BLOCK
    t_len = ATT_TILE
    qi = lax.broadcasted_iota(jnp.int32, (p, 2 * p), 0)
    kj = lax.broadcasted_iota(jnp.int32, (p, 2 * p), 1)
    dist = p + qi - kj
    valid = jnp.logical_and(dist >= 0, dist <= p)
    in_prev_block = kj < p
    slope = slope_ref[0]
    for bi, (_, dil) in enumerate(DILATED_BRANCHES):
        bias_ref[bi] = jnp.where(valid, -(slope * (dist * dil).astype(F32)), -jnp.inf)

    def rows(start, size, dil):
        return pl.ds(start, size, stride=dil) if dil > 1 else pl.ds(start, size)

    kv_base = [sum(_kv_rows(d) for _, d in DILATED_BRANCHES[:bi]) for bi in range(len(DILATED_BRANCHES))]
    srcs = (q_ref, k_ref, v_ref)
    dil4 = DILATED_BRANCHES[1][1]
    w4 = t_len // dil4
    for x, src in enumerate(srcs):
        for r in range(dil4):
            g4_ref[x, r * w4:(r + 1) * w4, :] = src[rows(r, w4, dil4), :]

    def residue_rows(x, r, dil):
        if dil == 1:
            return srcs[x][...]
        if dil == dil4:
            return g4_ref[x, r * w4:(r + 1) * w4, :]
        return g4_ref[x, rows((r % dil4) * w4 + r // dil4, t_len // dil, dil // dil4), :]

    nt = (((1,), (1,)), ((), ()))
    for bi, (_, dil) in enumerate(DILATED_BRANCHES):
        wq = t_len // dil
        nq = wq // p
        wk = p + wq
        base = kv_base[bi]

        @pl.when(tile == 0)
        def _(dil=dil, wk=wk, base=base):
            for r in range(dil):
                for dst_ref in (ks_ref, vs_ref):
                    dst_ref[base + r * wk:base + r * wk + p, :] = jnp.zeros((p, A_HEAD_DIM), BF16)

        @pl.when(tile > 0)
        def _(dil=dil, wk=wk, wq=wq, base=base):
            for r in range(dil):
                for dst_ref in (ks_ref, vs_ref):
                    dst_ref[base + r * wk:base + r * wk + p, :] = dst_ref[base + r * wk + wq:base + (r + 1) * wk, :]

        for r in range(dil):
            qs_ref[r * wq:(r + 1) * wq, :] = residue_rows(0, r, dil).astype(BF16)
            ks_ref[base + r * wk + p:base + (r + 1) * wk, :] = residue_rows(1, r, dil).astype(BF16)
            vs_ref[base + r * wk + p:base + (r + 1) * wk, :] = residue_rows(2, r, dil).astype(BF16)

        def group(g, carry, dil=dil, nq=nq, wk=wk, bi=bi, base=base):
            starts = []
            for u in range(ATT_UNROLL):
                idx = g * ATT_UNROLL + u
                r = idx // nq
                n = idx - r * nq
                q0 = pl.multiple_of(idx * p, p)
                k0 = pl.multiple_of(base + r * wk + n * p, p)
                s = lax.dot_general(qs_ref[pl.ds(q0, p), :], ks_ref[pl.ds(k0, 2 * p), :], nt,
                                    preferred_element_type=F32) + bias_ref[bi]
                no_history = jnp.logical_and(tile == 0, n == 0)
                s_ref[u * p:(u + 1) * p, :] = jnp.where(jnp.logical_and(no_history, in_prev_block), -jnp.inf, s)
                starts.append((k0, r + dil * p * n))
            s = s_ref[...]
            m = jnp.max(s, axis=-1, keepdims=True)
            e = jnp.exp(s - m)
            l = jnp.sum(e, axis=-1, keepdims=True)
            e_ref[...] = e.astype(BF16)
            inv_l = 1.0 / l
            lse = m + jnp.log(l)
            for u, (k0, t0) in enumerate(starts):
                o = jnp.dot(e_ref[u * p:(u + 1) * p, :], vs_ref[pl.ds(k0, 2 * p), :], preferred_element_type=F32)
                o_ref[bi, rows(t0, p, dil), :] = o * inv_l[u * p:(u + 1) * p]
                l_ref[bi, rows(t0, p, dil), :] = jnp.broadcast_to(lse[u * p:(u + 1) * p], (p, A_HEAD_DIM))
            return carry

        lax.fori_loop(0, (dil * nq) // ATT_UNROLL, group, 0)

    chunk = 256
    for c0 in range(0, t_len, chunk):
        sl = slice(c0, c0 + chunk)
        l1, l2, l3 = l_ref[0, sl, :], l_ref[1, sl, :], l_ref[2, sl, :]
        mx = jnp.maximum(jnp.maximum(l1, l2), l3)
        e1, e2, e3 = jnp.exp(l1 - mx), jnp.exp(l2 - mx), jnp.exp(l3 - mx)
        mix = (e1 * o_ref[0, sl, :] + e2 * o_ref[1, sl, :] + e3 * o_ref[2, sl, :]) / (e1 + e2 + e3)
        out_ref[sl, :] = mix.astype(out_ref.dtype)


def _dilated_mixture_attention(proj, bsz, seq, n_heads):
    m = proj.shape[0]
    hd = A_HEAD_DIM
    t_len = ATT_TILE
    nt = seq // t_len
    slopes = 2.0 ** (-8.0 * jnp.arange(1, n_heads + 1, dtype=F32) / n_heads)
    slopes = jnp.broadcast_to(slopes[:, None, None], (n_heads, 1, 2 * BAND_BLOCK))

    kv_rows = sum(_kv_rows(dil) for _, dil in DILATED_BRANCHES)

    def head_block(col):
        return pl.BlockSpec((t_len, hd), lambda b, h, t: (b * nt + t, col * n_heads + h))

    return pl.pallas_call(
        _attn_kernel,
        grid=(bsz, n_heads, nt),
        in_specs=[head_block(0), head_block(1), head_block(2),
                  pl.BlockSpec((1, 1, 2 * BAND_BLOCK), lambda b, h, t: (h, 0, 0))],
        out_specs=pl.BlockSpec((t_len, hd), lambda b, h, t: (b * nt + t, h)),
        out_shape=jax.ShapeDtypeStruct((m, n_heads * hd), BF16),
        scratch_shapes=[pltpu.VMEM((3, t_len, hd), F32),
                        pltpu.VMEM((t_len, hd), BF16),
                        pltpu.VMEM((kv_rows, hd), BF16),
                        pltpu.VMEM((kv_rows, hd), BF16),
                        pltpu.VMEM((3, t_len, hd), F32),
                        pltpu.VMEM((3, t_len, hd), F32),
                        pltpu.VMEM((3, BAND_BLOCK, 2 * BAND_BLOCK), F32),
                        pltpu.VMEM((ATT_UNROLL * BAND_BLOCK, 2 * BAND_BLOCK), F32),
                        pltpu.VMEM((ATT_UNROLL * BAND_BLOCK, 2 * BAND_BLOCK), BF16)],
        compiler_params=_params(("parallel", "parallel", "arbitrary")),
        name="dilated_attention",
    )(proj, proj, proj, slopes)


def _conv_tile(has_history, a_ref, g_ref, ha_ref, hg_ref, w_ref, cb_ref, lg_ref, lb_ref, o_ref, h_ref, hs_ref, c_ref,
               n_parts, side_work):
    ts, ch = a_ref.shape
    hist = ha_ref[...] * jax.nn.sigmoid(hg_ref[...])
    h_ref[0:CONV_HALO, :] = jnp.where(has_history, hist, 0.0)
    h_ref[CONV_HALO:, :] = a_ref[...] * jax.nn.sigmoid(g_ref[...])
    first = CONV_HALO - (CONV_WIDTH - 1)
    h_rows = h_ref.shape[0]
    for s in range(1, SUBLANES):
        hs_ref[s - 1, SUBLANES - s:SUBLANES - s + h_rows, :] = h_ref[...]
    rows, lanes = 32, 512
    groups = rows // SUBLANES
    part_rows = ts // n_parts

    def part(p, carry):
        base = pl.multiple_of(p * part_rows, part_rows)
        for r0 in range(0, part_rows, rows):
            for c0 in range(0, ch, lanes):
                acc = jnp.broadcast_to(cb_ref[:, c0:c0 + lanes], (groups, SUBLANES, lanes))
                for j in range(CONV_WIDTH):
                    a8, s = (first + j) // SUBLANES * SUBLANES, (first + j) % SUBLANES
                    if s == 0:
                        tap = h_ref[pl.ds(base + r0 + a8, rows), c0:c0 + lanes]
                    else:
                        tap = hs_ref[s - 1, pl.ds(base + SUBLANES + r0 + a8, rows), c0:c0 + lanes]
                    acc = acc + w_ref[j, :, c0:c0 + lanes] * tap.reshape(groups, SUBLANES, lanes)
                c_ref[pl.ds(base + r0, rows), c0:c0 + lanes] = acc.reshape(rows, lanes)
        side_work(p)
        return carry

    lax.fori_loop(0, n_parts, part, 0)
    y = _layer_norm(c_ref[...], lg_ref[...], lb_ref[...])
    o_ref[...] = (y * jax.nn.sigmoid(y)).astype(o_ref.dtype)


def _qkv_conv_kernel(n_row_tiles, tiles_per_seq, x_ref, w_ref, s_ref, a_ref, g_ref, ha_ref, hg_ref, cw_ref, cb_ref,
                     lg_ref, lb_ref, o_ref, conv_ref, wb_ref, h_ref, hs_ref, c_ref):
    j, i = pl.program_id(0), pl.program_id(1)

    @pl.when(i == 0)
    def _():
        wb_ref[...] = w_ref[...].astype(BF16)

    conv_tile = j * n_row_tiles + i
    rm = x_ref.shape[0] // QKV_ROW_CHUNKS

    def matmul_chunk(c):
        r0 = pl.multiple_of(c * rm, rm)
        acc = jnp.dot(x_ref[pl.ds(r0, rm), :], wb_ref[...], preferred_element_type=F32)
        o_ref[pl.ds(r0, rm), :] = acc * s_ref[...]

    _conv_tile(conv_tile % tiles_per_seq != 0, a_ref, g_ref, ha_ref, hg_ref, cw_ref, cb_ref, lg_ref, lb_ref,
               conv_ref, h_ref, hs_ref, c_ref, QKV_ROW_CHUNKS, matmul_chunk)


def _qkv_proj_conv(x, w_stack, layer, n_cols, col_scale, u, seq, conv_w, conv_b, ln_g, ln_b, tm=1024, tn=768):
    m, k = x.shape
    ch = conv_w.shape[1]
    n_col_blocks, n_row_tiles = n_cols // tn, m // tm
    ts = m // (n_col_blocks * n_row_tiles)
    assert n_cols % tn == 0 and m % tm == 0 and ts % CONV_HALO == 0 and seq % ts == 0
    per = ts // CONV_HALO
    w_rows = jnp.broadcast_to(conv_w[:, None, :], (CONV_WIDTH, SUBLANES, ch))

    def tile(col):
        return pl.BlockSpec((ts, ch), lambda j, i: (j * n_row_tiles + i, col))

    def halo(col):
        return pl.BlockSpec((CONV_HALO, ch), lambda j, i: (jnp.maximum((j * n_row_tiles + i) * per - 1, 0), col))

    def const(shape):
        return pl.BlockSpec(shape, lambda j, i: (0,) * len(shape))

    return pl.pallas_call(
        functools.partial(_qkv_conv_kernel, n_row_tiles, seq // ts),
        grid=(n_col_blocks, n_row_tiles),
        in_specs=[pl.BlockSpec((tm, k), lambda j, i: (i, 0)),
                  pl.BlockSpec((None, k, tn), lambda j, i: (layer, 0, j)),
                  pl.BlockSpec((1, tn), lambda j, i: (0, j)),
                  tile(0), tile(1), halo(0), halo(1),
                  const((CONV_WIDTH, SUBLANES, ch)), const((1, ch)), const((1, ch)), const((1, ch))],
        out_specs=[pl.BlockSpec((tm, tn), lambda j, i: (i, j)),
                   pl.BlockSpec((ts, ch), lambda j, i: (j * n_row_tiles + i, 0))],
        out_shape=[jax.ShapeDtypeStruct((m, n_cols), F32), jax.ShapeDtypeStruct((m, ch), BF16)],
        scratch_shapes=[pltpu.VMEM((k, tn), BF16),
                        pltpu.VMEM((ts + CONV_HALO, ch), F32),
                        pltpu.VMEM((SUBLANES - 1, ts + CONV_HALO + SUBLANES, ch), F32),
                        pltpu.VMEM((ts, ch), F32)],
        compiler_params=_params(("parallel", "arbitrary")),
        name="qkv_proj_conv",
    )(x, w_stack, col_scale, u, u, u, u, w_rows, conv_b.reshape(1, ch), ln_g.reshape(1, ch), ln_b.reshape(1, ch))


def _gates_kernel(x_ref, w_ref, bias_ref, gc_ref, gr_ref):
    length = MLSTM_CHUNK
    w = w_ref[...].astype(BF16)
    w = jnp.concatenate([w, jnp.zeros((GATE_LANES - w.shape[0], w.shape[1]), BF16)], axis=0)
    pre = lax.dot_general(x_ref[...], w, (((1,), (1,)), ((), ())), preferred_element_type=F32) + bias_ref[...]
    log_f = jnp.minimum(pre, 0.0) - jnp.log1p(jnp.exp(-jnp.abs(pre)))
    ti = lax.broadcasted_iota(jnp.int32, (length, length), 0)
    si = lax.broadcasted_iota(jnp.int32, (length, length), 1)
    tril = (ti >= si).astype(F32)
    lane = lax.broadcasted_iota(jnp.int32, (length, GATE_LANES), 1)
    for r0 in range(0, x_ref.shape[0], length):
        cum_f = jnp.dot(tril, log_f[r0:r0 + length], preferred_element_type=F32, precision=lax.Precision.HIGHEST)
        gates = jnp.where(lane < C_HEADS, pre[r0:r0 + length], cum_f)
        gc_ref[r0:r0 + length, :] = gates
        gr_ref[:, r0:r0 + length] = gates.T


def _mlstm_gates(x, w_stack_t, layer, gate_row0, bias):
    m, d = x.shape
    length = GATE_CHUNKS_PER_STEP * MLSTM_CHUNK
    gate_rows = 2 * C_HEADS
    gate_block = gate_row0 // gate_rows
    return pl.pallas_call(
        _gates_kernel,
        grid=(m // length,),
        in_specs=[pl.BlockSpec((length, d), lambda i: (i, 0)),
                  pl.BlockSpec((None, gate_rows, d), lambda i: (layer, gate_block, 0)),
                  pl.BlockSpec((1, GATE_LANES), lambda i: (0, 0))],
        out_specs=[pl.BlockSpec((length, GATE_LANES), lambda i: (i, 0)),
                   pl.BlockSpec((GATE_LANES, length), lambda i: (0, i))],
        out_shape=[jax.ShapeDtypeStruct((m, GATE_LANES), F32),
                   jax.ShapeDtypeStruct((GATE_LANES, m), F32)],
        compiler_params=_params(("parallel",)),
        name="mlstm_gates",
    )(x, w_stack_t, bias)


def _mlstm_kernel(dk, dv, q_ref, k_ref, v_ref, o_ref, gc_ref, gr_ref, ng_ref, y_ref, c_ref, n_ref, m_ref):
    length = q_ref.shape[0]

    @pl.when(pl.program_id(1) == 0)
    def _():
        c_ref[...] = jnp.zeros_like(c_ref)
        n_ref[...] = jnp.zeros_like(n_ref)
        m_ref[...] = jnp.zeros_like(m_ref)

    ti = lax.broadcasted_iota(jnp.int32, (length, length), 0)
    si = lax.broadcasted_iota(jnp.int32, (length, length), 1)
    causal = ti >= si
    for h in range(C_HEADS):
        q = q_ref[:, h * dk:(h + 1) * dk]
        k = k_ref[:, h * dk:(h + 1) * dk]
        v = v_ref[:, h * dv:(h + 1) * dv]
        i_row = gr_ref[h:h + 1, :]
        b_row = gr_ref[C_HEADS + h:C_HEADS + h + 1, :]
        i_col = gc_ref[:, h:h + 1]
        b_col = gc_ref[:, C_HEADS + h:C_HEADS + h + 1]
        m_prev = m_ref[h:h + 1, 0:1]
        c_prev = c_ref[h]
        n_prev = n_ref[h:h + 1, :]

        dmat = jnp.where(causal, b_col - b_row + i_row, -jnp.inf)
        inter = b_col + m_prev
        m_t = jnp.maximum(inter, jnp.max(dmat, axis=-1, keepdims=True))
        w = jnp.exp(dmat - m_t)
        a = jnp.exp(inter - m_t)
        qk = lax.dot_general(q, k, (((1,), (1,)), ((), ())), preferred_element_type=F32)
        sqk = qk * w
        num = (a * jnp.dot(q, c_prev.astype(BF16), preferred_element_type=F32)
               + jnp.dot(sqk.astype(BF16), v, preferred_element_type=F32))
        den = (a * jnp.sum(q.astype(F32) * n_prev, axis=-1, keepdims=True)
               + jnp.sum(sqk, axis=-1, keepdims=True))
        hh = num / jnp.maximum(jnp.abs(den), jnp.exp(-m_t))

        b_last = b_col[length - 1:length, :]
        g = b_last - b_col + i_col
        m_new = jnp.maximum(b_last + m_prev, jnp.max(g, axis=0, keepdims=True))
        decay = jnp.exp(b_last + m_prev - m_new)
        kw = k.astype(F32) * jnp.exp(g - m_new)
        c_ref[h] = decay * c_prev + lax.dot_general(kw.astype(BF16), v, (((0,), (0,)), ((), ())),
                                                    preferred_element_type=F32)
        n_ref[h:h + 1, :] = decay * n_prev + jnp.sum(kw, axis=0, keepdims=True)
        m_ref[h:h + 1, :] = jnp.broadcast_to(m_new, (1, m_ref.shape[1]))

        mu = jnp.mean(hh, axis=-1, keepdims=True)
        hc = hh - mu
        var = jnp.mean(hc * hc, axis=-1, keepdims=True)
        hn = hc * lax.rsqrt(var + LN_EPS) * ng_ref[:, h * dv:(h + 1) * dv]
        y_ref[:, h * dv:(h + 1) * dv] = (jax.nn.sigmoid(o_ref[:, h * dv:(h + 1) * dv]) * hn).astype(y_ref.dtype)


def _mlstm(qkv, o, gates_col, gates_row, norm_g, bsz, seq, dk, dv):
    length = MLSTM_CHUNK
    nc = seq // length
    qk_w = C_HEADS * dk
    v_w = C_HEADS * dv
    v_blk = (2 * qk_w) // v_w
    return pl.pallas_call(
        functools.partial(_mlstm_kernel, dk, dv),
        grid=(bsz, nc),
        in_specs=[pl.BlockSpec((length, qk_w), lambda b, c: (b * nc + c, 0)),
                  pl.BlockSpec((length, qk_w), lambda b, c: (b * nc + c, 1)),
                  pl.BlockSpec((length, v_w), lambda b, c: (b * nc + c, v_blk)),
                  pl.BlockSpec((length, v_w), lambda b, c: (b * nc + c, 0)),
                  pl.BlockSpec((length, GATE_LANES), lambda b, c: (b * nc + c, 0)),
                  pl.BlockSpec((GATE_LANES, length), lambda b, c: (0, b * nc + c)),
                  pl.BlockSpec((1, v_w), lambda b, c: (0, 0))],
        out_specs=pl.BlockSpec((length, v_w), lambda b, c: (b * nc + c, 0)),
        out_shape=jax.ShapeDtypeStruct((bsz * seq, v_w), BF16),
        scratch_shapes=[pltpu.VMEM((C_HEADS, dk, dv), F32),
                        pltpu.VMEM((8, dk), F32),
                        pltpu.VMEM((8, GATE_LANES), F32)],
        compiler_params=_params(("parallel", "arbitrary")),
        name="mlstm_chunk",
    )(qkv, qkv, qkv, o, gates_col, gates_row, norm_g)


def _even_mixer(xb, x, w_in, j, conv_w, conv_b, conv_ln_g, conv_ln_b, w_out, ln_g, ln_b, alpha, bsz, seq):
    d = x.shape[1]
    a_width = d // 2
    n_heads = a_width // A_HEAD_DIM
    in_width = w_in.shape[2]
    qkv_w = 3 * a_width
    qkv_scale = jnp.ones((1, qkv_w), F32).at[:, :a_width].set(A_HEAD_DIM ** -0.5)
    u = _matmul(xb, w_in, j, qkv_w, in_width - qkv_w, jnp.ones((1, in_width - qkv_w), F32), F32)
    qkv, conv = _qkv_proj_conv(xb, w_in, j, qkv_w, qkv_scale, u, seq, conv_w, conv_b, conv_ln_g, conv_ln_b)
    att = _dilated_mixture_attention(qkv, bsz, seq, n_heads)
    return _out_proj_ln([att, conv], w_out, j, x, ln_g.reshape(1, d), ln_b.reshape(1, d), alpha)


def _odd_mixer(xb, x, w_in_t, j, igate_b, fgate_b, norm_g, w_out, ln_g, ln_b, alpha, bsz, seq):
    d = x.shape[1]
    v_w = d
    dv = v_w // C_HEADS
    dk = dv // 2
    qk_w = C_HEADS * dk
    qkv_w = 2 * qk_w + v_w
    qkv_scale = jnp.ones((1, qkv_w), F32).at[:, :qk_w].set(dk ** -0.5)
    qkv = _matmul(xb, w_in_t, j, 0, qkv_w, qkv_scale, BF16, w_is_transposed=True)
    o = _matmul(xb, w_in_t, j, qkv_w, v_w, jnp.ones((1, v_w), F32), F32, w_is_transposed=True)
    bias = jnp.zeros((1, GATE_LANES), F32).at[0, :C_HEADS].set(igate_b).at[0, C_HEADS:2 * C_HEADS].set(fgate_b)
    gates_col, gates_row = _mlstm_gates(xb, w_in_t, j, qkv_w + v_w, bias)
    y = _mlstm(qkv, o, gates_col, gates_row, norm_g.reshape(1, v_w), bsz, seq, dk, dv)
    return _out_proj_ln([y], w_out, j, x, ln_g.reshape(1, d), ln_b.reshape(1, d), alpha)


def kernel(x, even_w_in, even_conv_w, even_conv_b, even_conv_ln_g, even_conv_ln_b, even_w_out,
           odd_w_in, odd_igate_b, odd_fgate_b, odd_norm_g, odd_w_out, mix_ln_g, mix_ln_b,
           ffn_w1, ffn_w2, ffn_ln_g, ffn_ln_b):
    bsz, seq, d = x.shape
    depth = mix_ln_g.shape[0]
    alpha = (2 * depth) ** 0.25
    xf = x.reshape(bsz * seq, d)
    xb = xf.astype(BF16)
    odd_w_in_t = jnp.swapaxes(odd_w_in, 1, 2)
    for layer in range(depth):
        j = layer // 2
        if layer % 2 == 0:
            xf, xb = _even_mixer(xb, xf, even_w_in, j, even_conv_w[j], even_conv_b[j], even_conv_ln_g[j],
                                 even_conv_ln_b[j], even_w_out, mix_ln_g[layer], mix_ln_b[layer],
                                 alpha, bsz, seq)
        else:
            xf, xb = _odd_mixer(xb, xf, odd_w_in_t, j, odd_igate_b[j], odd_fgate_b[j], odd_norm_g[j],
                                odd_w_out, mix_ln_g[layer], mix_ln_b[layer], alpha, bsz, seq)
        xf, xb = _ffn_ln(xf, ffn_w1, ffn_w2, layer,
                         ffn_ln_g[layer].reshape(1, d), ffn_ln_b[layer].reshape(1, d), alpha)
    return xf.reshape(bsz, seq, d)
```

```python
import functools

import jax
import jax.numpy as jnp
from jax import lax
from jax.experimental import pallas as pl
from jax.experimental.pallas import tpu as pltpu

LN_EPS = 1e-5
DILATED_BRANCHES = ((128, 1), (512, 4), (2048, 16))
BAND_BLOCK = 128
A_HEAD_DIM = 128
ATT_TILE = 2048
ATT_UNROLL = 16
CONV_WIDTH = 31
CONV_HALO = 32
C_HEADS = 4
MLSTM_CHUNK = 256
GATE_LANES = 128
GATE_CHUNKS_PER_STEP = 4
SUBLANES = 8
FFN_DMA_CHUNKS = 4
QKV_ROW_CHUNKS = 2
V7X_VMEM_LIMIT = 56 * 1024 * 1024
V7X_VMEM_LIMIT_FFN = 58 * 1024 * 1024

BF16 = jnp.bfloat16
F32 = jnp.float32


def _params(semantics, vmem_limit=V7X_VMEM_LIMIT):
    return pltpu.CompilerParams(dimension_semantics=semantics, vmem_limit_bytes=vmem_limit)


def _layer_norm(z, g, b):
    mu = jnp.mean(z, axis=-1, keepdims=True)
    zc = z - mu
    var = jnp.mean(zc * zc, axis=-1, keepdims=True)
    return zc * lax.rsqrt(var + LN_EPS) * g + b


def _matmul_kernel(w_is_transposed, x_ref, w_ref, s_ref, o_ref, wb_ref):
    @pl.when(pl.program_id(1) == 0)
    def _():
        w = w_ref[...]
        wb_ref[...] = (w.T if w_is_transposed else w).astype(BF16)

    acc = jnp.dot(x_ref[...], wb_ref[...], preferred_element_type=F32)
    o_ref[...] = (acc * s_ref[...]).astype(o_ref.dtype)


def _matmul(x, w_stack, layer, col0, n_cols, col_scale, out_dtype, w_is_transposed=False, tm=1024, tn=1024):
    m, k = x.shape
    cb0 = col0 // tn
    if w_is_transposed:
        w_spec = pl.BlockSpec((None, tn, k), lambda j, i: (layer, cb0 + j, 0))
    else:
        w_spec = pl.BlockSpec((None, k, tn), lambda j, i: (layer, 0, cb0 + j))
    return pl.pallas_call(
        functools.partial(_matmul_kernel, w_is_transposed),
        grid=(n_cols // tn, m // tm),
        in_specs=[pl.BlockSpec((tm, k), lambda j, i: (i, 0)),
                  w_spec,
                  pl.BlockSpec((1, tn), lambda j, i: (0, j))],
        out_specs=pl.BlockSpec((tm, tn), lambda j, i: (i, j)),
        out_shape=jax.ShapeDtypeStruct((m, n_cols), out_dtype),
        scratch_shapes=[pltpu.VMEM((k, tn), BF16)],
        compiler_params=_params(("parallel", "arbitrary")),
        name="proj_matmul",
    )(x, w_stack, col_scale)


def _out_proj_ln_kernel(alpha, n_in, *refs):
    ys = refs[:n_in]
    w_ref, x_ref, g_ref, b_ref, of_ref, ob_ref, wb_ref = refs[n_in:]

    @pl.when(pl.program_id(0) == 0)
    def _():
        wb_ref[...] = w_ref[...].astype(BF16)

    acc = alpha * x_ref[...]
    k0 = 0
    for y_ref in ys:
        kw = y_ref.shape[1]
        acc = acc + jnp.dot(y_ref[...], wb_ref[k0:k0 + kw, :], preferred_element_type=F32)
        k0 += kw
    out = _layer_norm(acc, g_ref[...], b_ref[...])
    of_ref[...] = out
    ob_ref[...] = out.astype(BF16)


def _out_proj_ln(ys, w_stack, layer, x, g, b, alpha, tm=512):
    m, d = x.shape
    n_in = len(ys)
    k = w_stack.shape[1]
    in_specs = ([pl.BlockSpec((tm, y.shape[1]), lambda i: (i, 0)) for y in ys]
                + [pl.BlockSpec((None, k, d), lambda i: (layer, 0, 0), pipeline_mode=pl.Buffered(1)),
                   pl.BlockSpec((tm, d), lambda i: (i, 0)),
                   pl.BlockSpec((1, d), lambda i: (0, 0)),
                   pl.BlockSpec((1, d), lambda i: (0, 0))])
    return pl.pallas_call(
        functools.partial(_out_proj_ln_kernel, alpha, n_in),
        grid=(m // tm,),
        in_specs=in_specs,
        out_specs=[pl.BlockSpec((tm, d), lambda i: (i, 0)),
                   pl.BlockSpec((tm, d), lambda i: (i, 0))],
        out_shape=[jax.ShapeDtypeStruct((m, d), F32), jax.ShapeDtypeStruct((m, d), BF16)],
        scratch_shapes=[pltpu.VMEM((k, d), BF16)],
        compiler_params=_params(("arbitrary",)),
        name="out_proj_ln",
    )(*ys, w_stack, x, g, b)


def _ffn_kernel(alpha, layer, tm, th, n_blocks, x_hbm, w1_hbm, w2_hbm, g_ref, b_ref, of_hbm, ob_hbm,
                x_buf, xb_ref, acc_ref, of_buf, ob_buf, w1_buf, w2_buf, sem_x, sem_w, sem_o):
    i = pl.program_id(0)
    last_tile = pl.num_programs(0) - 1
    chunks = FFN_DMA_CHUNKS

    def w_copies(j, slot):
        col = pl.multiple_of(j * th, th)
        r1 = w1_buf.shape[1] // chunks
        r2 = th // chunks
        copies = []
        for c in range(chunks):
            copies.append(pltpu.make_async_copy(
                w1_hbm.at[layer, pl.ds(c * r1, r1), pl.ds(col, th)],
                w1_buf.at[slot, pl.ds(c * r1, r1), :], sem_w.at[0, slot, c]))
            copies.append(pltpu.make_async_copy(
                w2_hbm.at[layer, pl.ds(col + c * r2, r2), :],
                w2_buf.at[slot, pl.ds(c * r2, r2), :], sem_w.at[1, slot, c]))
        return copies

    def x_copies(tile):
        rows = tm // chunks
        row0 = pl.multiple_of(tile * tm, tm)
        return [pltpu.make_async_copy(x_hbm.at[pl.ds(row0 + c * rows, rows), :],
                                      x_buf.at[pl.ds(c * rows, rows), :], sem_x.at[c]) for c in range(chunks)]

    def out_copies(tile):
        rows = tm // chunks
        row0 = pl.multiple_of(tile * tm, tm)
        copies = []
        for c in range(chunks):
            copies.append(pltpu.make_async_copy(of_buf.at[pl.ds(c * rows, rows), :],
                                                of_hbm.at[pl.ds(row0 + c * rows, rows), :], sem_o.at[0, c]))
            copies.append(pltpu.make_async_copy(ob_buf.at[pl.ds(c * rows, rows), :],
                                                ob_hbm.at[pl.ds(row0 + c * rows, rows), :], sem_o.at[1, c]))
        return copies

    def start(copies):
        for copy in copies:
            copy.start()

    def wait(copies):
        for copy in copies:
            copy.wait()

    @pl.when(i == 0)
    def _():
        start(x_copies(0))
        start(w_copies(0, 0))

    wait(x_copies(i))
    x = x_buf[...]
    xb_ref[...] = x.astype(BF16)
    acc_ref[...] = alpha * x

    @pl.when(i < last_tile)
    def _():
        start(x_copies(i + 1))

    def block(j, slot):
        wait(w_copies(j, slot))
        h = jnp.dot(xb_ref[...], w1_buf[slot].astype(BF16), preferred_element_type=F32)
        h = jnp.maximum(h, 0.0)
        h = (h * h).astype(BF16)
        acc_ref[...] += jnp.dot(h, w2_buf[slot].astype(BF16), preferred_element_type=F32)

    def pair(p, carry):
        j = 2 * p
        start(w_copies(j + 1, 1))
        block(j, 0)

        @pl.when(j + 2 < n_blocks)
        def _():
            start(w_copies(j + 2, 0))

        @pl.when(jnp.logical_and(j + 2 == n_blocks, i < last_tile))
        def _():
            start(w_copies(0, 0))

        block(j + 1, 1)
        return carry

    lax.fori_loop(0, n_blocks // 2, pair, 0)

    @pl.when(i > 0)
    def _():
        wait(out_copies(i - 1))

    out = _layer_norm(acc_ref[...], g_ref[...], b_ref[...])
    of_buf[...] = out
    ob_buf[...] = out.astype(BF16)
    start(out_copies(i))

    @pl.when(i == last_tile)
    def _():
        wait(out_copies(i))


def _ffn_ln(x, w1, w2, layer, g, b, alpha, tm=1024, th=512):
    m, d = x.shape
    f = w1.shape[2]
    n_blocks = f // th
    assert n_blocks % 2 == 0 and m % tm == 0
    hbm = pl.BlockSpec(memory_space=pl.ANY)
    return pl.pallas_call(
        functools.partial(_ffn_kernel, alpha, layer, tm, th, n_blocks),
        grid=(m // tm,),
        in_specs=[hbm, hbm, hbm,
                  pl.BlockSpec((1, d), lambda i: (0, 0)),
                  pl.BlockSpec((1, d), lambda i: (0, 0))],
        out_specs=[hbm, hbm],
        out_shape=[jax.ShapeDtypeStruct((m, d), F32), jax.ShapeDtypeStruct((m, d), BF16)],
        scratch_shapes=[pltpu.VMEM((tm, d), F32),
                        pltpu.VMEM((tm, d), BF16),
                        pltpu.VMEM((tm, d), F32),
                        pltpu.VMEM((tm, d), F32),
                        pltpu.VMEM((tm, d), BF16),
                        pltpu.VMEM((2, d, th), F32),
                        pltpu.VMEM((2, th, d), F32),
                        pltpu.SemaphoreType.DMA((FFN_DMA_CHUNKS,)),
                        pltpu.SemaphoreType.DMA((2, 2, FFN_DMA_CHUNKS)),
                        pltpu.SemaphoreType.DMA((2, FFN_DMA_CHUNKS))],
        compiler_params=_params(("arbitrary",), V7X_VMEM_LIMIT_FFN),
        name="ffn_ln",
    )(x, w1, w2, g, b)


def _kv_rows(dil):
    return dil * BAND_BLOCK + ATT_TILE


def _attn_kernel(q_ref, k_ref, v_ref, slope_ref, out_ref,
                 f_ref, g4_ref, qs_ref, ks_ref, vs_ref, o_ref, l_ref, bias_ref, s_ref, e_ref):
    tile = pl.program_id(2)
    p = BAND_BLOCK
    t_len = ATT_TILE
    qi = lax.broadcasted_iota(jnp.int32, (p, 2 * p), 0)
    kj = lax.broadcasted_iota(jnp.int32, (p, 2 * p), 1)
    dist = p + qi - kj
    valid = jnp.logical_and(dist >= 0, dist <= p)
    in_prev_block = kj < p
    slope = slope_ref[0]
    for bi, (_, dil) in enumerate(DILATED_BRANCHES):
        bias_ref[bi] = jnp.where(valid, -(slope * (dist * dil).astype(F32)), -jnp.inf)

    def rows(start, size, dil):
        return pl.ds(start, size, stride=dil) if dil > 1 else pl.ds(start, size)

    kv_base = [sum(_kv_rows(d) for _, d in DILATED_BRANCHES[:bi]) for bi in range(len(DILATED_BRANCHES))]
    srcs = (q_ref, k_ref, v_ref)
    dil4 = DILATED_BRANCHES[1][1]
    w4 = t_len // dil4
    for x, src in enumerate(srcs):
        f_ref[x] = src[...].astype(F32)
        for r in range(dil4):
            g4_ref[x, r * w4:(r + 1) * w4, :] = f_ref[x, rows(r, w4, dil4), :]

    def residue_rows(x, r, dil):
        if dil == 1:
            return srcs[x][...]
        if dil == dil4:
            return g4_ref[x, r * w4:(r + 1) * w4, :]
        return g4_ref[x, rows((r % dil4) * w4 + r // dil4, t_len // dil, dil // dil4), :]

    nt = (((1,), (1,)), ((), ()))
    for bi, (_, dil) in enumerate(DILATED_BRANCHES):
        wq = t_len // dil
        nq = wq // p
        wk = p + wq
        base = kv_base[bi]

        @pl.when(tile == 0)
        def _(dil=dil, wk=wk, base=base):
            for r in range(dil):
                for dst_ref in (ks_ref, vs_ref):
                    dst_ref[base + r * wk:base + r * wk + p, :] = jnp.zeros((p, A_HEAD_DIM), BF16)

        @pl.when(tile > 0)
        def _(dil=dil, wk=wk, wq=wq, base=base):
            for r in range(dil):
                for dst_ref in (ks_ref, vs_ref):
                    dst_ref[base + r * wk:base + r * wk + p, :] = dst_ref[base + r * wk + wq:base + (r + 1) * wk, :]

        for r in range(dil):
            qs_ref[r * wq:(r + 1) * wq, :] = residue_rows(0, r, dil).astype(BF16)
            ks_ref[base + r * wk + p:base + (r + 1) * wk, :] = residue_rows(1, r, dil).astype(BF16)
            vs_ref[base + r * wk + p:base + (r + 1) * wk, :] = residue_rows(2, r, dil).astype(BF16)

        def group(g, carry, dil=dil, nq=nq, wk=wk, bi=bi, base=base):
            starts = []
            for u in range(ATT_UNROLL):
                idx = g * ATT_UNROLL + u
                r = idx // nq
                n = idx - r * nq
                q0 = pl.multiple_of(idx * p, p)
                k0 = pl.multiple_of(base + r * wk + n * p, p)
                s = lax.dot_general(qs_ref[pl.ds(q0, p), :], ks_ref[pl.ds(k0, 2 * p), :], nt,
                                    preferred_element_type=F32) + bias_ref[bi]
                no_history = jnp.logical_and(tile == 0, n == 0)
                s_ref[u * p:(u + 1) * p, :] = jnp.where(jnp.logical_and(no_history, in_prev_block), -jnp.inf, s)
                starts.append((k0, r + dil * p * n))
            s = s_ref[...]
            m = jnp.max(s, axis=-1, keepdims=True)
            e = jnp.exp(s - m)
            l = jnp.sum(e, axis=-1, keepdims=True)
            e_ref[...] = e.astype(BF16)
            inv_l = 1.0 / l
            lse = m + jnp.log(l)
            for u, (k0, t0) in enumerate(starts):
                o = jnp.dot(e_ref[u * p:(u + 1) * p, :], vs_ref[pl.ds(k0, 2 * p), :], preferred_element_type=F32)
                o_ref[bi, rows(t0, p, dil), :] = o * inv_l[u * p:(u + 1) * p]
                l_ref[bi, rows(t0, p, dil), :] = jnp.broadcast_to(lse[u * p:(u + 1) * p], (p, A_HEAD_DIM))
            return carry

        lax.fori_loop(0, (dil * nq) // ATT_UNROLL, group, 0)

    chunk = 256
    for c0 in range(0, t_len, chunk):
        sl = slice(c0, c0 + chunk)
        l1, l2, l3 = l_ref[0, sl, :], l_ref[1, sl, :], l_ref[2, sl, :]
        mx = jnp.maximum(jnp.maximum(l1, l2), l3)
        e1, e2, e3 = jnp.exp(l1 - mx), jnp.exp(l2 - mx), jnp.exp(l3 - mx)
        mix = (e1 * o_ref[0, sl, :] + e2 * o_ref[1, sl, :] + e3 * o_ref[2, sl, :]) / (e1 + e2 + e3)
        out_ref[sl, :] = mix.astype(out_ref.dtype)


def _dilated_mixture_attention(proj, bsz, seq, n_heads):
    m = proj.shape[0]
    hd = A_HEAD_DIM
    t_len = ATT_TILE
    nt = seq // t_len
    slopes = 2.0 ** (-8.0 * jnp.arange(1, n_heads + 1, dtype=F32) / n_heads)
    slopes = jnp.broadcast_to(slopes[:, None, None], (n_heads, 1, 2 * BAND_BLOCK))

    kv_rows = sum(_kv_rows(dil) for _, dil in DILATED_BRANCHES)

    def head_block(col):
        return pl.BlockSpec((t_len, hd), lambda b, h, t: (b * nt + t, col * n_heads + h))

    return pl.pallas_call(
        _attn_kernel,
        grid=(bsz, n_heads, nt),
        in_specs=[head_block(0), head_block(1), head_block(2),
                  pl.BlockSpec((1, 1, 2 * BAND_BLOCK), lambda b, h, t: (h, 0, 0))],
        out_specs=pl.BlockSpec((t_len, hd), lambda b, h, t: (b * nt + t, h)),
        out_shape=jax.ShapeDtypeStruct((m, n_heads * hd), BF16),
        scratch_shapes=[pltpu.VMEM((3, t_len, hd), F32),
                        pltpu.VMEM((3, t_len, hd), F32),
                        pltpu.VMEM((t_len, hd), BF16),
                        pltpu.VMEM((kv_rows, hd), BF16),
                        pltpu.VMEM((kv_rows, hd), BF16),
                        pltpu.VMEM((3, t_len, hd), F32),
                        pltpu.VMEM((3, t_len, hd), F32),
                        pltpu.VMEM((3, BAND_BLOCK, 2 * BAND_BLOCK), F32),
                        pltpu.VMEM((ATT_UNROLL * BAND_BLOCK, 2 * BAND_BLOCK), F32),
                        pltpu.VMEM((ATT_UNROLL * BAND_BLOCK, 2 * BAND_BLOCK), BF16)],
        compiler_params=_params(("parallel", "parallel", "arbitrary")),
        name="dilated_attention",
    )(proj, proj, proj, slopes)


def _conv_tile(has_history, a_ref, g_ref, ha_ref, hg_ref, w_ref, cb_ref, lg_ref, lb_ref, o_ref, h_ref, hs_ref, c_ref,
               n_parts, side_work):
    ts, ch = a_ref.shape
    hist = ha_ref[...] * jax.nn.sigmoid(hg_ref[...])
    h_ref[0:CONV_HALO, :] = jnp.where(has_history, hist, 0.0)
    h_ref[CONV_HALO:, :] = a_ref[...] * jax.nn.sigmoid(g_ref[...])
    first = CONV_HALO - (CONV_WIDTH - 1)
    h_rows = h_ref.shape[0]
    for s in range(1, SUBLANES):
        hs_ref[s - 1, SUBLANES - s:SUBLANES - s + h_rows, :] = h_ref[...]
    rows, lanes = 32, 512
    groups = rows // SUBLANES
    part_rows = ts // n_parts

    def part(p, carry):
        base = pl.multiple_of(p * part_rows, part_rows)
        for r0 in range(0, part_rows, rows):
            for c0 in range(0, ch, lanes):
                acc = jnp.broadcast_to(cb_ref[:, c0:c0 + lanes], (groups, SUBLANES, lanes))
                for j in range(CONV_WIDTH):
                    a8, s = (first + j) // SUBLANES * SUBLANES, (first + j) % SUBLANES
                    if s == 0:
                        tap = h_ref[pl.ds(base + r0 + a8, rows), c0:c0 + lanes]
                    else:
                        tap = hs_ref[s - 1, pl.ds(base + SUBLANES + r0 + a8, rows), c0:c0 + lanes]
                    acc = acc + w_ref[j, :, c0:c0 + lanes] * tap.reshape(groups, SUBLANES, lanes)
                c_ref[pl.ds(base + r0, rows), c0:c0 + lanes] = acc.reshape(rows, lanes)
        side_work(p)
        return carry

    lax.fori_loop(0, n_parts, part, 0)
    y = _layer_norm(c_ref[...], lg_ref[...], lb_ref[...])
    o_ref[...] = (y * jax.nn.sigmoid(y)).astype(o_ref.dtype)


def _qkv_conv_kernel(n_row_tiles, tiles_per_seq, x_ref, w_ref, s_ref, a_ref, g_ref, ha_ref, hg_ref, cw_ref, cb_ref,
                     lg_ref, lb_ref, o_ref, conv_ref, wb_ref, h_ref, hs_ref, c_ref):
    j, i = pl.program_id(0), pl.program_id(1)

    @pl.when(i == 0)
    def _():
        wb_ref[...] = w_ref[...].astype(BF16)

    conv_tile = j * n_row_tiles + i
    rm = x_ref.shape[0] // QKV_ROW_CHUNKS

    def matmul_chunk(c):
        r0 = pl.multiple_of(c * rm, rm)
        acc = jnp.dot(x_ref[pl.ds(r0, rm), :], wb_ref[...], preferred_element_type=F32)
        o_ref[pl.ds(r0, rm), :] = (acc * s_ref[...]).astype(o_ref.dtype)

    _conv_tile(conv_tile % tiles_per_seq != 0, a_ref, g_ref, ha_ref, hg_ref, cw_ref, cb_ref, lg_ref, lb_ref,
               conv_ref, h_ref, hs_ref, c_ref, QKV_ROW_CHUNKS, matmul_chunk)


def _qkv_proj_conv(x, w_stack, layer, n_cols, col_scale, u, seq, conv_w, conv_b, ln_g, ln_b, tm=1024, tn=768):
    m, k = x.shape
    ch = conv_w.shape[1]
    n_col_blocks, n_row_tiles = n_cols // tn, m // tm
    ts = m // (n_col_blocks * n_row_tiles)
    assert n_cols % tn == 0 and m % tm == 0 and ts % CONV_HALO == 0 and seq % ts == 0
    per = ts // CONV_HALO
    w_rows = jnp.broadcast_to(conv_w[:, None, :], (CONV_WIDTH, SUBLANES, ch))

    def tile(col):
        return pl.BlockSpec((ts, ch), lambda j, i: (j * n_row_tiles + i, col))

    def halo(col):
        return pl.BlockSpec((CONV_HALO, ch), lambda j, i: (jnp.maximum((j * n_row_tiles + i) * per - 1, 0), col))

    def const(shape):
        return pl.BlockSpec(shape, lambda j, i: (0,) * len(shape))

    return pl.pallas_call(
        functools.partial(_qkv_conv_kernel, n_row_tiles, seq // ts),
        grid=(n_col_blocks, n_row_tiles),
        in_specs=[pl.BlockSpec((tm, k), lambda j, i: (i, 0)),
                  pl.BlockSpec((None, k, tn), lambda j, i: (layer, 0, j)),
                  pl.BlockSpec((1, tn), lambda j, i: (0, j)),
                  tile(0), tile(1), halo(0), halo(1),
                  const((CONV_WIDTH, SUBLANES, ch)), const((1, ch)), const((1, ch)), const((1, ch))],
        out_specs=[pl.BlockSpec((tm, tn), lambda j, i: (i, j)),
                   pl.BlockSpec((ts, ch), lambda j, i: (j * n_row_tiles + i, 0))],
        out_shape=[jax.ShapeDtypeStruct((m, n_cols), BF16), jax.ShapeDtypeStruct((m, ch), BF16)],
        scratch_shapes=[pltpu.VMEM((k, tn), BF16),
                        pltpu.VMEM((ts + CONV_HALO, ch), F32),
                        pltpu.VMEM((SUBLANES - 1, ts + CONV_HALO + SUBLANES, ch), F32),
                        pltpu.VMEM((ts, ch), F32)],
        compiler_params=_params(("parallel", "arbitrary")),
        name="qkv_proj_conv",
    )(x, w_stack, col_scale, u, u, u, u, w_rows, conv_b.reshape(1, ch), ln_g.reshape(1, ch), ln_b.reshape(1, ch))


def _gates_kernel(x_ref, w_ref, bias_ref, gc_ref, gr_ref):
    length = MLSTM_CHUNK
    w = w_ref[...].astype(BF16)
    w = jnp.concatenate([w, jnp.zeros((GATE_LANES - w.shape[0], w.shape[1]), BF16)], axis=0)
    pre = lax.dot_general(x_ref[...], w, (((1,), (1,)), ((), ())), preferred_element_type=F32) + bias_ref[...]
    log_f = jnp.minimum(pre, 0.0) - jnp.log1p(jnp.exp(-jnp.abs(pre)))
    ti = lax.broadcasted_iota(jnp.int32, (length, length), 0)
    si = lax.broadcasted_iota(jnp.int32, (length, length), 1)
    tril = (ti >= si).astype(F32)
    lane = lax.broadcasted_iota(jnp.int32, (length, GATE_LANES), 1)
    for r0 in range(0, x_ref.shape[0], length):
        cum_f = jnp.dot(tril, log_f[r0:r0 + length], preferred_element_type=F32, precision=lax.Precision.HIGHEST)
        gates = jnp.where(lane < C_HEADS, pre[r0:r0 + length], cum_f)
        gc_ref[r0:r0 + length, :] = gates
        gr_ref[:, r0:r0 + length] = gates.T


def _mlstm_gates(x, w_stack_t, layer, gate_row0, bias):
    m, d = x.shape
    length = GATE_CHUNKS_PER_STEP * MLSTM_CHUNK
    gate_rows = 2 * C_HEADS
    gate_block = gate_row0 // gate_rows
    return pl.pallas_call(
        _gates_kernel,
        grid=(m // length,),
        in_specs=[pl.BlockSpec((length, d), lambda i: (i, 0)),
                  pl.BlockSpec((None, gate_rows, d), lambda i: (layer, gate_block, 0)),
                  pl.BlockSpec((1, GATE_LANES), lambda i: (0, 0))],
        out_specs=[pl.BlockSpec((length, GATE_LANES), lambda i: (i, 0)),
                   pl.BlockSpec((GATE_LANES, length), lambda i: (0, i))],
        out_shape=[jax.ShapeDtypeStruct((m, GATE_LANES), F32),
                   jax.ShapeDtypeStruct((GATE_LANES, m), F32)],
        compiler_params=_params(("parallel",)),
        name="mlstm_gates",
    )(x, w_stack_t, bias)


def _mlstm_kernel(dk, dv, q_ref, k_ref, v_ref, o_ref, gc_ref, gr_ref, ng_ref, y_ref, c_ref, n_ref, m_ref):
    length = q_ref.shape[0]

    @pl.when(pl.program_id(1) == 0)
    def _():
        c_ref[...] = jnp.zeros_like(c_ref)
        n_ref[...] = jnp.zeros_like(n_ref)
        m_ref[...] = jnp.zeros_like(m_ref)

    ti = lax.broadcasted_iota(jnp.int32, (length, length), 0)
    si = lax.broadcasted_iota(jnp.int32, (length, length), 1)
    causal = ti >= si
    for h in range(C_HEADS):
        q = q_ref[:, h * dk:(h + 1) * dk]
        k = k_ref[:, h * dk:(h + 1) * dk]
        v = v_ref[:, h * dv:(h + 1) * dv]
        i_row = gr_ref[h:h + 1, :]
        b_row = gr_ref[C_HEADS + h:C_HEADS + h + 1, :]
        i_col = gc_ref[:, h:h + 1]
        b_col = gc_ref[:, C_HEADS + h:C_HEADS + h + 1]
        m_prev = m_ref[h:h + 1, 0:1]
        c_prev = c_ref[h]
        n_prev = n_ref[h:h + 1, :]

        dmat = jnp.where(causal, b_col - b_row + i_row, -jnp.inf)
        inter = b_col + m_prev
        m_t = jnp.maximum(inter, jnp.max(dmat, axis=-1, keepdims=True))
        w = jnp.exp(dmat - m_t)
        a = jnp.exp(inter - m_t)
        qk = lax.dot_general(q, k, (((1,), (1,)), ((), ())), preferred_element_type=F32)
        sqk = qk * w
        num = (a * jnp.dot(q, c_prev.astype(BF16), preferred_element_type=F32)
               + jnp.dot(sqk.astype(BF16), v, preferred_element_type=F32))
        den = (a * jnp.sum(q.astype(F32) * n_prev, axis=-1, keepdims=True)
               + jnp.sum(sqk, axis=-1, keepdims=True))
        hh = num / jnp.maximum(jnp.abs(den), jnp.exp(-m_t))

        b_last = b_col[length - 1:length, :]
        g = b_last - b_col + i_col
        m_new = jnp.maximum(b_last + m_prev, jnp.max(g, axis=0, keepdims=True))
        decay = jnp.exp(b_last + m_prev - m_new)
        kw = k.astype(F32) * jnp.exp(g - m_new)
        c_ref[h] = decay * c_prev + lax.dot_general(kw.astype(BF16), v, (((0,), (0,)), ((), ())),
                                                    preferred_element_type=F32)
        n_ref[h:h + 1, :] = decay * n_prev + jnp.sum(kw, axis=0, keepdims=True)
        m_ref[h:h + 1, :] = jnp.broadcast_to(m_new, (1, m_ref.shape[1]))

        mu = jnp.mean(hh, axis=-1, keepdims=True)
        hc = hh - mu
        var = jnp.mean(hc * hc, axis=-1, keepdims=True)
        hn = hc * lax.rsqrt(var + LN_EPS) * ng_ref[:, h * dv:(h + 1) * dv]
        y_ref[:, h * dv:(h + 1) * dv] = (jax.nn.sigmoid(o_ref[:, h * dv:(h + 1) * dv]) * hn).astype(y_ref.dtype)


def _mlstm(qkv, o, gates_col, gates_row, norm_g, bsz, seq, dk, dv):
    length = MLSTM_CHUNK
    nc = seq // length
    qk_w = C_HEADS * dk
    v_w = C_HEADS * dv
    v_blk = (2 * qk_w) // v_w
    return pl.pallas_call(
        functools.partial(_mlstm_kernel, dk, dv),
        grid=(bsz, nc),
        in_specs=[pl.BlockSpec((length, qk_w), lambda b, c: (b * nc + c, 0)),
                  pl.BlockSpec((length, qk_w), lambda b, c: (b * nc + c, 1)),
                  pl.BlockSpec((length, v_w), lambda b, c: (b * nc + c, v_blk)),
                  pl.BlockSpec((length, v_w), lambda b, c: (b * nc + c, 0)),
                  pl.BlockSpec((length, GATE_LANES), lambda b, c: (b * nc + c, 0)),
                  pl.BlockSpec((GATE_LANES, length), lambda b, c: (0, b * nc + c)),
                  pl.BlockSpec((1, v_w), lambda b, c: (0, 0))],
        out_specs=pl.BlockSpec((length, v_w), lambda b, c: (b * nc + c, 0)),
        out_shape=jax.ShapeDtypeStruct((bsz * seq, v_w), BF16),
        scratch_shapes=[pltpu.VMEM((C_HEADS, dk, dv), F32),
                        pltpu.VMEM((8, dk), F32),
                        pltpu.VMEM((8, GATE_LANES), F32)],
        compiler_params=_params(("parallel", "arbitrary")),
        name="mlstm_chunk",
    )(qkv, qkv, qkv, o, gates_col, gates_row, norm_g)


def _even_mixer(xb, x, w_in, j, conv_w, conv_b, conv_ln_g, conv_ln_b, w_out, ln_g, ln_b, alpha, bsz, seq):
    d = x.shape[1]
    a_width = d // 2
    n_heads = a_width // A_HEAD_DIM
    in_width = w_in.shape[2]
    qkv_w = 3 * a_width
    qkv_scale = jnp.ones((1, qkv_w), F32).at[:, :a_width].set(A_HEAD_DIM ** -0.5)
    u = _matmul(xb, w_in, j, qkv_w, in_width - qkv_w, jnp.ones((1, in_width - qkv_w), F32), F32)
    qkv, conv = _qkv_proj_conv(xb, w_in, j, qkv_w, qkv_scale, u, seq, conv_w, conv_b, conv_ln_g, conv_ln_b)
    att = _dilated_mixture_attention(qkv, bsz, seq, n_heads)
    return _out_proj_ln([att, conv], w_out, j, x, ln_g.reshape(1, d), ln_b.reshape(1, d), alpha)


def _odd_mixer(xb, x, w_in_t, j, igate_b, fgate_b, norm_g, w_out, ln_g, ln_b, alpha, bsz, seq):
    d = x.shape[1]
    v_w = d
    dv = v_w // C_HEADS
    dk = dv // 2
    qk_w = C_HEADS * dk
    qkv_w = 2 * qk_w + v_w
    qkv_scale = jnp.ones((1, qkv_w), F32).at[:, :qk_w].set(dk ** -0.5)
    qkv = _matmul(xb, w_in_t, j, 0, qkv_w, qkv_scale, BF16, w_is_transposed=True)
    o = _matmul(xb, w_in_t, j, qkv_w, v_w, jnp.ones((1, v_w), F32), F32, w_is_transposed=True)
    bias = jnp.zeros((1, GATE_LANES), F32).at[0, :C_HEADS].set(igate_b).at[0, C_HEADS:2 * C_HEADS].set(fgate_b)
    gates_col, gates_row = _mlstm_gates(xb, w_in_t, j, qkv_w + v_w, bias)
    y = _mlstm(qkv, o, gates_col, gates_row, norm_g.reshape(1, v_w), bsz, seq, dk, dv)
    return _out_proj_ln([y], w_out, j, x, ln_g.reshape(1, d), ln_b.reshape(1, d), alpha)


def kernel(x, even_w_in, even_conv_w, even_conv_b, even_conv_ln_g, even_conv_ln_b, even_w_out,
           odd_w_in, odd_igate_b, odd_fgate_b, odd_norm_g, odd_w_out, mix_ln_g, mix_ln_b,
           ffn_w1, ffn_w2, ffn_ln_g, ffn_ln_b):
    bsz, seq, d = x.shape
    depth = mix_ln_g.shape[0]
    alpha = (2 * depth) ** 0.25
    xf = x.reshape(bsz * seq, d)
    xb = xf.astype(BF16)
    odd_w_in_t = jnp.swapaxes(odd_w_in, 1, 2)
    for layer in range(depth):
        j = layer // 2
        if layer % 2 == 0:
            xf, xb = _even_mixer(xb, xf, even_w_in, j, even_conv_w[j], even_conv_b[j], even_conv_ln_g[j],
                                 even_conv_ln_b[j], even_w_out, mix_ln_g[layer], mix_ln_b[layer],
                                 alpha, bsz, seq)
        else:
            xf, xb = _odd_mixer(xb, xf, odd_w_in_t, j, odd_igate_b[j], odd_fgate_b[j], odd_norm_g[j],
                                odd_w_out, mix_ln_g[layer], mix_ln_b[layer], alpha, bsz, seq)
        xf, xb = _ffn_ln(xf, ffn_w1, ffn_w2, layer,
                         ffn_ln_g[layer].reshape(1, d), ffn_ln_b[layer].reshape(1, d), alpha)
    return xf.reshape(bsz, seq, d)
```

```python
import functools

import jax
import jax.numpy as jnp
from jax import lax
from jax.experimental import pallas as pl
from jax.experimental.pallas import tpu as pltpu

LN_EPS = 1e-5
DILATED_BRANCHES = ((128, 1), (512, 4), (2048, 16))
BAND_BLOCK = 128
A_HEAD_DIM = 128
ATT_TILE = 2048
ATT_UNROLL = 16
CONV_WIDTH = 31
CONV_HALO = 32
C_HEADS = 4
MLSTM_CHUNK = 256
GATE_LANES = 128
GATE_CHUNKS_PER_STEP = 4
SUBLANES = 8
FFN_DMA_CHUNKS = 4
QKV_ROW_CHUNKS = 2
V7X_VMEM_LIMIT = 56 * 1024 * 1024
V7X_VMEM_LIMIT_FFN = 58 * 1024 * 1024

BF16 = jnp.bfloat16
F32 = jnp.float32


def _params(semantics, vmem_limit=V7X_VMEM_LIMIT):
    return pltpu.CompilerParams(dimension_semantics=semantics, vmem_limit_bytes=vmem_limit)


def _layer_norm(z, g, b):
    mu = jnp.mean(z, axis=-1, keepdims=True)
    zc = z - mu
    var = jnp.mean(zc * zc, axis=-1, keepdims=True)
    return zc * lax.rsqrt(var + LN_EPS) * g + b


def _matmul_kernel(w_is_transposed, sigmoid_from_block, x_ref, w_ref, s_ref, o_ref, wb_ref):
    @pl.when(pl.program_id(1) == 0)
    def _():
        w = w_ref[...]
        wb_ref[...] = (w.T if w_is_transposed else w).astype(BF16)

    out = jnp.dot(x_ref[...], wb_ref[...], preferred_element_type=F32) * s_ref[...]
    if sigmoid_from_block is not None:
        out = jnp.where(pl.program_id(0) >= sigmoid_from_block, jax.nn.sigmoid(out), out)
    o_ref[...] = out.astype(o_ref.dtype)


def _matmul(x, w_stack, layer, col0, n_cols, col_scale, out_dtype, w_is_transposed=False, sigmoid_from_block=None,
            tm=1024, tn=1024):
    m, k = x.shape
    cb0 = col0 // tn
    if w_is_transposed:
        w_spec = pl.BlockSpec((None, tn, k), lambda j, i: (layer, cb0 + j, 0))
    else:
        w_spec = pl.BlockSpec((None, k, tn), lambda j, i: (layer, 0, cb0 + j))
    return pl.pallas_call(
        functools.partial(_matmul_kernel, w_is_transposed, sigmoid_from_block),
        grid=(n_cols // tn, m // tm),
        in_specs=[pl.BlockSpec((tm, k), lambda j, i: (i, 0)),
                  w_spec,
                  pl.BlockSpec((1, tn), lambda j, i: (0, j))],
        out_specs=pl.BlockSpec((tm, tn), lambda j, i: (i, j)),
        out_shape=jax.ShapeDtypeStruct((m, n_cols), out_dtype),
        scratch_shapes=[pltpu.VMEM((k, tn), BF16)],
        compiler_params=_params(("parallel", "arbitrary")),
        name="proj_matmul",
    )(x, w_stack, col_scale)


def _out_proj_ln_kernel(alpha, n_in, *refs):
    ys = refs[:n_in]
    w_ref, x_ref, g_ref, b_ref, of_ref, ob_ref, wb_ref = refs[n_in:]

    @pl.when(pl.program_id(0) == 0)
    def _():
        wb_ref[...] = w_ref[...].astype(BF16)

    acc = alpha * x_ref[...]
    k0 = 0
    for y_ref in ys:
        kw = y_ref.shape[1]
        acc = acc + jnp.dot(y_ref[...], wb_ref[k0:k0 + kw, :], preferred_element_type=F32)
        k0 += kw
    out = _layer_norm(acc, g_ref[...], b_ref[...])
    of_ref[...] = out
    ob_ref[...] = out.astype(BF16)


def _out_proj_ln(ys, w_stack, layer, x, g, b, alpha, tm=512):
    m, d = x.shape
    n_in = len(ys)
    k = w_stack.shape[1]
    in_specs = ([pl.BlockSpec((tm, y.shape[1]), lambda i: (i, 0)) for y in ys]
                + [pl.BlockSpec((None, k, d), lambda i: (layer, 0, 0), pipeline_mode=pl.Buffered(1)),
                   pl.BlockSpec((tm, d), lambda i: (i, 0)),
                   pl.BlockSpec((1, d), lambda i: (0, 0)),
                   pl.BlockSpec((1, d), lambda i: (0, 0))])
    return pl.pallas_call(
        functools.partial(_out_proj_ln_kernel, alpha, n_in),
        grid=(m // tm,),
        in_specs=in_specs,
        out_specs=[pl.BlockSpec((tm, d), lambda i: (i, 0)),
                   pl.BlockSpec((tm, d), lambda i: (i, 0))],
        out_shape=[jax.ShapeDtypeStruct((m, d), F32), jax.ShapeDtypeStruct((m, d), BF16)],
        scratch_shapes=[pltpu.VMEM((k, d), BF16)],
        compiler_params=_params(("arbitrary",)),
        name="out_proj_ln",
    )(*ys, w_stack, x, g, b)


def _ffn_kernel(alpha, layer, tm, th, n_blocks, x_hbm, w1_hbm, w2_hbm, g_ref, b_ref, of_hbm, ob_hbm,
                x_buf, xb_ref, acc_ref, of_buf, ob_buf, w1_buf, w2_buf, sem_x, sem_w, sem_o):
    i = pl.program_id(0)
    last_tile = pl.num_programs(0) - 1
    chunks = FFN_DMA_CHUNKS

    def w_copies(j, slot):
        col = pl.multiple_of(j * th, th)
        r1 = w1_buf.shape[1] // chunks
        r2 = th // chunks
        copies = []
        for c in range(chunks):
            copies.append(pltpu.make_async_copy(
                w1_hbm.at[layer, pl.ds(c * r1, r1), pl.ds(col, th)],
                w1_buf.at[slot, pl.ds(c * r1, r1), :], sem_w.at[0, slot, c]))
            copies.append(pltpu.make_async_copy(
                w2_hbm.at[layer, pl.ds(col + c * r2, r2), :],
                w2_buf.at[slot, pl.ds(c * r2, r2), :], sem_w.at[1, slot, c]))
        return copies

    def x_copies(tile):
        rows = tm // chunks
        row0 = pl.multiple_of(tile * tm, tm)
        return [pltpu.make_async_copy(x_hbm.at[pl.ds(row0 + c * rows, rows), :],
                                      x_buf.at[pl.ds(c * rows, rows), :], sem_x.at[c]) for c in range(chunks)]

    def out_copies(tile):
        rows = tm // chunks
        row0 = pl.multiple_of(tile * tm, tm)
        copies = []
        for c in range(chunks):
            copies.append(pltpu.make_async_copy(of_buf.at[pl.ds(c * rows, rows), :],
                                                of_hbm.at[pl.ds(row0 + c * rows, rows), :], sem_o.at[0, c]))
            copies.append(pltpu.make_async_copy(ob_buf.at[pl.ds(c * rows, rows), :],
                                                ob_hbm.at[pl.ds(row0 + c * rows, rows), :], sem_o.at[1, c]))
        return copies

    def start(copies):
        for copy in copies:
            copy.start()

    def wait(copies):
        for copy in copies:
            copy.wait()

    @pl.when(i == 0)
    def _():
        start(x_copies(0))
        start(w_copies(0, 0))

    wait(x_copies(i))
    x = x_buf[...]
    xb_ref[...] = x.astype(BF16)
    acc_ref[...] = alpha * x

    @pl.when(i < last_tile)
    def _():
        start(x_copies(i + 1))

    def block(j, slot):
        wait(w_copies(j, slot))
        h = jnp.dot(xb_ref[...], w1_buf[slot].astype(BF16), preferred_element_type=F32)
        h = jnp.maximum(h, 0.0)
        h = (h * h).astype(BF16)
        acc_ref[...] += jnp.dot(h, w2_buf[slot].astype(BF16), preferred_element_type=F32)

    def pair(p, carry):
        j = 2 * p
        start(w_copies(j + 1, 1))
        block(j, 0)

        @pl.when(j + 2 < n_blocks)
        def _():
            start(w_copies(j + 2, 0))

        @pl.when(jnp.logical_and(j + 2 == n_blocks, i < last_tile))
        def _():
            start(w_copies(0, 0))

        block(j + 1, 1)
        return carry

    lax.fori_loop(0, n_blocks // 2, pair, 0)

    @pl.when(i > 0)
    def _():
        wait(out_copies(i - 1))

    out = _layer_norm(acc_ref[...], g_ref[...], b_ref[...])
    of_buf[...] = out
    ob_buf[...] = out.astype(BF16)
    start(out_copies(i))

    @pl.when(i == last_tile)
    def _():
        wait(out_copies(i))


def _ffn_ln(x, w1, w2, layer, g, b, alpha, tm=1024, th=512):
    m, d = x.shape
    f = w1.shape[2]
    n_blocks = f // th
    assert n_blocks % 2 == 0 and m % tm == 0
    hbm = pl.BlockSpec(memory_space=pl.ANY)
    return pl.pallas_call(
        functools.partial(_ffn_kernel, alpha, layer, tm, th, n_blocks),
        grid=(m // tm,),
        in_specs=[hbm, hbm, hbm,
                  pl.BlockSpec((1, d), lambda i: (0, 0)),
                  pl.BlockSpec((1, d), lambda i: (0, 0))],
        out_specs=[hbm, hbm],
        out_shape=[jax.ShapeDtypeStruct((m, d), F32), jax.ShapeDtypeStruct((m, d), BF16)],
        scratch_shapes=[pltpu.VMEM((tm, d), F32),
                        pltpu.VMEM((tm, d), BF16),
                        pltpu.VMEM((tm, d), F32),
                        pltpu.VMEM((tm, d), F32),
                        pltpu.VMEM((tm, d), BF16),
                        pltpu.VMEM((2, d, th), F32),
                        pltpu.VMEM((2, th, d), F32),
                        pltpu.SemaphoreType.DMA((FFN_DMA_CHUNKS,)),
                        pltpu.SemaphoreType.DMA((2, 2, FFN_DMA_CHUNKS)),
                        pltpu.SemaphoreType.DMA((2, FFN_DMA_CHUNKS))],
        compiler_params=_params(("arbitrary",), V7X_VMEM_LIMIT_FFN),
        name="ffn_ln",
    )(x, w1, w2, g, b)


def _kv_rows(dil):
    return dil * BAND_BLOCK + ATT_TILE


def _attn_kernel(q_ref, k_ref, v_ref, slope_ref, out_ref,
                 f_ref, g4_ref, qs_ref, ks_ref, vs_ref, o_ref, l_ref, bias_ref, s_ref, e_ref):
    tile = pl.program_id(2)
    p = BAND_BLOCK
    t_len = ATT_TILE
    qi = lax.broadcasted_iota(jnp.int32, (p, 2 * p), 0)
    kj = lax.broadcasted_iota(jnp.int32, (p, 2 * p), 1)
    dist = p + qi - kj
    valid = jnp.logical_and(dist >= 0, dist <= p)
    in_prev_block = kj < p
    slope = slope_ref[0]
    for bi, (_, dil) in enumerate(DILATED_BRANCHES):
        bias_ref[bi] = jnp.where(valid, -(slope * (dist * dil).astype(F32)), -jnp.inf)

    def rows(start, size, dil):
        return pl.ds(start, size, stride=dil) if dil > 1 else pl.ds(start, size)

    kv_base = [sum(_kv_rows(d) for _, d in DILATED_BRANCHES[:bi]) for bi in range(len(DILATED_BRANCHES))]
    srcs = (q_ref, k_ref, v_ref)
    dil4 = DILATED_BRANCHES[1][1]
    w4 = t_len // dil4
    for x, src in enumerate(srcs):
        f_ref[x] = src[...].astype(F32)
        for r in range(dil4):
            g4_ref[x, r * w4:(r + 1) * w4, :] = f_ref[x, rows(r, w4, dil4), :]

    def residue_rows(x, r, dil):
        if dil == 1:
            return srcs[x][...]
        if dil == dil4:
            return g4_ref[x, r * w4:(r + 1) * w4, :]
        return g4_ref[x, rows((r % dil4) * w4 + r // dil4, t_len // dil, dil // dil4), :]

    nt = (((1,), (1,)), ((), ()))
    for bi, (_, dil) in enumerate(DILATED_BRANCHES):
        wq = t_len // dil
        nq = wq // p
        wk = p + wq
        base = kv_base[bi]

        @pl.when(tile == 0)
        def _(dil=dil, wk=wk, base=base):
            for r in range(dil):
                for dst_ref in (ks_ref, vs_ref):
                    dst_ref[base + r * wk:base + r * wk + p, :] = jnp.zeros((p, A_HEAD_DIM), BF16)

        @pl.when(tile > 0)
        def _(dil=dil, wk=wk, wq=wq, base=base):
            for r in range(dil):
                for dst_ref in (ks_ref, vs_ref):
                    dst_ref[base + r * wk:base + r * wk + p, :] = dst_ref[base + r * wk + wq:base + (r + 1) * wk, :]

        for r in range(dil):
            qs_ref[r * wq:(r + 1) * wq, :] = residue_rows(0, r, dil).astype(BF16)
            ks_ref[base + r * wk + p:base + (r + 1) * wk, :] = residue_rows(1, r, dil).astype(BF16)
            vs_ref[base + r * wk + p:base + (r + 1) * wk, :] = residue_rows(2, r, dil).astype(BF16)

        def group(g, carry, dil=dil, nq=nq, wk=wk, bi=bi, base=base):
            starts = []
            for u in range(ATT_UNROLL):
                idx = g * ATT_UNROLL + u
                r = idx // nq
                n = idx - r * nq
                q0 = pl.multiple_of(idx * p, p)
                k0 = pl.multiple_of(base + r * wk + n * p, p)
                s = lax.dot_general(qs_ref[pl.ds(q0, p), :], ks_ref[pl.ds(k0, 2 * p), :], nt,
                                    preferred_element_type=F32) + bias_ref[bi]
                no_history = jnp.logical_and(tile == 0, n == 0)
                s_ref[u * p:(u + 1) * p, :] = jnp.where(jnp.logical_and(no_history, in_prev_block), -jnp.inf, s)
                starts.append((k0, r + dil * p * n))
            s = s_ref[...]
            m = jnp.max(s, axis=-1, keepdims=True)
            e = jnp.exp(s - m)
            l = jnp.sum(e, axis=-1, keepdims=True)
            e_ref[...] = e.astype(BF16)
            inv_l = 1.0 / l
            lse = m + jnp.log(l)
            for u, (k0, t0) in enumerate(starts):
                o = jnp.dot(e_ref[u * p:(u + 1) * p, :], vs_ref[pl.ds(k0, 2 * p), :], preferred_element_type=F32)
                o_ref[bi, rows(t0, p, dil), :] = o * inv_l[u * p:(u + 1) * p]
                l_ref[bi, rows(t0, p, dil), :] = jnp.broadcast_to(lse[u * p:(u + 1) * p], (p, A_HEAD_DIM))
            return carry

        lax.fori_loop(0, (dil * nq) // ATT_UNROLL, group, 0)

    chunk = 256
    for c0 in range(0, t_len, chunk):
        sl = slice(c0, c0 + chunk)
        l1, l2, l3 = l_ref[0, sl, :], l_ref[1, sl, :], l_ref[2, sl, :]
        mx = jnp.maximum(jnp.maximum(l1, l2), l3)
        e1, e2, e3 = jnp.exp(l1 - mx), jnp.exp(l2 - mx), jnp.exp(l3 - mx)
        mix = (e1 * o_ref[0, sl, :] + e2 * o_ref[1, sl, :] + e3 * o_ref[2, sl, :]) / (e1 + e2 + e3)
        out_ref[sl, :] = mix.astype(out_ref.dtype)


def _dilated_mixture_attention(proj, bsz, seq, n_heads):
    m = proj.shape[0]
    hd = A_HEAD_DIM
    t_len = ATT_TILE
    nt = seq // t_len
    slopes = 2.0 ** (-8.0 * jnp.arange(1, n_heads + 1, dtype=F32) / n_heads)
    slopes = jnp.broadcast_to(slopes[:, None, None], (n_heads, 1, 2 * BAND_BLOCK))

    kv_rows = sum(_kv_rows(dil) for _, dil in DILATED_BRANCHES)

    def head_block(col):
        return pl.BlockSpec((t_len, hd), lambda b, h, t: (b * nt + t, col * n_heads + h))

    return pl.pallas_call(
        _attn_kernel,
        grid=(bsz, n_heads, nt),
        in_specs=[head_block(0), head_block(1), head_block(2),
                  pl.BlockSpec((1, 1, 2 * BAND_BLOCK), lambda b, h, t: (h, 0, 0))],
        out_specs=pl.BlockSpec((t_len, hd), lambda b, h, t: (b * nt + t, h)),
        out_shape=jax.ShapeDtypeStruct((m, n_heads * hd), BF16),
        scratch_shapes=[pltpu.VMEM((3, t_len, hd), F32),
                        pltpu.VMEM((3, t_len, hd), F32),
                        pltpu.VMEM((t_len, hd), BF16),
                        pltpu.VMEM((kv_rows, hd), BF16),
                        pltpu.VMEM((kv_rows, hd), BF16),
                        pltpu.VMEM((3, t_len, hd), F32),
                        pltpu.VMEM((3, t_len, hd), F32),
                        pltpu.VMEM((3, BAND_BLOCK, 2 * BAND_BLOCK), F32),
                        pltpu.VMEM((ATT_UNROLL * BAND_BLOCK, 2 * BAND_BLOCK), F32),
                        pltpu.VMEM((ATT_UNROLL * BAND_BLOCK, 2 * BAND_BLOCK), BF16)],
        compiler_params=_params(("parallel", "parallel", "arbitrary")),
        name="dilated_attention",
    )(proj, proj, proj, slopes)


def _conv_tile(has_history, a_ref, g_ref, ha_ref, hg_ref, w_ref, cb_ref, lg_ref, lb_ref, o_ref, h_ref, hs_ref, c_ref,
               n_parts, side_work):
    ts, ch = a_ref.shape
    hist = ha_ref[...] * hg_ref[...]
    h_ref[0:CONV_HALO, :] = jnp.where(has_history, hist, 0.0)
    h_ref[CONV_HALO:, :] = a_ref[...] * g_ref[...]
    first = CONV_HALO - (CONV_WIDTH - 1)
    h_rows = h_ref.shape[0]
    for s in range(1, SUBLANES):
        hs_ref[s - 1, SUBLANES - s:SUBLANES - s + h_rows, :] = h_ref[...]
    rows, lanes = 32, 512
    groups = rows // SUBLANES
    part_rows = ts // n_parts

    def part(p, carry):
        base = pl.multiple_of(p * part_rows, part_rows)
        for r0 in range(0, part_rows, rows):
            for c0 in range(0, ch, lanes):
                acc = jnp.broadcast_to(cb_ref[:, c0:c0 + lanes], (groups, SUBLANES, lanes))
                for j in range(CONV_WIDTH):
                    a8, s = (first + j) // SUBLANES * SUBLANES, (first + j) % SUBLANES
                    if s == 0:
                        tap = h_ref[pl.ds(base + r0 + a8, rows), c0:c0 + lanes]
                    else:
                        tap = hs_ref[s - 1, pl.ds(base + SUBLANES + r0 + a8, rows), c0:c0 + lanes]
                    acc = acc + w_ref[j, :, c0:c0 + lanes] * tap.reshape(groups, SUBLANES, lanes)
                c_ref[pl.ds(base + r0, rows), c0:c0 + lanes] = acc.reshape(rows, lanes)
        side_work(p)
        return carry

    lax.fori_loop(0, n_parts, part, 0)
    y = _layer_norm(c_ref[...], lg_ref[...], lb_ref[...])
    o_ref[...] = (y * jax.nn.sigmoid(y)).astype(o_ref.dtype)


def _qkv_conv_kernel(n_row_tiles, tiles_per_seq, x_ref, w_ref, s_ref, a_ref, g_ref, ha_ref, hg_ref, cw_ref, cb_ref,
                     lg_ref, lb_ref, o_ref, conv_ref, wb_ref, h_ref, hs_ref, c_ref):
    j, i = pl.program_id(0), pl.program_id(1)

    @pl.when(i == 0)
    def _():
        wb_ref[...] = w_ref[...].astype(BF16)

    conv_tile = j * n_row_tiles + i
    rm = x_ref.shape[0] // QKV_ROW_CHUNKS

    def matmul_chunk(c):
        r0 = pl.multiple_of(c * rm, rm)
        acc = jnp.dot(x_ref[pl.ds(r0, rm), :], wb_ref[...], preferred_element_type=F32)
        o_ref[pl.ds(r0, rm), :] = (acc * s_ref[...]).astype(o_ref.dtype)

    _conv_tile(conv_tile % tiles_per_seq != 0, a_ref, g_ref, ha_ref, hg_ref, cw_ref, cb_ref, lg_ref, lb_ref,
               conv_ref, h_ref, hs_ref, c_ref, QKV_ROW_CHUNKS, matmul_chunk)


def _qkv_proj_conv(x, w_stack, layer, n_cols, col_scale, u, seq, conv_w, conv_b, ln_g, ln_b, tm=1024, tn=768):
    m, k = x.shape
    ch = conv_w.shape[1]
    n_col_blocks, n_row_tiles = n_cols // tn, m // tm
    ts = m // (n_col_blocks * n_row_tiles)
    assert n_cols % tn == 0 and m % tm == 0 and ts % CONV_HALO == 0 and seq % ts == 0
    per = ts // CONV_HALO
    w_rows = jnp.broadcast_to(conv_w[:, None, :], (CONV_WIDTH, SUBLANES, ch))

    def tile(col):
        return pl.BlockSpec((ts, ch), lambda j, i: (j * n_row_tiles + i, col))

    def halo(col):
        return pl.BlockSpec((CONV_HALO, ch), lambda j, i: (jnp.maximum((j * n_row_tiles + i) * per - 1, 0), col))

    def const(shape):
        return pl.BlockSpec(shape, lambda j, i: (0,) * len(shape))

    return pl.pallas_call(
        functools.partial(_qkv_conv_kernel, n_row_tiles, seq // ts),
        grid=(n_col_blocks, n_row_tiles),
        in_specs=[pl.BlockSpec((tm, k), lambda j, i: (i, 0)),
                  pl.BlockSpec((None, k, tn), lambda j, i: (layer, 0, j)),
                  pl.BlockSpec((1, tn), lambda j, i: (0, j)),
                  tile(0), tile(1), halo(0), halo(1),
                  const((CONV_WIDTH, SUBLANES, ch)), const((1, ch)), const((1, ch)), const((1, ch))],
        out_specs=[pl.BlockSpec((tm, tn), lambda j, i: (i, j)),
                   pl.BlockSpec((ts, ch), lambda j, i: (j * n_row_tiles + i, 0))],
        out_shape=[jax.ShapeDtypeStruct((m, n_cols), BF16), jax.ShapeDtypeStruct((m, ch), BF16)],
        scratch_shapes=[pltpu.VMEM((k, tn), BF16),
                        pltpu.VMEM((ts + CONV_HALO, ch), F32),
                        pltpu.VMEM((SUBLANES - 1, ts + CONV_HALO + SUBLANES, ch), F32),
                        pltpu.VMEM((ts, ch), F32)],
        compiler_params=_params(("parallel", "arbitrary")),
        name="qkv_proj_conv",
    )(x, w_stack, col_scale, u, u, u, u, w_rows, conv_b.reshape(1, ch), ln_g.reshape(1, ch), ln_b.reshape(1, ch))


def _gates_kernel(x_ref, w_ref, bias_ref, gc_ref, gr_ref):
    length = MLSTM_CHUNK
    w = w_ref[...].astype(BF16)
    w = jnp.concatenate([w, jnp.zeros((GATE_LANES - w.shape[0], w.shape[1]), BF16)], axis=0)
    pre = lax.dot_general(x_ref[...], w, (((1,), (1,)), ((), ())), preferred_element_type=F32) + bias_ref[...]
    log_f = jnp.minimum(pre, 0.0) - jnp.log1p(jnp.exp(-jnp.abs(pre)))
    ti = lax.broadcasted_iota(jnp.int32, (length, length), 0)
    si = lax.broadcasted_iota(jnp.int32, (length, length), 1)
    tril = (ti >= si).astype(F32)
    lane = lax.broadcasted_iota(jnp.int32, (length, GATE_LANES), 1)
    for r0 in range(0, x_ref.shape[0], length):
        cum_f = jnp.dot(tril, log_f[r0:r0 + length], preferred_element_type=F32, precision=lax.Precision.HIGHEST)
        gates = jnp.where(lane < C_HEADS, pre[r0:r0 + length], cum_f)
        gc_ref[r0:r0 + length, :] = gates
        gr_ref[:, r0:r0 + length] = gates.T


def _mlstm_gates(x, w_stack_t, layer, gate_row0, bias):
    m, d = x.shape
    length = GATE_CHUNKS_PER_STEP * MLSTM_CHUNK
    gate_rows = 2 * C_HEADS
    gate_block = gate_row0 // gate_rows
    return pl.pallas_call(
        _gates_kernel,
        grid=(m // length,),
        in_specs=[pl.BlockSpec((length, d), lambda i: (i, 0)),
                  pl.BlockSpec((None, gate_rows, d), lambda i: (layer, gate_block, 0)),
                  pl.BlockSpec((1, GATE_LANES), lambda i: (0, 0))],
        out_specs=[pl.BlockSpec((length, GATE_LANES), lambda i: (i, 0)),
                   pl.BlockSpec((GATE_LANES, length), lambda i: (0, i))],
        out_shape=[jax.ShapeDtypeStruct((m, GATE_LANES), F32),
                   jax.ShapeDtypeStruct((GATE_LANES, m), F32)],
        compiler_params=_params(("parallel",)),
        name="mlstm_gates",
    )(x, w_stack_t, bias)


def _mlstm_kernel(dk, dv, q_ref, k_ref, v_ref, o_ref, gc_ref, gr_ref, ng_ref, y_ref, c_ref, n_ref, m_ref):
    length = q_ref.shape[0]

    @pl.when(pl.program_id(1) == 0)
    def _():
        c_ref[...] = jnp.zeros_like(c_ref)
        n_ref[...] = jnp.zeros_like(n_ref)
        m_ref[...] = jnp.zeros_like(m_ref)

    ti = lax.broadcasted_iota(jnp.int32, (length, length), 0)
    si = lax.broadcasted_iota(jnp.int32, (length, length), 1)
    causal = ti >= si
    for h in range(C_HEADS):
        q = q_ref[:, h * dk:(h + 1) * dk]
        k = k_ref[:, h * dk:(h + 1) * dk]
        v = v_ref[:, h * dv:(h + 1) * dv]
        i_row = gr_ref[h:h + 1, :]
        b_row = gr_ref[C_HEADS + h:C_HEADS + h + 1, :]
        i_col = gc_ref[:, h:h + 1]
        b_col = gc_ref[:, C_HEADS + h:C_HEADS + h + 1]
        m_prev = m_ref[h:h + 1, 0:1]
        c_prev = c_ref[h]
        n_prev = n_ref[h:h + 1, :]

        dmat = jnp.where(causal, b_col - b_row + i_row, -jnp.inf)
        inter = b_col + m_prev
        m_t = jnp.maximum(inter, jnp.max(dmat, axis=-1, keepdims=True))
        w = jnp.exp(dmat - m_t)
        a = jnp.exp(inter - m_t)
        qk = lax.dot_general(q, k, (((1,), (1,)), ((), ())), preferred_element_type=F32)
        sqk = qk * w
        num = (a * jnp.dot(q, c_prev.astype(BF16), preferred_element_type=F32)
               + jnp.dot(sqk.astype(BF16), v, preferred_element_type=F32))
        den = (a * jnp.sum(q.astype(F32) * n_prev, axis=-1, keepdims=True)
               + jnp.sum(sqk, axis=-1, keepdims=True))
        hh = num / jnp.maximum(jnp.abs(den), jnp.exp(-m_t))

        b_last = b_col[length - 1:length, :]
        g = b_last - b_col + i_col
        m_new = jnp.maximum(b_last + m_prev, jnp.max(g, axis=0, keepdims=True))
        decay = jnp.exp(b_last + m_prev - m_new)
        kw = k.astype(F32) * jnp.exp(g - m_new)
        c_ref[h] = decay * c_prev + lax.dot_general(kw.astype(BF16), v, (((0,), (0,)), ((), ())),
                                                    preferred_element_type=F32)
        n_ref[h:h + 1, :] = decay * n_prev + jnp.sum(kw, axis=0, keepdims=True)
        m_ref[h:h + 1, :] = jnp.broadcast_to(m_new, (1, m_ref.shape[1]))

        mu = jnp.mean(hh, axis=-1, keepdims=True)
        hc = hh - mu
        var = jnp.mean(hc * hc, axis=-1, keepdims=True)
        hn = hc * lax.rsqrt(var + LN_EPS) * ng_ref[:, h * dv:(h + 1) * dv]
        y_ref[:, h * dv:(h + 1) * dv] = (o_ref[:, h * dv:(h + 1) * dv] * hn).astype(y_ref.dtype)


def _mlstm(qkv, o, gates_col, gates_row, norm_g, bsz, seq, dk, dv):
    length = MLSTM_CHUNK
    nc = seq // length
    qk_w = C_HEADS * dk
    v_w = C_HEADS * dv
    v_blk = (2 * qk_w) // v_w
    return pl.pallas_call(
        functools.partial(_mlstm_kernel, dk, dv),
        grid=(bsz, nc),
        in_specs=[pl.BlockSpec((length, qk_w), lambda b, c: (b * nc + c, 0)),
                  pl.BlockSpec((length, qk_w), lambda b, c: (b * nc + c, 1)),
                  pl.BlockSpec((length, v_w), lambda b, c: (b * nc + c, v_blk)),
                  pl.BlockSpec((length, v_w), lambda b, c: (b * nc + c, 0)),
                  pl.BlockSpec((length, GATE_LANES), lambda b, c: (b * nc + c, 0)),
                  pl.BlockSpec((GATE_LANES, length), lambda b, c: (0, b * nc + c)),
                  pl.BlockSpec((1, v_w), lambda b, c: (0, 0))],
        out_specs=pl.BlockSpec((length, v_w), lambda b, c: (b * nc + c, 0)),
        out_shape=jax.ShapeDtypeStruct((bsz * seq, v_w), BF16),
        scratch_shapes=[pltpu.VMEM((C_HEADS, dk, dv), F32),
                        pltpu.VMEM((8, dk), F32),
                        pltpu.VMEM((8, GATE_LANES), F32)],
        compiler_params=_params(("parallel", "arbitrary")),
        name="mlstm_chunk",
    )(qkv, qkv, qkv, o, gates_col, gates_row, norm_g)


def _even_mixer(xb, x, w_in, j, conv_w, conv_b, conv_ln_g, conv_ln_b, w_out, ln_g, ln_b, alpha, bsz, seq):
    d = x.shape[1]
    a_width = d // 2
    n_heads = a_width // A_HEAD_DIM
    in_width = w_in.shape[2]
    qkv_w = 3 * a_width
    qkv_scale = jnp.ones((1, qkv_w), F32).at[:, :a_width].set(A_HEAD_DIM ** -0.5)
    u = _matmul(xb, w_in, j, qkv_w, in_width - qkv_w, jnp.ones((1, in_width - qkv_w), F32), F32,
                sigmoid_from_block=1)
    qkv, conv = _qkv_proj_conv(xb, w_in, j, qkv_w, qkv_scale, u, seq, conv_w, conv_b, conv_ln_g, conv_ln_b)
    att = _dilated_mixture_attention(qkv, bsz, seq, n_heads)
    return _out_proj_ln([att, conv], w_out, j, x, ln_g.reshape(1, d), ln_b.reshape(1, d), alpha)


def _odd_mixer(xb, x, w_in_t, j, igate_b, fgate_b, norm_g, w_out, ln_g, ln_b, alpha, bsz, seq):
    d = x.shape[1]
    v_w = d
    dv = v_w // C_HEADS
    dk = dv // 2
    qk_w = C_HEADS * dk
    qkv_w = 2 * qk_w + v_w
    qkv_scale = jnp.ones((1, qkv_w), F32).at[:, :qk_w].set(dk ** -0.5)
    qkv = _matmul(xb, w_in_t, j, 0, qkv_w, qkv_scale, BF16, w_is_transposed=True)
    o = _matmul(xb, w_in_t, j, qkv_w, v_w, jnp.ones((1, v_w), F32), F32, w_is_transposed=True,
                sigmoid_from_block=0)
    bias = jnp.zeros((1, GATE_LANES), F32).at[0, :C_HEADS].set(igate_b).at[0, C_HEADS:2 * C_HEADS].set(fgate_b)
    gates_col, gates_row = _mlstm_gates(xb, w_in_t, j, qkv_w + v_w, bias)
    y = _mlstm(qkv, o, gates_col, gates_row, norm_g.reshape(1, v_w), bsz, seq, dk, dv)
    return _out_proj_ln([y], w_out, j, x, ln_g.reshape(1, d), ln_b.reshape(1, d), alpha)


def kernel(x, even_w_in, even_conv_w, even_conv_b, even_conv_ln_g, even_conv_ln_b, even_w_out,
           odd_w_in, odd_igate_b, odd_fgate_b, odd_norm_g, odd_w_out, mix_ln_g, mix_ln_b,
           ffn_w1, ffn_w2, ffn_ln_g, ffn_ln_b):
    bsz, seq, d = x.shape
    depth = mix_ln_g.shape[0]
    alpha = (2 * depth) ** 0.25
    xf = x.reshape(bsz * seq, d)
    xb = xf.astype(BF16)
    odd_w_in_t = jnp.swapaxes(odd_w_in, 1, 2)
    for layer in range(depth):
        j = layer // 2
        if layer % 2 == 0:
            xf, xb = _even_mixer(xb, xf, even_w_in, j, even_conv_w[j], even_conv_b[j], even_conv_ln_g[j],
                                 even_conv_ln_b[j], even_w_out, mix_ln_g[layer], mix_ln_b[layer],
                                 alpha, bsz, seq)
        else:
            xf, xb = _odd_mixer(xb, xf, odd_w_in_t, j, odd_igate_b[j], odd_fgate_b[j], odd_norm_g[j],
                                odd_w_out, mix_ln_g[layer], mix_ln_b[layer], alpha, bsz, seq)
        xf, xb = _ffn_ln(xf, ffn_w1, ffn_w2, layer,
                         ffn_ln_g[layer].reshape(1, d), ffn_ln_b[layer].reshape(1, d), alpha)
    return xf.reshape(bsz, seq, d)
```

```python
import functools

import jax
import jax.numpy as jnp
from jax import lax
from jax.experimental import pallas as pl
from jax.experimental.pallas import tpu as pltpu

LN_EPS = 1e-5
DILATED_BRANCHES = ((128, 1), (512, 4), (2048, 16))
BAND_BLOCK = 128
A_HEAD_DIM = 128
ATT_TILE = 2048
ATT_UNROLL = 16
CONV_WIDTH = 31
CONV_HALO = 32
C_HEADS = 4
MLSTM_CHUNK = 256
GATE_LANES = 128
GATE_CHUNKS_PER_STEP = 4
SUBLANES = 8
FFN_DMA_CHUNKS = 4
QKV_ROW_CHUNKS = 2
V7X_VMEM_LIMIT = 56 * 1024 * 1024
V7X_VMEM_LIMIT_FFN = 58 * 1024 * 1024

BF16 = jnp.bfloat16
F32 = jnp.float32


def _params(semantics, vmem_limit=V7X_VMEM_LIMIT):
    return pltpu.CompilerParams(dimension_semantics=semantics, vmem_limit_bytes=vmem_limit)


def _layer_norm(z, g, b):
    mu = jnp.mean(z, axis=-1, keepdims=True)
    zc = z - mu
    var = jnp.mean(zc * zc, axis=-1, keepdims=True)
    return zc * lax.rsqrt(var + LN_EPS) * g + b


def _matmul_kernel(w_is_transposed, has_scale, x_ref, w_ref, *rest):
    s_ref = rest[0] if has_scale else None
    o_ref, wb_ref = rest[-2:]

    @pl.when(pl.program_id(1) == 0)
    def _():
        w = w_ref[...]
        wb_ref[...] = (w.T if w_is_transposed else w).astype(BF16)

    acc = jnp.dot(x_ref[...], wb_ref[...], preferred_element_type=F32)
    if has_scale:
        acc = acc * s_ref[...]
    o_ref[...] = acc.astype(o_ref.dtype)


def _matmul(x, w_stack, layer, col0, n_cols, col_scale, out_dtype, w_is_transposed=False, tm=1024, tn=1024):
    m, k = x.shape
    cb0 = col0 // tn
    if w_is_transposed:
        w_spec = pl.BlockSpec((None, tn, k), lambda j, i: (layer, cb0 + j, 0))
    else:
        w_spec = pl.BlockSpec((None, k, tn), lambda j, i: (layer, 0, cb0 + j))
    has_scale = col_scale is not None
    scale_spec = [pl.BlockSpec((1, tn), lambda j, i: (0, j))] if has_scale else []
    scale_arg = [col_scale] if has_scale else []
    return pl.pallas_call(
        functools.partial(_matmul_kernel, w_is_transposed, has_scale),
        grid=(n_cols // tn, m // tm),
        in_specs=[pl.BlockSpec((tm, k), lambda j, i: (i, 0)), w_spec] + scale_spec,
        out_specs=pl.BlockSpec((tm, tn), lambda j, i: (i, j)),
        out_shape=jax.ShapeDtypeStruct((m, n_cols), out_dtype),
        scratch_shapes=[pltpu.VMEM((k, tn), BF16)],
        compiler_params=_params(("parallel", "arbitrary")),
        name="proj_matmul",
    )(x, w_stack, *scale_arg)


def _out_proj_ln_kernel(alpha, n_in, *refs):
    ys = refs[:n_in]
    w_ref, x_ref, g_ref, b_ref, of_ref, ob_ref, wb_ref = refs[n_in:]

    @pl.when(pl.program_id(0) == 0)
    def _():
        wb_ref[...] = w_ref[...].astype(BF16)

    acc = alpha * x_ref[...]
    k0 = 0
    for y_ref in ys:
        kw = y_ref.shape[1]
        acc = acc + jnp.dot(y_ref[...], wb_ref[k0:k0 + kw, :], preferred_element_type=F32)
        k0 += kw
    out = _layer_norm(acc, g_ref[...], b_ref[...])
    of_ref[...] = out
    ob_ref[...] = out.astype(BF16)


def _out_proj_ln(ys, w_stack, layer, x, g, b, alpha, tm=512):
    m, d = x.shape
    n_in = len(ys)
    k = w_stack.shape[1]
    in_specs = ([pl.BlockSpec((tm, y.shape[1]), lambda i: (i, 0)) for y in ys]
                + [pl.BlockSpec((None, k, d), lambda i: (layer, 0, 0), pipeline_mode=pl.Buffered(1)),
                   pl.BlockSpec((tm, d), lambda i: (i, 0)),
                   pl.BlockSpec((1, d), lambda i: (0, 0)),
                   pl.BlockSpec((1, d), lambda i: (0, 0))])
    return pl.pallas_call(
        functools.partial(_out_proj_ln_kernel, alpha, n_in),
        grid=(m // tm,),
        in_specs=in_specs,
        out_specs=[pl.BlockSpec((tm, d), lambda i: (i, 0)),
                   pl.BlockSpec((tm, d), lambda i: (i, 0))],
        out_shape=[jax.ShapeDtypeStruct((m, d), F32), jax.ShapeDtypeStruct((m, d), BF16)],
        scratch_shapes=[pltpu.VMEM((k, d), BF16)],
        compiler_params=_params(("arbitrary",)),
        name="out_proj_ln",
    )(*ys, w_stack, x, g, b)


def _ffn_kernel(alpha, layer, tm, th, n_blocks, x_hbm, w1_hbm, w2_hbm, g_ref, b_ref, of_hbm, ob_hbm,
                x_buf, xb_ref, acc_ref, of_buf, ob_buf, w1_buf, w2_buf, sem_x, sem_w, sem_o):
    i = pl.program_id(0)
    last_tile = pl.num_programs(0) - 1
    chunks = FFN_DMA_CHUNKS

    def w_copies(j, slot):
        col = pl.multiple_of(j * th, th)
        r1 = w1_buf.shape[1] // chunks
        r2 = th // chunks
        copies = []
        for c in range(chunks):
            copies.append(pltpu.make_async_copy(
                w1_hbm.at[layer, pl.ds(c * r1, r1), pl.ds(col, th)],
                w1_buf.at[slot, pl.ds(c * r1, r1), :], sem_w.at[0, slot, c]))
            copies.append(pltpu.make_async_copy(
                w2_hbm.at[layer, pl.ds(col + c * r2, r2), :],
                w2_buf.at[slot, pl.ds(c * r2, r2), :], sem_w.at[1, slot, c]))
        return copies

    def x_copies(tile):
        rows = tm // chunks
        row0 = pl.multiple_of(tile * tm, tm)
        return [pltpu.make_async_copy(x_hbm.at[pl.ds(row0 + c * rows, rows), :],
                                      x_buf.at[pl.ds(c * rows, rows), :], sem_x.at[c]) for c in range(chunks)]

    def out_copies(tile):
        rows = tm // chunks
        row0 = pl.multiple_of(tile * tm, tm)
        copies = []
        for c in range(chunks):
            copies.append(pltpu.make_async_copy(of_buf.at[pl.ds(c * rows, rows), :],
                                                of_hbm.at[pl.ds(row0 + c * rows, rows), :], sem_o.at[0, c]))
            copies.append(pltpu.make_async_copy(ob_buf.at[pl.ds(c * rows, rows), :],
                                                ob_hbm.at[pl.ds(row0 + c * rows, rows), :], sem_o.at[1, c]))
        return copies

    def start(copies):
        for copy in copies:
            copy.start()

    def wait(copies):
        for copy in copies:
            copy.wait()

    @pl.when(i == 0)
    def _():
        start(x_copies(0))
        start(w_copies(0, 0))

    wait(x_copies(i))
    x = x_buf[...]
    xb_ref[...] = x.astype(BF16)
    acc_ref[...] = alpha * x

    @pl.when(i < last_tile)
    def _():
        start(x_copies(i + 1))

    def block(j, slot):
        wait(w_copies(j, slot))
        h = jnp.dot(xb_ref[...], w1_buf[slot].astype(BF16), preferred_element_type=F32)
        h = jnp.maximum(h, 0.0)
        h = (h * h).astype(BF16)
        acc_ref[...] += jnp.dot(h, w2_buf[slot].astype(BF16), preferred_element_type=F32)

    def pair(p, carry):
        j = 2 * p
        start(w_copies(j + 1, 1))
        block(j, 0)

        @pl.when(j + 2 < n_blocks)
        def _():
            start(w_copies(j + 2, 0))

        @pl.when(jnp.logical_and(j + 2 == n_blocks, i < last_tile))
        def _():
            start(w_copies(0, 0))

        block(j + 1, 1)
        return carry

    lax.fori_loop(0, n_blocks // 2, pair, 0)

    @pl.when(i > 0)
    def _():
        wait(out_copies(i - 1))

    out = _layer_norm(acc_ref[...], g_ref[...], b_ref[...])
    of_buf[...] = out
    ob_buf[...] = out.astype(BF16)
    start(out_copies(i))

    @pl.when(i == last_tile)
    def _():
        wait(out_copies(i))


def _ffn_ln(x, w1, w2, layer, g, b, alpha, tm=1024, th=512):
    m, d = x.shape
    f = w1.shape[2]
    n_blocks = f // th
    assert n_blocks % 2 == 0 and m % tm == 0
    hbm = pl.BlockSpec(memory_space=pl.ANY)
    return pl.pallas_call(
        functools.partial(_ffn_kernel, alpha, layer, tm, th, n_blocks),
        grid=(m // tm,),
        in_specs=[hbm, hbm, hbm,
                  pl.BlockSpec((1, d), lambda i: (0, 0)),
                  pl.BlockSpec((1, d), lambda i: (0, 0))],
        out_specs=[hbm, hbm],
        out_shape=[jax.ShapeDtypeStruct((m, d), F32), jax.ShapeDtypeStruct((m, d), BF16)],
        scratch_shapes=[pltpu.VMEM((tm, d), F32),
                        pltpu.VMEM((tm, d), BF16),
                        pltpu.VMEM((tm, d), F32),
                        pltpu.VMEM((tm, d), F32),
                        pltpu.VMEM((tm, d), BF16),
                        pltpu.VMEM((2, d, th), F32),
                        pltpu.VMEM((2, th, d), F32),
                        pltpu.SemaphoreType.DMA((FFN_DMA_CHUNKS,)),
                        pltpu.SemaphoreType.DMA((2, 2, FFN_DMA_CHUNKS)),
                        pltpu.SemaphoreType.DMA((2, FFN_DMA_CHUNKS))],
        compiler_params=_params(("arbitrary",), V7X_VMEM_LIMIT_FFN),
        name="ffn_ln",
    )(x, w1, w2, g, b)


def _kv_rows(dil):
    return dil * BAND_BLOCK + ATT_TILE


def _attn_kernel(q_ref, k_ref, v_ref, slope_ref, out_ref,
                 f_ref, g4_ref, qs_ref, ks_ref, vs_ref, o_ref, l_ref, bias_ref, s_ref, e_ref):
    tile = pl.program_id(2)
    p = BAND_BLOCK
    t_len = ATT_TILE
    qi = lax.broadcasted_iota(jnp.int32, (p, 2 * p), 0)
    kj = lax.broadcasted_iota(jnp.int32, (p, 2 * p), 1)
    dist = p + qi - kj
    valid = jnp.logical_and(dist >= 0, dist <= p)
    in_prev_block = kj < p
    slope = slope_ref[0]
    for bi, (_, dil) in enumerate(DILATED_BRANCHES):
        bias_ref[bi] = jnp.where(valid, -(slope * (dist * dil).astype(F32)), -jnp.inf)

    def rows(start, size, dil):
        return pl.ds(start, size, stride=dil) if dil > 1 else pl.ds(start, size)

    kv_base = [sum(_kv_rows(d) for _, d in DILATED_BRANCHES[:bi]) for bi in range(len(DILATED_BRANCHES))]
    srcs = (q_ref, k_ref, v_ref)
    dil4 = DILATED_BRANCHES[1][1]
    w4 = t_len // dil4
    for x, src in enumerate(srcs):
        f_ref[x] = src[...].astype(F32)
        for r in range(dil4):
            g4_ref[x, r * w4:(r + 1) * w4, :] = f_ref[x, rows(r, w4, dil4), :]

    def residue_rows(x, r, dil):
        if dil == 1:
            return srcs[x][...]
        if dil == dil4:
            return g4_ref[x, r * w4:(r + 1) * w4, :]
        return g4_ref[x, rows((r % dil4) * w4 + r // dil4, t_len // dil, dil // dil4), :]

    nt = (((1,), (1,)), ((), ()))
    for bi, (_, dil) in enumerate(DILATED_BRANCHES):
        wq = t_len // dil
        nq = wq // p
        wk = p + wq
        base = kv_base[bi]

        @pl.when(tile == 0)
        def _(dil=dil, wk=wk, base=base):
            for r in range(dil):
                for dst_ref in (ks_ref, vs_ref):
                    dst_ref[base + r * wk:base + r * wk + p, :] = jnp.zeros((p, A_HEAD_DIM), BF16)

        @pl.when(tile > 0)
        def _(dil=dil, wk=wk, wq=wq, base=base):
            for r in range(dil):
                for dst_ref in (ks_ref, vs_ref):
                    dst_ref[base + r * wk:base + r * wk + p, :] = dst_ref[base + r * wk + wq:base + (r + 1) * wk, :]

        for r in range(dil):
            qs_ref[r * wq:(r + 1) * wq, :] = residue_rows(0, r, dil).astype(BF16)
            ks_ref[base + r * wk + p:base + (r + 1) * wk, :] = residue_rows(1, r, dil).astype(BF16)
            vs_ref[base + r * wk + p:base + (r + 1) * wk, :] = residue_rows(2, r, dil).astype(BF16)

        def group(g, carry, dil=dil, nq=nq, wk=wk, bi=bi, base=base):
            starts = []
            for u in range(ATT_UNROLL):
                idx = g * ATT_UNROLL + u
                r = idx // nq
                n = idx - r * nq
                q0 = pl.multiple_of(idx * p, p)
                k0 = pl.multiple_of(base + r * wk + n * p, p)
                s = lax.dot_general(qs_ref[pl.ds(q0, p), :], ks_ref[pl.ds(k0, 2 * p), :], nt,
                                    preferred_element_type=F32) + bias_ref[bi]
                no_history = jnp.logical_and(tile == 0, n == 0)
                s_ref[u * p:(u + 1) * p, :] = jnp.where(jnp.logical_and(no_history, in_prev_block), -jnp.inf, s)
                starts.append((k0, r + dil * p * n))
            s = s_ref[...]
            m = jnp.max(s, axis=-1, keepdims=True)
            e = jnp.exp(s - m)
            l = jnp.sum(e, axis=-1, keepdims=True)
            e_ref[...] = e.astype(BF16)
            inv_l = 1.0 / l
            lse = m + jnp.log(l)
            for u, (k0, t0) in enumerate(starts):
                o = jnp.dot(e_ref[u * p:(u + 1) * p, :], vs_ref[pl.ds(k0, 2 * p), :], preferred_element_type=F32)
                o_ref[bi, rows(t0, p, dil), :] = o * inv_l[u * p:(u + 1) * p]
                l_ref[bi, rows(t0, p, dil), :] = jnp.broadcast_to(lse[u * p:(u + 1) * p], (p, A_HEAD_DIM))
            return carry

        lax.fori_loop(0, (dil * nq) // ATT_UNROLL, group, 0)

    chunk = 256
    for c0 in range(0, t_len, chunk):
        sl = slice(c0, c0 + chunk)
        l1, l2, l3 = l_ref[0, sl, :], l_ref[1, sl, :], l_ref[2, sl, :]
        mx = jnp.maximum(jnp.maximum(l1, l2), l3)
        e1, e2, e3 = jnp.exp(l1 - mx), jnp.exp(l2 - mx), jnp.exp(l3 - mx)
        mix = (e1 * o_ref[0, sl, :] + e2 * o_ref[1, sl, :] + e3 * o_ref[2, sl, :]) / (e1 + e2 + e3)
        out_ref[sl, :] = mix.astype(out_ref.dtype)


def _dilated_mixture_attention(proj, bsz, seq, n_heads):
    m = proj.shape[0]
    hd = A_HEAD_DIM
    t_len = ATT_TILE
    nt = seq // t_len
    slopes = 2.0 ** (-8.0 * jnp.arange(1, n_heads + 1, dtype=F32) / n_heads)
    slopes = jnp.broadcast_to(slopes[:, None, None], (n_heads, 1, 2 * BAND_BLOCK))

    kv_rows = sum(_kv_rows(dil) for _, dil in DILATED_BRANCHES)

    def head_block(col):
        return pl.BlockSpec((t_len, hd), lambda b, h, t: (b * nt + t, col * n_heads + h))

    return pl.pallas_call(
        _attn_kernel,
        grid=(bsz, n_heads, nt),
        in_specs=[head_block(0), head_block(1), head_block(2),
                  pl.BlockSpec((1, 1, 2 * BAND_BLOCK), lambda b, h, t: (h, 0, 0))],
        out_specs=pl.BlockSpec((t_len, hd), lambda b, h, t: (b * nt + t, h)),
        out_shape=jax.ShapeDtypeStruct((m, n_heads * hd), BF16),
        scratch_shapes=[pltpu.VMEM((3, t_len, hd), F32),
                        pltpu.VMEM((3, t_len, hd), F32),
                        pltpu.VMEM((t_len, hd), BF16),
                        pltpu.VMEM((kv_rows, hd), BF16),
                        pltpu.VMEM((kv_rows, hd), BF16),
                        pltpu.VMEM((3, t_len, hd), F32),
                        pltpu.VMEM((3, t_len, hd), F32),
                        pltpu.VMEM((3, BAND_BLOCK, 2 * BAND_BLOCK), F32),
                        pltpu.VMEM((ATT_UNROLL * BAND_BLOCK, 2 * BAND_BLOCK), F32),
                        pltpu.VMEM((ATT_UNROLL * BAND_BLOCK, 2 * BAND_BLOCK), BF16)],
        compiler_params=_params(("parallel", "parallel", "arbitrary")),
        name="dilated_attention",
    )(proj, proj, proj, slopes)


def _conv_tile(has_history, a_ref, g_ref, ha_ref, hg_ref, w_ref, cb_ref, lg_ref, lb_ref, o_ref, h_ref, hs_ref, c_ref,
               n_parts, side_work):
    ts, ch = a_ref.shape
    hist = ha_ref[...] * jax.nn.sigmoid(hg_ref[...])
    h_ref[0:CONV_HALO, :] = jnp.where(has_history, hist, 0.0)
    h_ref[CONV_HALO:, :] = a_ref[...] * jax.nn.sigmoid(g_ref[...])
    first = CONV_HALO - (CONV_WIDTH - 1)
    h_rows = h_ref.shape[0]
    for s in range(1, SUBLANES):
        hs_ref[s - 1, SUBLANES - s:SUBLANES - s + h_rows, :] = h_ref[...]
    rows, lanes = 32, 512
    groups = rows // SUBLANES
    part_rows = ts // n_parts

    def part(p, carry):
        base = pl.multiple_of(p * part_rows, part_rows)
        for r0 in range(0, part_rows, rows):
            for c0 in range(0, ch, lanes):
                acc = jnp.broadcast_to(cb_ref[:, c0:c0 + lanes], (groups, SUBLANES, lanes))
                for j in range(CONV_WIDTH):
                    a8, s = (first + j) // SUBLANES * SUBLANES, (first + j) % SUBLANES
                    if s == 0:
                        tap = h_ref[pl.ds(base + r0 + a8, rows), c0:c0 + lanes]
                    else:
                        tap = hs_ref[s - 1, pl.ds(base + SUBLANES + r0 + a8, rows), c0:c0 + lanes]
                    acc = acc + w_ref[j, :, c0:c0 + lanes] * tap.reshape(groups, SUBLANES, lanes)
                c_ref[pl.ds(base + r0, rows), c0:c0 + lanes] = acc.reshape(rows, lanes)
        side_work(p)
        return carry

    lax.fori_loop(0, n_parts, part, 0)
    y = _layer_norm(c_ref[...], lg_ref[...], lb_ref[...])
    o_ref[...] = (y * jax.nn.sigmoid(y)).astype(o_ref.dtype)


def _qkv_conv_kernel(n_row_tiles, tiles_per_seq, x_ref, w_ref, s_ref, a_ref, g_ref, ha_ref, hg_ref, cw_ref, cb_ref,
                     lg_ref, lb_ref, o_ref, conv_ref, wb_ref, h_ref, hs_ref, c_ref):
    j, i = pl.program_id(0), pl.program_id(1)

    @pl.when(i == 0)
    def _():
        wb_ref[...] = w_ref[...].astype(BF16)

    conv_tile = j * n_row_tiles + i
    rm = x_ref.shape[0] // QKV_ROW_CHUNKS

    def matmul_chunk(c):
        r0 = pl.multiple_of(c * rm, rm)
        acc = jnp.dot(x_ref[pl.ds(r0, rm), :], wb_ref[...], preferred_element_type=F32)
        o_ref[pl.ds(r0, rm), :] = (acc * s_ref[...]).astype(o_ref.dtype)

    _conv_tile(conv_tile % tiles_per_seq != 0, a_ref, g_ref, ha_ref, hg_ref, cw_ref, cb_ref, lg_ref, lb_ref,
               conv_ref, h_ref, hs_ref, c_ref, QKV_ROW_CHUNKS, matmul_chunk)


def _qkv_proj_conv(x, w_stack, layer, n_cols, col_scale, u, seq, conv_w, conv_b, ln_g, ln_b, tm=1024, tn=768):
    m, k = x.shape
    ch = conv_w.shape[1]
    n_col_blocks, n_row_tiles = n_cols // tn, m // tm
    ts = m // (n_col_blocks * n_row_tiles)
    assert n_cols % tn == 0 and m % tm == 0 and ts % CONV_HALO == 0 and seq % ts == 0
    per = ts // CONV_HALO
    w_rows = jnp.broadcast_to(conv_w[:, None, :], (CONV_WIDTH, SUBLANES, ch))

    def tile(col):
        return pl.BlockSpec((ts, ch), lambda j, i: (j * n_row_tiles + i, col))

    def halo(col):
        return pl.BlockSpec((CONV_HALO, ch), lambda j, i: (jnp.maximum((j * n_row_tiles + i) * per - 1, 0), col))

    def const(shape):
        return pl.BlockSpec(shape, lambda j, i: (0,) * len(shape))

    return pl.pallas_call(
        functools.partial(_qkv_conv_kernel, n_row_tiles, seq // ts),
        grid=(n_col_blocks, n_row_tiles),
        in_specs=[pl.BlockSpec((tm, k), lambda j, i: (i, 0)),
                  pl.BlockSpec((None, k, tn), lambda j, i: (layer, 0, j)),
                  pl.BlockSpec((1, tn), lambda j, i: (0, j)),
                  tile(0), tile(1), halo(0), halo(1),
                  const((CONV_WIDTH, SUBLANES, ch)), const((1, ch)), const((1, ch)), const((1, ch))],
        out_specs=[pl.BlockSpec((tm, tn), lambda j, i: (i, j)),
                   pl.BlockSpec((ts, ch), lambda j, i: (j * n_row_tiles + i, 0))],
        out_shape=[jax.ShapeDtypeStruct((m, n_cols), BF16), jax.ShapeDtypeStruct((m, ch), BF16)],
        scratch_shapes=[pltpu.VMEM((k, tn), BF16),
                        pltpu.VMEM((ts + CONV_HALO, ch), F32),
                        pltpu.VMEM((SUBLANES - 1, ts + CONV_HALO + SUBLANES, ch), F32),
                        pltpu.VMEM((ts, ch), F32)],
        compiler_params=_params(("parallel", "arbitrary")),
        name="qkv_proj_conv",
    )(x, w_stack, col_scale, u, u, u, u, w_rows, conv_b.reshape(1, ch), ln_g.reshape(1, ch), ln_b.reshape(1, ch))


def _gates_kernel(x_ref, w_ref, bias_ref, gc_ref, gr_ref):
    length = MLSTM_CHUNK
    w = w_ref[...].astype(BF16)
    w = jnp.concatenate([w, jnp.zeros((GATE_LANES - w.shape[0], w.shape[1]), BF16)], axis=0)
    pre = lax.dot_general(x_ref[...], w, (((1,), (1,)), ((), ())), preferred_element_type=F32) + bias_ref[...]
    log_f = jnp.minimum(pre, 0.0) - jnp.log1p(jnp.exp(-jnp.abs(pre)))
    ti = lax.broadcasted_iota(jnp.int32, (length, length), 0)
    si = lax.broadcasted_iota(jnp.int32, (length, length), 1)
    tril = (ti >= si).astype(F32)
    lane = lax.broadcasted_iota(jnp.int32, (length, GATE_LANES), 1)
    for r0 in range(0, x_ref.shape[0], length):
        cum_f = jnp.dot(tril, log_f[r0:r0 + length], preferred_element_type=F32, precision=lax.Precision.HIGHEST)
        gates = jnp.where(lane < C_HEADS, pre[r0:r0 + length], cum_f)
        gc_ref[r0:r0 + length, :] = gates
        gr_ref[:, r0:r0 + length] = gates.T


def _mlstm_gates(x, w_stack_t, layer, gate_row0, bias):
    m, d = x.shape
    length = GATE_CHUNKS_PER_STEP * MLSTM_CHUNK
    gate_rows = 2 * C_HEADS
    gate_block = gate_row0 // gate_rows
    return pl.pallas_call(
        _gates_kernel,
        grid=(m // length,),
        in_specs=[pl.BlockSpec((length, d), lambda i: (i, 0)),
                  pl.BlockSpec((None, gate_rows, d), lambda i: (layer, gate_block, 0)),
                  pl.BlockSpec((1, GATE_LANES), lambda i: (0, 0))],
        out_specs=[pl.BlockSpec((length, GATE_LANES), lambda i: (i, 0)),
                   pl.BlockSpec((GATE_LANES, length), lambda i: (0, i))],
        out_shape=[jax.ShapeDtypeStruct((m, GATE_LANES), F32),
                   jax.ShapeDtypeStruct((GATE_LANES, m), F32)],
        compiler_params=_params(("parallel",)),
        name="mlstm_gates",
    )(x, w_stack_t, bias)


def _mlstm_kernel(dk, dv, q_ref, k_ref, v_ref, o_ref, gc_ref, gr_ref, ng_ref, y_ref, c_ref, n_ref, m_ref):
    length = q_ref.shape[0]

    @pl.when(pl.program_id(1) == 0)
    def _():
        c_ref[...] = jnp.zeros_like(c_ref)
        n_ref[...] = jnp.zeros_like(n_ref)
        m_ref[...] = jnp.zeros_like(m_ref)

    ti = lax.broadcasted_iota(jnp.int32, (length, length), 0)
    si = lax.broadcasted_iota(jnp.int32, (length, length), 1)
    causal = ti >= si
    for h in range(C_HEADS):
        q = q_ref[:, h * dk:(h + 1) * dk]
        k = k_ref[:, h * dk:(h + 1) * dk]
        v = v_ref[:, h * dv:(h + 1) * dv]
        i_row = gr_ref[h:h + 1, :]
        b_row = gr_ref[C_HEADS + h:C_HEADS + h + 1, :]
        i_col = gc_ref[:, h:h + 1]
        b_col = gc_ref[:, C_HEADS + h:C_HEADS + h + 1]
        m_prev = m_ref[h:h + 1, 0:1]
        c_prev = c_ref[h]
        n_prev = n_ref[h:h + 1, :]

        dmat = jnp.where(causal, b_col - b_row + i_row, -jnp.inf)
        inter = b_col + m_prev
        m_t = jnp.maximum(inter, jnp.max(dmat, axis=-1, keepdims=True))
        w = jnp.exp(dmat - m_t)
        a = jnp.exp(inter - m_t)
        qk = lax.dot_general(q, k, (((1,), (1,)), ((), ())), preferred_element_type=F32)
        sqk = qk * w
        num = (a * jnp.dot(q, c_prev.astype(BF16), preferred_element_type=F32)
               + jnp.dot(sqk.astype(BF16), v, preferred_element_type=F32))
        den = (a * jnp.sum(q.astype(F32) * n_prev, axis=-1, keepdims=True)
               + jnp.sum(sqk, axis=-1, keepdims=True))
        hh = num / jnp.maximum(jnp.abs(den), jnp.exp(-m_t))

        b_last = b_col[length - 1:length, :]
        g = b_last - b_col + i_col
        m_new = jnp.maximum(b_last + m_prev, jnp.max(g, axis=0, keepdims=True))
        decay = jnp.exp(b_last + m_prev - m_new)
        kw = k.astype(F32) * jnp.exp(g - m_new)
        c_ref[h] = decay * c_prev + lax.dot_general(kw.astype(BF16), v, (((0,), (0,)), ((), ())),
                                                    preferred_element_type=F32)
        n_ref[h:h + 1, :] = decay * n_prev + jnp.sum(kw, axis=0, keepdims=True)
        m_ref[h:h + 1, :] = jnp.broadcast_to(m_new, (1, m_ref.shape[1]))

        mu = jnp.mean(hh, axis=-1, keepdims=True)
        hc = hh - mu
        var = jnp.mean(hc * hc, axis=-1, keepdims=True)
        hn = hc * lax.rsqrt(var + LN_EPS) * ng_ref[:, h * dv:(h + 1) * dv]
        y_ref[:, h * dv:(h + 1) * dv] = (jax.nn.sigmoid(o_ref[:, h * dv:(h + 1) * dv]) * hn).astype(y_ref.dtype)


def _mlstm(qkv, o, gates_col, gates_row, norm_g, bsz, seq, dk, dv):
    length = MLSTM_CHUNK
    nc = seq // length
    qk_w = C_HEADS * dk
    v_w = C_HEADS * dv
    v_blk = (2 * qk_w) // v_w
    return pl.pallas_call(
        functools.partial(_mlstm_kernel, dk, dv),
        grid=(bsz, nc),
        in_specs=[pl.BlockSpec((length, qk_w), lambda b, c: (b * nc + c, 0)),
                  pl.BlockSpec((length, qk_w), lambda b, c: (b * nc + c, 1)),
                  pl.BlockSpec((length, v_w), lambda b, c: (b * nc + c, v_blk)),
                  pl.BlockSpec((length, v_w), lambda b, c: (b * nc + c, 0)),
                  pl.BlockSpec((length, GATE_LANES), lambda b, c: (b * nc + c, 0)),
                  pl.BlockSpec((GATE_LANES, length), lambda b, c: (0, b * nc + c)),
                  pl.BlockSpec((1, v_w), lambda b, c: (0, 0))],
        out_specs=pl.BlockSpec((length, v_w), lambda b, c: (b * nc + c, 0)),
        out_shape=jax.ShapeDtypeStruct((bsz * seq, v_w), BF16),
        scratch_shapes=[pltpu.VMEM((C_HEADS, dk, dv), F32),
                        pltpu.VMEM((8, dk), F32),
                        pltpu.VMEM((8, GATE_LANES), F32)],
        compiler_params=_params(("parallel", "arbitrary")),
        name="mlstm_chunk",
    )(qkv, qkv, qkv, o, gates_col, gates_row, norm_g)


def _even_mixer(xb, x, w_in, j, conv_w, conv_b, conv_ln_g, conv_ln_b, w_out, ln_g, ln_b, alpha, bsz, seq):
    d = x.shape[1]
    a_width = d // 2
    n_heads = a_width // A_HEAD_DIM
    in_width = w_in.shape[2]
    qkv_w = 3 * a_width
    qkv_scale = jnp.ones((1, qkv_w), F32).at[:, :a_width].set(A_HEAD_DIM ** -0.5)
    u = _matmul(xb, w_in, j, qkv_w, in_width - qkv_w, None, F32)
    qkv, conv = _qkv_proj_conv(xb, w_in, j, qkv_w, qkv_scale, u, seq, conv_w, conv_b, conv_ln_g, conv_ln_b)
    att = _dilated_mixture_attention(qkv, bsz, seq, n_heads)
    return _out_proj_ln([att, conv], w_out, j, x, ln_g.reshape(1, d), ln_b.reshape(1, d), alpha)


def _odd_mixer(xb, x, w_in_t, j, igate_b, fgate_b, norm_g, w_out, ln_g, ln_b, alpha, bsz, seq):
    d = x.shape[1]
    v_w = d
    dv = v_w // C_HEADS
    dk = dv // 2
    qk_w = C_HEADS * dk
    qkv_w = 2 * qk_w + v_w
    qkv_scale = jnp.ones((1, qkv_w), F32).at[:, :qk_w].set(dk ** -0.5)
    qkv = _matmul(xb, w_in_t, j, 0, qkv_w, qkv_scale, BF16, w_is_transposed=True)
    o = _matmul(xb, w_in_t, j, qkv_w, v_w, None, F32, w_is_transposed=True)
    bias = jnp.zeros((1, GATE_LANES), F32).at[0, :C_HEADS].set(igate_b).at[0, C_HEADS:2 * C_HEADS].set(fgate_b)
    gates_col, gates_row = _mlstm_gates(xb, w_in_t, j, qkv_w + v_w, bias)
    y = _mlstm(qkv, o, gates_col, gates_row, norm_g.reshape(1, v_w), bsz, seq, dk, dv)
    return _out_proj_ln([y], w_out, j, x, ln_g.reshape(1, d), ln_b.reshape(1, d), alpha)


def kernel(x, even_w_in, even_conv_w, even_conv_b, even_conv_ln_g, even_conv_ln_b, even_w_out,
           odd_w_in, odd_igate_b, odd_fgate_b, odd_norm_g, odd_w_out, mix_ln_g, mix_ln_b,
           ffn_w1, ffn_w2, ffn_ln_g, ffn_ln_b):
    bsz, seq, d = x.shape
    depth = mix_ln_g.shape[0]
    alpha = (2 * depth) ** 0.25
    xf = x.reshape(bsz * seq, d)
    xb = xf.astype(BF16)
    odd_w_in_t = jnp.swapaxes(odd_w_in, 1, 2)
    for layer in range(depth):
        j = layer // 2
        if layer % 2 == 0:
            xf, xb = _even_mixer(xb, xf, even_w_in, j, even_conv_w[j], even_conv_b[j], even_conv_ln_g[j],
                                 even_conv_ln_b[j], even_w_out, mix_ln_g[layer], mix_ln_b[layer],
                                 alpha, bsz, seq)
        else:
            xf, xb = _odd_mixer(xb, xf, odd_w_in_t, j, odd_igate_b[j], odd_fgate_b[j], odd_norm_g[j],
                                odd_w_out, mix_ln_g[layer], mix_ln_b[layer], alpha, bsz, seq)
        xf, xb = _ffn_ln(xf, ffn_w1, ffn_w2, layer,
                         ffn_ln_g[layer].reshape(1, d), ffn_ln_b[layer].reshape(1, d), alpha)
    return xf.reshape(bsz, seq, d)
```

```python
import functools

import jax
import jax.numpy as jnp
from jax import lax
from jax.experimental import pallas as pl
from jax.experimental.pallas import tpu as pltpu

LN_EPS = 1e-5
DILATED_BRANCHES = ((128, 1), (512, 4), (2048, 16))
BAND_BLOCK = 128
A_HEAD_DIM = 128
ATT_TILE = 2048
ATT_UNROLL = 16
CONV_WIDTH = 31
CONV_HALO = 32
C_HEADS = 4
MLSTM_CHUNK = 256
MLSTM_CHUNKS_PER_STEP = 2
GATE_LANES = 128
GATE_CHUNKS_PER_STEP = 4
SUBLANES = 8
FFN_DMA_CHUNKS = 4
QKV_ROW_CHUNKS = 2
V7X_VMEM_LIMIT = 56 * 1024 * 1024
V7X_VMEM_LIMIT_FFN = 58 * 1024 * 1024

BF16 = jnp.bfloat16
F32 = jnp.float32


def _params(semantics, vmem_limit=V7X_VMEM_LIMIT):
    return pltpu.CompilerParams(dimension_semantics=semantics, vmem_limit_bytes=vmem_limit)


def _layer_norm(z, g, b):
    mu = jnp.mean(z, axis=-1, keepdims=True)
    zc = z - mu
    var = jnp.mean(zc * zc, axis=-1, keepdims=True)
    return zc * lax.rsqrt(var + LN_EPS) * g + b


def _matmul_kernel(w_is_transposed, x_ref, w_ref, s_ref, o_ref, wb_ref):
    @pl.when(pl.program_id(1) == 0)
    def _():
        w = w_ref[...]
        wb_ref[...] = (w.T if w_is_transposed else w).astype(BF16)

    acc = jnp.dot(x_ref[...], wb_ref[...], preferred_element_type=F32)
    o_ref[...] = (acc * s_ref[...]).astype(o_ref.dtype)


def _matmul(x, w_stack, layer, col0, n_cols, col_scale, out_dtype, w_is_transposed=False, tm=1024, tn=1024):
    m, k = x.shape
    cb0 = col0 // tn
    if w_is_transposed:
        w_spec = pl.BlockSpec((None, tn, k), lambda j, i: (layer, cb0 + j, 0))
    else:
        w_spec = pl.BlockSpec((None, k, tn), lambda j, i: (layer, 0, cb0 + j))
    return pl.pallas_call(
        functools.partial(_matmul_kernel, w_is_transposed),
        grid=(n_cols // tn, m // tm),
        in_specs=[pl.BlockSpec((tm, k), lambda j, i: (i, 0)),
                  w_spec,
                  pl.BlockSpec((1, tn), lambda j, i: (0, j))],
        out_specs=pl.BlockSpec((tm, tn), lambda j, i: (i, j)),
        out_shape=jax.ShapeDtypeStruct((m, n_cols), out_dtype),
        scratch_shapes=[pltpu.VMEM((k, tn), BF16)],
        compiler_params=_params(("parallel", "arbitrary")),
        name="proj_matmul",
    )(x, w_stack, col_scale)


def _out_proj_ln_kernel(alpha, n_in, *refs):
    ys = refs[:n_in]
    w_ref, x_ref, g_ref, b_ref, of_ref, ob_ref, wb_ref = refs[n_in:]

    @pl.when(pl.program_id(0) == 0)
    def _():
        wb_ref[...] = w_ref[...].astype(BF16)

    acc = alpha * x_ref[...]
    k0 = 0
    for y_ref in ys:
        kw = y_ref.shape[1]
        acc = acc + jnp.dot(y_ref[...], wb_ref[k0:k0 + kw, :], preferred_element_type=F32)
        k0 += kw
    out = _layer_norm(acc, g_ref[...], b_ref[...])
    of_ref[...] = out
    ob_ref[...] = out.astype(BF16)


def _out_proj_ln(ys, w_stack, layer, x, g, b, alpha, tm=512):
    m, d = x.shape
    n_in = len(ys)
    k = w_stack.shape[1]
    in_specs = ([pl.BlockSpec((tm, y.shape[1]), lambda i: (i, 0)) for y in ys]
                + [pl.BlockSpec((None, k, d), lambda i: (layer, 0, 0), pipeline_mode=pl.Buffered(1)),
                   pl.BlockSpec((tm, d), lambda i: (i, 0)),
                   pl.BlockSpec((1, d), lambda i: (0, 0)),
                   pl.BlockSpec((1, d), lambda i: (0, 0))])
    return pl.pallas_call(
        functools.partial(_out_proj_ln_kernel, alpha, n_in),
        grid=(m // tm,),
        in_specs=in_specs,
        out_specs=[pl.BlockSpec((tm, d), lambda i: (i, 0)),
                   pl.BlockSpec((tm, d), lambda i: (i, 0))],
        out_shape=[jax.ShapeDtypeStruct((m, d), F32), jax.ShapeDtypeStruct((m, d), BF16)],
        scratch_shapes=[pltpu.VMEM((k, d), BF16)],
        compiler_params=_params(("arbitrary",)),
        name="out_proj_ln",
    )(*ys, w_stack, x, g, b)


def _ffn_kernel(alpha, layer, tm, th, n_blocks, x_hbm, w1_hbm, w2_hbm, g_ref, b_ref, of_hbm, ob_hbm,
                x_buf, xb_ref, acc_ref, of_buf, ob_buf, w1_buf, w2_buf, sem_x, sem_w, sem_o):
    i = pl.program_id(0)
    last_tile = pl.num_programs(0) - 1
    chunks = FFN_DMA_CHUNKS

    def w_copies(j, slot):
        col = pl.multiple_of(j * th, th)
        r1 = w1_buf.shape[1] // chunks
        r2 = th // chunks
        copies = []
        for c in range(chunks):
            copies.append(pltpu.make_async_copy(
                w1_hbm.at[layer, pl.ds(c * r1, r1), pl.ds(col, th)],
                w1_buf.at[slot, pl.ds(c * r1, r1), :], sem_w.at[0, slot, c]))
            copies.append(pltpu.make_async_copy(
                w2_hbm.at[layer, pl.ds(col + c * r2, r2), :],
                w2_buf.at[slot, pl.ds(c * r2, r2), :], sem_w.at[1, slot, c]))
        return copies

    def x_copies(tile):
        rows = tm // chunks
        row0 = pl.multiple_of(tile * tm, tm)
        return [pltpu.make_async_copy(x_hbm.at[pl.ds(row0 + c * rows, rows), :],
                                      x_buf.at[pl.ds(c * rows, rows), :], sem_x.at[c]) for c in range(chunks)]

    def out_copies(tile):
        rows = tm // chunks
        row0 = pl.multiple_of(tile * tm, tm)
        copies = []
        for c in range(chunks):
            copies.append(pltpu.make_async_copy(of_buf.at[pl.ds(c * rows, rows), :],
                                                of_hbm.at[pl.ds(row0 + c * rows, rows), :], sem_o.at[0, c]))
            copies.append(pltpu.make_async_copy(ob_buf.at[pl.ds(c * rows, rows), :],
                                                ob_hbm.at[pl.ds(row0 + c * rows, rows), :], sem_o.at[1, c]))
        return copies

    def start(copies):
        for copy in copies:
            copy.start()

    def wait(copies):
        for copy in copies:
            copy.wait()

    @pl.when(i == 0)
    def _():
        start(x_copies(0))
        start(w_copies(0, 0))

    wait(x_copies(i))
    x = x_buf[...]
    xb_ref[...] = x.astype(BF16)
    acc_ref[...] = alpha * x

    @pl.when(i < last_tile)
    def _():
        start(x_copies(i + 1))

    def block(j, slot):
        wait(w_copies(j, slot))
        h = jnp.dot(xb_ref[...], w1_buf[slot].astype(BF16), preferred_element_type=F32)
        h = jnp.maximum(h, 0.0)
        h = (h * h).astype(BF16)
        acc_ref[...] += jnp.dot(h, w2_buf[slot].astype(BF16), preferred_element_type=F32)

    def pair(p, carry):
        j = 2 * p
        start(w_copies(j + 1, 1))
        block(j, 0)

        @pl.when(j + 2 < n_blocks)
        def _():
            start(w_copies(j + 2, 0))

        @pl.when(jnp.logical_and(j + 2 == n_blocks, i < last_tile))
        def _():
            start(w_copies(0, 0))

        block(j + 1, 1)
        return carry

    lax.fori_loop(0, n_blocks // 2, pair, 0)

    @pl.when(i > 0)
    def _():
        wait(out_copies(i - 1))

    out = _layer_norm(acc_ref[...], g_ref[...], b_ref[...])
    of_buf[...] = out
    ob_buf[...] = out.astype(BF16)
    start(out_copies(i))

    @pl.when(i == last_tile)
    def _():
        wait(out_copies(i))


def _ffn_ln(x, w1, w2, layer, g, b, alpha, tm=1024, th=512):
    m, d = x.shape
    f = w1.shape[2]
    n_blocks = f // th
    assert n_blocks % 2 == 0 and m % tm == 0
    hbm = pl.BlockSpec(memory_space=pl.ANY)
    return pl.pallas_call(
        functools.partial(_ffn_kernel, alpha, layer, tm, th, n_blocks),
        grid=(m // tm,),
        in_specs=[hbm, hbm, hbm,
                  pl.BlockSpec((1, d), lambda i: (0, 0)),
                  pl.BlockSpec((1, d), lambda i: (0, 0))],
        out_specs=[hbm, hbm],
        out_shape=[jax.ShapeDtypeStruct((m, d), F32), jax.ShapeDtypeStruct((m, d), BF16)],
        scratch_shapes=[pltpu.VMEM((tm, d), F32),
                        pltpu.VMEM((tm, d), BF16),
                        pltpu.VMEM((tm, d), F32),
                        pltpu.VMEM((tm, d), F32),
                        pltpu.VMEM((tm, d), BF16),
                        pltpu.VMEM((2, d, th), F32),
                        pltpu.VMEM((2, th, d), F32),
                        pltpu.SemaphoreType.DMA((FFN_DMA_CHUNKS,)),
                        pltpu.SemaphoreType.DMA((2, 2, FFN_DMA_CHUNKS)),
                        pltpu.SemaphoreType.DMA((2, FFN_DMA_CHUNKS))],
        compiler_params=_params(("arbitrary",), V7X_VMEM_LIMIT_FFN),
        name="ffn_ln",
    )(x, w1, w2, g, b)


def _kv_rows(dil):
    return dil * BAND_BLOCK + ATT_TILE


def _attn_kernel(q_ref, k_ref, v_ref, slope_ref, out_ref,
                 f_ref, g4_ref, qs_ref, ks_ref, vs_ref, o_ref, l_ref, bias_ref, s_ref, e_ref):
    tile = pl.program_id(2)
    p = BAND_BLOCK
    t_len = ATT_TILE
    qi = lax.broadcasted_iota(jnp.int32, (p, 2 * p), 0)
    kj = lax.broadcasted_iota(jnp.int32, (p, 2 * p), 1)
    dist = p + qi - kj
    valid = jnp.logical_and(dist >= 0, dist <= p)
    in_prev_block = kj < p
    slope = slope_ref[0]
    for bi, (_, dil) in enumerate(DILATED_BRANCHES):
        bias_ref[bi] = jnp.where(valid, -(slope * (dist * dil).astype(F32)), -jnp.inf)

    def rows(start, size, dil):
        return pl.ds(start, size, stride=dil) if dil > 1 else pl.ds(start, size)

    kv_base = [sum(_kv_rows(d) for _, d in DILATED_BRANCHES[:bi]) for bi in range(len(DILATED_BRANCHES))]
    srcs = (q_ref, k_ref, v_ref)
    dil4 = DILATED_BRANCHES[1][1]
    w4 = t_len // dil4
    for x, src in enumerate(srcs):
        f_ref[x] = src[...].astype(F32)
        for r in range(dil4):
            g4_ref[x, r * w4:(r + 1) * w4, :] = f_ref[x, rows(r, w4, dil4), :]

    def residue_rows(x, r, dil):
        if dil == 1:
            return srcs[x][...]
        if dil == dil4:
            return g4_ref[x, r * w4:(r + 1) * w4, :]
        return g4_ref[x, rows((r % dil4) * w4 + r // dil4, t_len // dil, dil // dil4), :]

    nt = (((1,), (1,)), ((), ()))
    for bi, (_, dil) in enumerate(DILATED_BRANCHES):
        wq = t_len // dil
        nq = wq // p
        wk = p + wq
        base = kv_base[bi]

        @pl.when(tile == 0)
        def _(dil=dil, wk=wk, base=base):
            for r in range(dil):
                for dst_ref in (ks_ref, vs_ref):
                    dst_ref[base + r * wk:base + r * wk + p, :] = jnp.zeros((p, A_HEAD_DIM), BF16)

        @pl.when(tile > 0)
        def _(dil=dil, wk=wk, wq=wq, base=base):
            for r in range(dil):
                for dst_ref in (ks_ref, vs_ref):
                    dst_ref[base + r * wk:base + r * wk + p, :] = dst_ref[base + r * wk + wq:base + (r + 1) * wk, :]

        for r in range(dil):
            qs_ref[r * wq:(r + 1) * wq, :] = residue_rows(0, r, dil).astype(BF16)
            ks_ref[base + r * wk + p:base + (r + 1) * wk, :] = residue_rows(1, r, dil).astype(BF16)
            vs_ref[base + r * wk + p:base + (r + 1) * wk, :] = residue_rows(2, r, dil).astype(BF16)

        def group(g, carry, dil=dil, nq=nq, wk=wk, bi=bi, base=base):
            starts = []
            for u in range(ATT_UNROLL):
                idx = g * ATT_UNROLL + u
                r = idx // nq
                n = idx - r * nq
                q0 = pl.multiple_of(idx * p, p)
                k0 = pl.multiple_of(base + r * wk + n * p, p)
                s = lax.dot_general(qs_ref[pl.ds(q0, p), :], ks_ref[pl.ds(k0, 2 * p), :], nt,
                                    preferred_element_type=F32) + bias_ref[bi]
                no_history = jnp.logical_and(tile == 0, n == 0)
                s_ref[u * p:(u + 1) * p, :] = jnp.where(jnp.logical_and(no_history, in_prev_block), -jnp.inf, s)
                starts.append((k0, r + dil * p * n))
            s = s_ref[...]
            m = jnp.max(s, axis=-1, keepdims=True)
            e = jnp.exp(s - m)
            l = jnp.sum(e, axis=-1, keepdims=True)
            e_ref[...] = e.astype(BF16)
            inv_l = 1.0 / l
            lse = m + jnp.log(l)
            for u, (k0, t0) in enumerate(starts):
                o = jnp.dot(e_ref[u * p:(u + 1) * p, :], vs_ref[pl.ds(k0, 2 * p), :], preferred_element_type=F32)
                o_ref[bi, rows(t0, p, dil), :] = o * inv_l[u * p:(u + 1) * p]
                l_ref[bi, rows(t0, p, dil), :] = jnp.broadcast_to(lse[u * p:(u + 1) * p], (p, A_HEAD_DIM))
            return carry

        lax.fori_loop(0, (dil * nq) // ATT_UNROLL, group, 0)

    chunk = 256
    for c0 in range(0, t_len, chunk):
        sl = slice(c0, c0 + chunk)
        l1, l2, l3 = l_ref[0, sl, :], l_ref[1, sl, :], l_ref[2, sl, :]
        mx = jnp.maximum(jnp.maximum(l1, l2), l3)
        e1, e2, e3 = jnp.exp(l1 - mx), jnp.exp(l2 - mx), jnp.exp(l3 - mx)
        mix = (e1 * o_ref[0, sl, :] + e2 * o_ref[1, sl, :] + e3 * o_ref[2, sl, :]) / (e1 + e2 + e3)
        out_ref[sl, :] = mix.astype(out_ref.dtype)


def _dilated_mixture_attention(proj, bsz, seq, n_heads):
    m = proj.shape[0]
    hd = A_HEAD_DIM
    t_len = ATT_TILE
    nt = seq // t_len
    slopes = 2.0 ** (-8.0 * jnp.arange(1, n_heads + 1, dtype=F32) / n_heads)
    slopes = jnp.broadcast_to(slopes[:, None, None], (n_heads, 1, 2 * BAND_BLOCK))

    kv_rows = sum(_kv_rows(dil) for _, dil in DILATED_BRANCHES)

    def head_block(col):
        return pl.BlockSpec((t_len, hd), lambda b, h, t: (b * nt + t, col * n_heads + h))

    return pl.pallas_call(
        _attn_kernel,
        grid=(bsz, n_heads, nt),
        in_specs=[head_block(0), head_block(1), head_block(2),
                  pl.BlockSpec((1, 1, 2 * BAND_BLOCK), lambda b, h, t: (h, 0, 0))],
        out_specs=pl.BlockSpec((t_len, hd), lambda b, h, t: (b * nt + t, h)),
        out_shape=jax.ShapeDtypeStruct((m, n_heads * hd), BF16),
        scratch_shapes=[pltpu.VMEM((3, t_len, hd), F32),
                        pltpu.VMEM((3, t_len, hd), F32),
                        pltpu.VMEM((t_len, hd), BF16),
                        pltpu.VMEM((kv_rows, hd), BF16),
                        pltpu.VMEM((kv_rows, hd), BF16),
                        pltpu.VMEM((3, t_len, hd), F32),
                        pltpu.VMEM((3, t_len, hd), F32),
                        pltpu.VMEM((3, BAND_BLOCK, 2 * BAND_BLOCK), F32),
                        pltpu.VMEM((ATT_UNROLL * BAND_BLOCK, 2 * BAND_BLOCK), F32),
                        pltpu.VMEM((ATT_UNROLL * BAND_BLOCK, 2 * BAND_BLOCK), BF16)],
        compiler_params=_params(("parallel", "parallel", "arbitrary")),
        name="dilated_attention",
    )(proj, proj, proj, slopes)


def _conv_tile(has_history, a_ref, g_ref, ha_ref, hg_ref, w_ref, cb_ref, lg_ref, lb_ref, o_ref, h_ref, hs_ref, c_ref,
               n_parts, side_work):
    ts, ch = a_ref.shape
    hist = ha_ref[...] * jax.nn.sigmoid(hg_ref[...])
    h_ref[0:CONV_HALO, :] = jnp.where(has_history, hist, 0.0)
    h_ref[CONV_HALO:, :] = a_ref[...] * jax.nn.sigmoid(g_ref[...])
    first = CONV_HALO - (CONV_WIDTH - 1)
    h_rows = h_ref.shape[0]
    for s in range(1, SUBLANES):
        hs_ref[s - 1, SUBLANES - s:SUBLANES - s + h_rows, :] = h_ref[...]
    rows, lanes = 32, 512
    groups = rows // SUBLANES
    part_rows = ts // n_parts

    def part(p, carry):
        base = pl.multiple_of(p * part_rows, part_rows)
        for r0 in range(0, part_rows, rows):
            for c0 in range(0, ch, lanes):
                acc = jnp.broadcast_to(cb_ref[:, c0:c0 + lanes], (groups, SUBLANES, lanes))
                for j in range(CONV_WIDTH):
                    a8, s = (first + j) // SUBLANES * SUBLANES, (first + j) % SUBLANES
                    if s == 0:
                        tap = h_ref[pl.ds(base + r0 + a8, rows), c0:c0 + lanes]
                    else:
                        tap = hs_ref[s - 1, pl.ds(base + SUBLANES + r0 + a8, rows), c0:c0 + lanes]
                    acc = acc + w_ref[j, :, c0:c0 + lanes] * tap.reshape(groups, SUBLANES, lanes)
                c_ref[pl.ds(base + r0, rows), c0:c0 + lanes] = acc.reshape(rows, lanes)
        side_work(p)
        return carry

    lax.fori_loop(0, n_parts, part, 0)
    y = _layer_norm(c_ref[...], lg_ref[...], lb_ref[...])
    o_ref[...] = (y * jax.nn.sigmoid(y)).astype(o_ref.dtype)


def _qkv_conv_kernel(n_row_tiles, tiles_per_seq, x_ref, w_ref, s_ref, a_ref, g_ref, ha_ref, hg_ref, cw_ref, cb_ref,
                     lg_ref, lb_ref, o_ref, conv_ref, wb_ref, h_ref, hs_ref, c_ref):
    j, i = pl.program_id(0), pl.program_id(1)

    @pl.when(i == 0)
    def _():
        wb_ref[...] = w_ref[...].astype(BF16)

    conv_tile = j * n_row_tiles + i
    rm = x_ref.shape[0] // QKV_ROW_CHUNKS

    def matmul_chunk(c):
        r0 = pl.multiple_of(c * rm, rm)
        acc = jnp.dot(x_ref[pl.ds(r0, rm), :], wb_ref[...], preferred_element_type=F32)
        o_ref[pl.ds(r0, rm), :] = (acc * s_ref[...]).astype(o_ref.dtype)

    _conv_tile(conv_tile % tiles_per_seq != 0, a_ref, g_ref, ha_ref, hg_ref, cw_ref, cb_ref, lg_ref, lb_ref,
               conv_ref, h_ref, hs_ref, c_ref, QKV_ROW_CHUNKS, matmul_chunk)


def _qkv_proj_conv(x, w_stack, layer, n_cols, col_scale, u, seq, conv_w, conv_b, ln_g, ln_b, tm=1024, tn=768):
    m, k = x.shape
    ch = conv_w.shape[1]
    n_col_blocks, n_row_tiles = n_cols // tn, m // tm
    ts = m // (n_col_blocks * n_row_tiles)
    assert n_cols % tn == 0 and m % tm == 0 and ts % CONV_HALO == 0 and seq % ts == 0
    per = ts // CONV_HALO
    w_rows = jnp.broadcast_to(conv_w[:, None, :], (CONV_WIDTH, SUBLANES, ch))

    def tile(col):
        return pl.BlockSpec((ts, ch), lambda j, i: (j * n_row_tiles + i, col))

    def halo(col):
        return pl.BlockSpec((CONV_HALO, ch), lambda j, i: (jnp.maximum((j * n_row_tiles + i) * per - 1, 0), col))

    def const(shape):
        return pl.BlockSpec(shape, lambda j, i: (0,) * len(shape))

    return pl.pallas_call(
        functools.partial(_qkv_conv_kernel, n_row_tiles, seq // ts),
        grid=(n_col_blocks, n_row_tiles),
        in_specs=[pl.BlockSpec((tm, k), lambda j, i: (i, 0)),
                  pl.BlockSpec((None, k, tn), lambda j, i: (layer, 0, j)),
                  pl.BlockSpec((1, tn), lambda j, i: (0, j)),
                  tile(0), tile(1), halo(0), halo(1),
                  const((CONV_WIDTH, SUBLANES, ch)), const((1, ch)), const((1, ch)), const((1, ch))],
        out_specs=[pl.BlockSpec((tm, tn), lambda j, i: (i, j)),
                   pl.BlockSpec((ts, ch), lambda j, i: (j * n_row_tiles + i, 0))],
        out_shape=[jax.ShapeDtypeStruct((m, n_cols), BF16), jax.ShapeDtypeStruct((m, ch), BF16)],
        scratch_shapes=[pltpu.VMEM((k, tn), BF16),
                        pltpu.VMEM((ts + CONV_HALO, ch), F32),
                        pltpu.VMEM((SUBLANES - 1, ts + CONV_HALO + SUBLANES, ch), F32),
                        pltpu.VMEM((ts, ch), F32)],
        compiler_params=_params(("parallel", "arbitrary")),
        name="qkv_proj_conv",
    )(x, w_stack, col_scale, u, u, u, u, w_rows, conv_b.reshape(1, ch), ln_g.reshape(1, ch), ln_b.reshape(1, ch))


def _gates_kernel(x_ref, w_ref, bias_ref, gc_ref, gr_ref):
    length = MLSTM_CHUNK
    w = w_ref[...].astype(BF16)
    w = jnp.concatenate([w, jnp.zeros((GATE_LANES - w.shape[0], w.shape[1]), BF16)], axis=0)
    pre = lax.dot_general(x_ref[...], w, (((1,), (1,)), ((), ())), preferred_element_type=F32) + bias_ref[...]
    log_f = jnp.minimum(pre, 0.0) - jnp.log1p(jnp.exp(-jnp.abs(pre)))
    ti = lax.broadcasted_iota(jnp.int32, (length, length), 0)
    si = lax.broadcasted_iota(jnp.int32, (length, length), 1)
    tril = (ti >= si).astype(F32)
    lane = lax.broadcasted_iota(jnp.int32, (length, GATE_LANES), 1)
    for r0 in range(0, x_ref.shape[0], length):
        cum_f = jnp.dot(tril, log_f[r0:r0 + length], preferred_element_type=F32, precision=lax.Precision.HIGHEST)
        gates = jnp.where(lane < C_HEADS, pre[r0:r0 + length], cum_f)
        gc_ref[r0:r0 + length, :] = gates
        gr_ref[:, r0:r0 + length] = gates.T


def _mlstm_gates(x, w_stack_t, layer, gate_row0, bias):
    m, d = x.shape
    length = GATE_CHUNKS_PER_STEP * MLSTM_CHUNK
    gate_rows = 2 * C_HEADS
    gate_block = gate_row0 // gate_rows
    return pl.pallas_call(
        _gates_kernel,
        grid=(m // length,),
        in_specs=[pl.BlockSpec((length, d), lambda i: (i, 0)),
                  pl.BlockSpec((None, gate_rows, d), lambda i: (layer, gate_block, 0)),
                  pl.BlockSpec((1, GATE_LANES), lambda i: (0, 0))],
        out_specs=[pl.BlockSpec((length, GATE_LANES), lambda i: (i, 0)),
                   pl.BlockSpec((GATE_LANES, length), lambda i: (0, i))],
        out_shape=[jax.ShapeDtypeStruct((m, GATE_LANES), F32),
                   jax.ShapeDtypeStruct((GATE_LANES, m), F32)],
        compiler_params=_params(("parallel",)),
        name="mlstm_gates",
    )(x, w_stack_t, bias)


def _mlstm_kernel(dk, dv, q_ref, k_ref, v_ref, o_ref, gc_ref, gr_ref, ng_ref, y_ref, c_ref, n_ref, m_ref):
    @pl.when(pl.program_id(1) == 0)
    def _():
        c_ref[...] = jnp.zeros_like(c_ref)
        n_ref[...] = jnp.zeros_like(n_ref)
        m_ref[...] = jnp.zeros_like(m_ref)

    for c in range(MLSTM_CHUNKS_PER_STEP):
        rs = pl.ds(c * MLSTM_CHUNK, MLSTM_CHUNK)
        _mlstm_chunk(dk, dv, q_ref.at[rs], k_ref.at[rs], v_ref.at[rs], o_ref.at[rs], gc_ref.at[rs], gr_ref.at[:, rs],
                     ng_ref, y_ref.at[rs], c_ref, n_ref, m_ref)


def _mlstm_chunk(dk, dv, q_ref, k_ref, v_ref, o_ref, gc_ref, gr_ref, ng_ref, y_ref, c_ref, n_ref, m_ref):
    length = q_ref.shape[0]
    ti = lax.broadcasted_iota(jnp.int32, (length, length), 0)
    si = lax.broadcasted_iota(jnp.int32, (length, length), 1)
    causal = ti >= si
    for h in range(C_HEADS):
        q = q_ref[:, h * dk:(h + 1) * dk]
        k = k_ref[:, h * dk:(h + 1) * dk]
        v = v_ref[:, h * dv:(h + 1) * dv]
        i_row = gr_ref[h:h + 1, :]
        b_row = gr_ref[C_HEADS + h:C_HEADS + h + 1, :]
        i_col = gc_ref[:, h:h + 1]
        b_col = gc_ref[:, C_HEADS + h:C_HEADS + h + 1]
        m_prev = m_ref[h:h + 1, 0:1]
        c_prev = c_ref[h]
        n_prev = n_ref[h:h + 1, :]

        dmat = jnp.where(causal, b_col - b_row + i_row, -jnp.inf)
        inter = b_col + m_prev
        m_t = jnp.maximum(inter, jnp.max(dmat, axis=-1, keepdims=True))
        w = jnp.exp(dmat - m_t)
        a = jnp.exp(inter - m_t)
        qk = lax.dot_general(q, k, (((1,), (1,)), ((), ())), preferred_element_type=F32)
        sqk = qk * w
        num = (a * jnp.dot(q, c_prev.astype(BF16), preferred_element_type=F32)
               + jnp.dot(sqk.astype(BF16), v, preferred_element_type=F32))
        den = (a * jnp.sum(q.astype(F32) * n_prev, axis=-1, keepdims=True)
               + jnp.sum(sqk, axis=-1, keepdims=True))
        hh = num / jnp.maximum(jnp.abs(den), jnp.exp(-m_t))

        b_last = b_col[length - 1:length, :]
        g = b_last - b_col + i_col
        m_new = jnp.maximum(b_last + m_prev, jnp.max(g, axis=0, keepdims=True))
        decay = jnp.exp(b_last + m_prev - m_new)
        kw = k.astype(F32) * jnp.exp(g - m_new)
        c_ref[h] = decay * c_prev + lax.dot_general(kw.astype(BF16), v, (((0,), (0,)), ((), ())),
                                                    preferred_element_type=F32)
        n_ref[h:h + 1, :] = decay * n_prev + jnp.sum(kw, axis=0, keepdims=True)
        m_ref[h:h + 1, :] = jnp.broadcast_to(m_new, (1, m_ref.shape[1]))

        mu = jnp.mean(hh, axis=-1, keepdims=True)
        hc = hh - mu
        var = jnp.mean(hc * hc, axis=-1, keepdims=True)
        hn = hc * lax.rsqrt(var + LN_EPS) * ng_ref[:, h * dv:(h + 1) * dv]
        y_ref[:, h * dv:(h + 1) * dv] = (jax.nn.sigmoid(o_ref[:, h * dv:(h + 1) * dv]) * hn).astype(y_ref.dtype)


def _mlstm(qkv, o, gates_col, gates_row, norm_g, bsz, seq, dk, dv):
    length = MLSTM_CHUNKS_PER_STEP * MLSTM_CHUNK
    nc = seq // length
    qk_w = C_HEADS * dk
    v_w = C_HEADS * dv
    v_blk = (2 * qk_w) // v_w
    return pl.pallas_call(
        functools.partial(_mlstm_kernel, dk, dv),
        grid=(bsz, nc),
        in_specs=[pl.BlockSpec((length, qk_w), lambda b, c: (b * nc + c, 0)),
                  pl.BlockSpec((length, qk_w), lambda b, c: (b * nc + c, 1)),
                  pl.BlockSpec((length, v_w), lambda b, c: (b * nc + c, v_blk)),
                  pl.BlockSpec((length, v_w), lambda b, c: (b * nc + c, 0)),
                  pl.BlockSpec((length, GATE_LANES), lambda b, c: (b * nc + c, 0)),
                  pl.BlockSpec((GATE_LANES, length), lambda b, c: (0, b * nc + c)),
                  pl.BlockSpec((1, v_w), lambda b, c: (0, 0))],
        out_specs=pl.BlockSpec((length, v_w), lambda b, c: (b * nc + c, 0)),
        out_shape=jax.ShapeDtypeStruct((bsz * seq, v_w), BF16),
        scratch_shapes=[pltpu.VMEM((C_HEADS, dk, dv), F32),
                        pltpu.VMEM((8, dk), F32),
                        pltpu.VMEM((8, GATE_LANES), F32)],
        compiler_params=_params(("parallel", "arbitrary")),
        name="mlstm_chunk",
    )(qkv, qkv, qkv, o, gates_col, gates_row, norm_g)


def _even_mixer(xb, x, w_in, j, conv_w, conv_b, conv_ln_g, conv_ln_b, w_out, ln_g, ln_b, alpha, bsz, seq):
    d = x.shape[1]
    a_width = d // 2
    n_heads = a_width // A_HEAD_DIM
    in_width = w_in.shape[2]
    qkv_w = 3 * a_width
    qkv_scale = jnp.ones((1, qkv_w), F32).at[:, :a_width].set(A_HEAD_DIM ** -0.5)
    u = _matmul(xb, w_in, j, qkv_w, in_width - qkv_w, jnp.ones((1, in_width - qkv_w), F32), F32)
    qkv, conv = _qkv_proj_conv(xb, w_in, j, qkv_w, qkv_scale, u, seq, conv_w, conv_b, conv_ln_g, conv_ln_b)
    att = _dilated_mixture_attention(qkv, bsz, seq, n_heads)
    return _out_proj_ln([att, conv], w_out, j, x, ln_g.reshape(1, d), ln_b.reshape(1, d), alpha)


def _odd_mixer(xb, x, w_in_t, j, igate_b, fgate_b, norm_g, w_out, ln_g, ln_b, alpha, bsz, seq):
    d = x.shape[1]
    v_w = d
    dv = v_w // C_HEADS
    dk = dv // 2
    qk_w = C_HEADS * dk
    qkv_w = 2 * qk_w + v_w
    qkv_scale = jnp.ones((1, qkv_w), F32).at[:, :qk_w].set(dk ** -0.5)
    qkv = _matmul(xb, w_in_t, j, 0, qkv_w, qkv_scale, BF16, w_is_transposed=True)
    o = _matmul(xb, w_in_t, j, qkv_w, v_w, jnp.ones((1, v_w), F32), F32, w_is_transposed=True)
    bias = jnp.zeros((1, GATE_LANES), F32).at[0, :C_HEADS].set(igate_b).at[0, C_HEADS:2 * C_HEADS].set(fgate_b)
    gates_col, gates_row = _mlstm_gates(xb, w_in_t, j, qkv_w + v_w, bias)
    y = _mlstm(qkv, o, gates_col, gates_row, norm_g.reshape(1, v_w), bsz, seq, dk, dv)
    return _out_proj_ln([y], w_out, j, x, ln_g.reshape(1, d), ln_b.reshape(1, d), alpha)


def kernel(x, even_w_in, even_conv_w, even_conv_b, even_conv_ln_g, even_conv_ln_b, even_w_out,
           odd_w_in, odd_igate_b, odd_fgate_b, odd_norm_g, odd_w_out, mix_ln_g, mix_ln_b,
           ffn_w1, ffn_w2, ffn_ln_g, ffn_ln_b):
    bsz, seq, d = x.shape
    depth = mix_ln_g.shape[0]
    alpha = (2 * depth) ** 0.25
    xf = x.reshape(bsz * seq, d)
    xb = xf.astype(BF16)
    odd_w_in_t = jnp.swapaxes(odd_w_in, 1, 2)
    for layer in range(depth):
        j = layer // 2
        if layer % 2 == 0:
            xf, xb = _even_mixer(xb, xf, even_w_in, j, even_conv_w[j], even_conv_b[j], even_conv_ln_g[j],
                                 even_conv_ln_b[j], even_w_out, mix_ln_g[layer], mix_ln_b[layer],
                                 alpha, bsz, seq)
        else:
            xf, xb = _odd_mixer(xb, xf, odd_w_in_t, j, odd_igate_b[j], odd_fgate_b[j], odd_norm_g[j],
                                odd_w_out, mix_ln_g[layer], mix_ln_b[layer], alpha, bsz, seq)
        xf, xb = _ffn_ln(xf, ffn_w1, ffn_w2, layer,
                         ffn_ln_g[layer].reshape(1, d), ffn_ln_b[layer].reshape(1, d), alpha)
    return xf.reshape(bsz, seq, d)
```

```python
import functools

import jax
import jax.numpy as jnp
from jax import lax
from jax.experimental import pallas as pl
from jax.experimental.pallas import tpu as pltpu

LN_EPS = 1e-5
DILATED_BRANCHES = ((128, 1), (512, 4), (2048, 16))
BAND_BLOCK = 128
A_HEAD_DIM = 128
ATT_TILE = 2048
ATT_UNROLL = 16
CONV_WIDTH = 31
CONV_HALO = 32
C_HEADS = 4
MLSTM_CHUNK = 256
GATE_LANES = 128
GATE_CHUNKS_PER_STEP = 4
SUBLANES = 8
FFN_DMA_CHUNKS = 4
QKV_ROW_CHUNKS = 2
V7X_VMEM_LIMIT = 56 * 1024 * 1024
V7X_VMEM_LIMIT_FFN = 58 * 1024 * 1024

BF16 = jnp.bfloat16
F32 = jnp.float32


def _params(semantics, vmem_limit=V7X_VMEM_LIMIT):
    return pltpu.CompilerParams(dimension_semantics=semantics, vmem_limit_bytes=vmem_limit)


def _layer_norm(z, g, b):
    mu = jnp.mean(z, axis=-1, keepdims=True)
    zc = z - mu
    var = jnp.mean(zc * zc, axis=-1, keepdims=True)
    return zc * lax.rsqrt(var + LN_EPS) * g + b


def _matmul_kernel(w_is_transposed, x_ref, w_ref, s_ref, o_ref, wb_ref):
    @pl.when(pl.program_id(1) == 0)
    def _():
        w = w_ref[...]
        wb_ref[...] = (w.T if w_is_transposed else w).astype(BF16)

    acc = jnp.dot(x_ref[...], wb_ref[...], preferred_element_type=F32)
    o_ref[...] = (acc * s_ref[...]).astype(o_ref.dtype)


def _matmul(x, w_stack, layer, col0, n_cols, col_scale, out_dtype, w_is_transposed=False, tm=1024, tn=1024):
    m, k = x.shape
    cb0 = col0 // tn
    if w_is_transposed:
        w_spec = pl.BlockSpec((None, tn, k), lambda j, i: (layer, cb0 + j, 0))
    else:
        w_spec = pl.BlockSpec((None, k, tn), lambda j, i: (layer, 0, cb0 + j))
    return pl.pallas_call(
        functools.partial(_matmul_kernel, w_is_transposed),
        grid=(n_cols // tn, m // tm),
        in_specs=[pl.BlockSpec((tm, k), lambda j, i: (i, 0)),
                  w_spec,
                  pl.BlockSpec((1, tn), lambda j, i: (0, j))],
        out_specs=pl.BlockSpec((tm, tn), lambda j, i: (i, j)),
        out_shape=jax.ShapeDtypeStruct((m, n_cols), out_dtype),
        scratch_shapes=[pltpu.VMEM((k, tn), BF16)],
        compiler_params=_params(("parallel", "arbitrary")),
        name="proj_matmul",
    )(x, w_stack, col_scale)


def _out_proj_ln_kernel(alpha, n_in, *refs):
    ys = refs[:n_in]
    w_ref, x_ref, g_ref, b_ref, of_ref, ob_ref, wb_ref = refs[n_in:]

    @pl.when(pl.program_id(0) == 0)
    def _():
        wb_ref[...] = w_ref[...].astype(BF16)

    acc = alpha * x_ref[...]
    k0 = 0
    for y_ref in ys:
        kw = y_ref.shape[1]
        acc = acc + jnp.dot(y_ref[...], wb_ref[k0:k0 + kw, :], preferred_element_type=F32)
        k0 += kw
    out = _layer_norm(acc, g_ref[...], b_ref[...])
    of_ref[...] = out
    ob_ref[...] = out.astype(BF16)


def _out_proj_ln(ys, w_stack, layer, x, g, b, alpha, tm=512):
    m, d = x.shape
    n_in = len(ys)
    k = w_stack.shape[1]
    in_specs = ([pl.BlockSpec((tm, y.shape[1]), lambda i: (i, 0)) for y in ys]
                + [pl.BlockSpec((None, k, d), lambda i: (layer, 0, 0), pipeline_mode=pl.Buffered(1)),
                   pl.BlockSpec((tm, d), lambda i: (i, 0)),
                   pl.BlockSpec((1, d), lambda i: (0, 0)),
                   pl.BlockSpec((1, d), lambda i: (0, 0))])
    return pl.pallas_call(
        functools.partial(_out_proj_ln_kernel, alpha, n_in),
        grid=(m // tm,),
        in_specs=in_specs,
        out_specs=[pl.BlockSpec((tm, d), lambda i: (i, 0)),
                   pl.BlockSpec((tm, d), lambda i: (i, 0))],
        out_shape=[jax.ShapeDtypeStruct((m, d), F32), jax.ShapeDtypeStruct((m, d), BF16)],
        scratch_shapes=[pltpu.VMEM((k, d), BF16)],
        compiler_params=_params(("arbitrary",)),
        name="out_proj_ln",
    )(*ys, w_stack, x, g, b)


def _ffn_kernel(alpha, layer, tm, th, n_blocks, x_hbm, w1_hbm, w2_hbm, g_ref, b_ref, of_hbm, ob_hbm,
                x_buf, xb_ref, acc_ref, of_buf, ob_buf, w1_buf, w2_buf, sem_x, sem_w, sem_o):
    i = pl.program_id(0)
    last_tile = pl.num_programs(0) - 1
    chunks = FFN_DMA_CHUNKS

    def w_copies(j, slot):
        col = pl.multiple_of(j * th, th)
        r1 = w1_buf.shape[1] // chunks
        r2 = th // chunks
        copies = []
        for c in range(chunks):
            copies.append(pltpu.make_async_copy(
                w1_hbm.at[layer, pl.ds(c * r1, r1), pl.ds(col, th)],
                w1_buf.at[slot, pl.ds(c * r1, r1), :], sem_w.at[0, slot, c]))
            copies.append(pltpu.make_async_copy(
                w2_hbm.at[layer, pl.ds(col + c * r2, r2), :],
                w2_buf.at[slot, pl.ds(c * r2, r2), :], sem_w.at[1, slot, c]))
        return copies

    def x_copies(tile):
        rows = tm // chunks
        row0 = pl.multiple_of(tile * tm, tm)
        return [pltpu.make_async_copy(x_hbm.at[pl.ds(row0 + c * rows, rows), :],
                                      x_buf.at[pl.ds(c * rows, rows), :], sem_x.at[c]) for c in range(chunks)]

    def out_copies(tile):
        rows = tm // chunks
        row0 = pl.multiple_of(tile * tm, tm)
        copies = []
        for c in range(chunks):
            copies.append(pltpu.make_async_copy(of_buf.at[pl.ds(c * rows, rows), :],
                                                of_hbm.at[pl.ds(row0 + c * rows, rows), :], sem_o.at[0, c]))
            copies.append(pltpu.make_async_copy(ob_buf.at[pl.ds(c * rows, rows), :],
                                                ob_hbm.at[pl.ds(row0 + c * rows, rows), :], sem_o.at[1, c]))
        return copies

    def start(copies):
        for copy in copies:
            copy.start()

    def wait(copies):
        for copy in copies:
            copy.wait()

    @pl.when(i == 0)
    def _():
        start(x_copies(0))
        start(w_copies(0, 0))

    wait(x_copies(i))
    x = x_buf[...]
    xb_ref[...] = x.astype(BF16)
    acc_ref[...] = alpha * x

    @pl.when(i < last_tile)
    def _():
        start(x_copies(i + 1))

    def block(j, slot):
        wait(w_copies(j, slot))
        h = jnp.dot(xb_ref[...], w1_buf[slot].astype(BF16), preferred_element_type=F32)
        h = jnp.maximum(h, 0.0)
        h = (h * h).astype(BF16)
        acc_ref[...] += jnp.dot(h, w2_buf[slot].astype(BF16), preferred_element_type=F32)

    def pair(p, carry):
        j = 2 * p
        start(w_copies(j + 1, 1))
        block(j, 0)

        @pl.when(j + 2 < n_blocks)
        def _():
            start(w_copies(j + 2, 0))

        @pl.when(jnp.logical_and(j + 2 == n_blocks, i < last_tile))
        def _():
            start(w_copies(0, 0))

        block(j + 1, 1)
        return carry

    lax.fori_loop(0, n_blocks // 2, pair, 0)

    @pl.when(i > 0)
    def _():
        wait(out_copies(i - 1))

    out = _layer_norm(acc_ref[...], g_ref[...], b_ref[...])
    of_buf[...] = out
    ob_buf[...] = out.astype(BF16)
    start(out_copies(i))

    @pl.when(i == last_tile)
    def _():
        wait(out_copies(i))


def _ffn_ln(x, w1, w2, layer, g, b, alpha, tm=1024, th=512):
    m, d = x.shape
    f = w1.shape[2]
    n_blocks = f // th
    assert n_blocks % 2 == 0 and m % tm == 0
    hbm = pl.BlockSpec(memory_space=pl.ANY)
    return pl.pallas_call(
        functools.partial(_ffn_kernel, alpha, layer, tm, th, n_blocks),
        grid=(m // tm,),
        in_specs=[hbm, hbm, hbm,
                  pl.BlockSpec((1, d), lambda i: (0, 0)),
                  pl.BlockSpec((1, d), lambda i: (0, 0))],
        out_specs=[hbm, hbm],
        out_shape=[jax.ShapeDtypeStruct((m, d), F32), jax.ShapeDtypeStruct((m, d), BF16)],
        scratch_shapes=[pltpu.VMEM((tm, d), F32),
                        pltpu.VMEM((tm, d), BF16),
                        pltpu.VMEM((tm, d), F32),
                        pltpu.VMEM((tm, d), F32),
                        pltpu.VMEM((tm, d), BF16),
                        pltpu.VMEM((2, d, th), F32),
                        pltpu.VMEM((2, th, d), F32),
                        pltpu.SemaphoreType.DMA((FFN_DMA_CHUNKS,)),
                        pltpu.SemaphoreType.DMA((2, 2, FFN_DMA_CHUNKS)),
                        pltpu.SemaphoreType.DMA((2, FFN_DMA_CHUNKS))],
        compiler_params=_params(("arbitrary",), V7X_VMEM_LIMIT_FFN),
        name="ffn_ln",
    )(x, w1, w2, g, b)


def _kv_rows(dil):
    return dil * BAND_BLOCK + ATT_TILE


def _attn_kernel(q_ref, k_ref, v_ref, slope_ref, out_ref,
                 f_ref, g4_ref, qs_ref, ks_ref, vs_ref, o_ref, l_ref, bias_ref, s_ref, e_ref):
    tile = pl.program_id(2)
    p = BAND_BLOCK
    t_len = ATT_TILE
    qi = lax.broadcasted_iota(jnp.int32, (p, 2 * p), 0)
    kj = lax.broadcasted_iota(jnp.int32, (p, 2 * p), 1)
    dist = p + qi - kj
    valid = jnp.logical_and(dist >= 0, dist <= p)
    in_prev_block = kj < p
    slope = slope_ref[0]
    for bi, (_, dil) in enumerate(DILATED_BRANCHES):
        bias_ref[bi] = jnp.where(valid, -(slope * (dist * dil).astype(F32)), -jnp.inf)

    def rows(start, size, dil):
        return pl.ds(start, size, stride=dil) if dil > 1 else pl.ds(start, size)

    kv_base = [sum(_kv_rows(d) for _, d in DILATED_BRANCHES[:bi]) for bi in range(len(DILATED_BRANCHES))]
    srcs = (q_ref, k_ref, v_ref)
    dil4 = DILATED_BRANCHES[1][1]
    w4 = t_len // dil4
    for x, src in enumerate(srcs):
        f_ref[x] = src[...].astype(F32)
        for r in range(dil4):
            g4_ref[x, r * w4:(r + 1) * w4, :] = f_ref[x, rows(r, w4, dil4), :]

    def residue_rows(x, r, dil):
        if dil == 1:
            return srcs[x][...]
        if dil == dil4:
            return g4_ref[x, r * w4:(r + 1) * w4, :]
        return g4_ref[x, rows((r % dil4) * w4 + r // dil4, t_len // dil, dil // dil4), :]

    nt = (((1,), (1,)), ((), ()))
    for bi, (_, dil) in enumerate(DILATED_BRANCHES):
        wq = t_len // dil
        nq = wq // p
        wk = p + wq
        base = kv_base[bi]

        @pl.when(tile == 0)
        def _(dil=dil, wk=wk, base=base):
            for r in range(dil):
                for dst_ref in (ks_ref, vs_ref):
                    dst_ref[base + r * wk:base + r * wk + p, :] = jnp.zeros((p, A_HEAD_DIM), BF16)

        @pl.when(tile > 0)
        def _(dil=dil, wk=wk, wq=wq, base=base):
            for r in range(dil):
                for dst_ref in (ks_ref, vs_ref):
                    dst_ref[base + r * wk:base + r * wk + p, :] = dst_ref[base + r * wk + wq:base + (r + 1) * wk, :]

        for r in range(dil):
            qs_ref[r * wq:(r + 1) * wq, :] = residue_rows(0, r, dil).astype(BF16)
            ks_ref[base + r * wk + p:base + (r + 1) * wk, :] = residue_rows(1, r, dil).astype(BF16)
            vs_ref[base + r * wk + p:base + (r + 1) * wk, :] = residue_rows(2, r, dil).astype(BF16)

        def group(g, carry, dil=dil, nq=nq, wk=wk, bi=bi, base=base):
            starts = []
            for u in range(ATT_UNROLL):
                idx = g * ATT_UNROLL + u
                r = idx // nq
                n = idx - r * nq
                q0 = pl.multiple_of(idx * p, p)
                k0 = pl.multiple_of(base + r * wk + n * p, p)
                s = lax.dot_general(qs_ref[pl.ds(q0, p), :], ks_ref[pl.ds(k0, 2 * p), :], nt,
                                    preferred_element_type=F32) + bias_ref[bi]
                no_history = jnp.logical_and(tile == 0, n == 0)
                s_ref[u * p:(u + 1) * p, :] = jnp.where(jnp.logical_and(no_history, in_prev_block), -jnp.inf, s)
                starts.append((k0, r + dil * p * n))
            s = s_ref[...]
            m = jnp.max(s, axis=-1, keepdims=True)
            e = jnp.exp(s - m)
            l = jnp.sum(e, axis=-1, keepdims=True)
            e_ref[...] = e.astype(BF16)
            inv_l = 1.0 / l
            lse = m + jnp.log(l)
            for u, (k0, t0) in enumerate(starts):
                o = jnp.dot(e_ref[u * p:(u + 1) * p, :], vs_ref[pl.ds(k0, 2 * p), :], preferred_element_type=F32)
                o_ref[bi, rows(t0, p, dil), :] = o * inv_l[u * p:(u + 1) * p]
                l_ref[bi, rows(t0, p, dil), :] = jnp.broadcast_to(lse[u * p:(u + 1) * p], (p, A_HEAD_DIM))
            return carry

        lax.fori_loop(0, (dil * nq) // ATT_UNROLL, group, 0)

    chunk = 256
    for c0 in range(0, t_len, chunk):
        sl = slice(c0, c0 + chunk)
        l1, l2, l3 = l_ref[0, sl, :], l_ref[1, sl, :], l_ref[2, sl, :]
        mx = jnp.maximum(jnp.maximum(l1, l2), l3)
        e1, e2, e3 = jnp.exp(l1 - mx), jnp.exp(l2 - mx), jnp.exp(l3 - mx)
        mix = (e1 * o_ref[0, sl, :] + e2 * o_ref[1, sl, :] + e3 * o_ref[2, sl, :]) / (e1 + e2 + e3)
        out_ref[sl, :] = mix.astype(out_ref.dtype)


def _dilated_mixture_attention(proj, bsz, seq, n_heads):
    m = proj.shape[0]
    hd = A_HEAD_DIM
    t_len = ATT_TILE
    nt = seq // t_len
    slopes = 2.0 ** (-8.0 * jnp.arange(1, n_heads + 1, dtype=F32) / n_heads)
    slopes = jnp.broadcast_to(slopes[:, None, None], (n_heads, 1, 2 * BAND_BLOCK))

    kv_rows = sum(_kv_rows(dil) for _, dil in DILATED_BRANCHES)

    def head_block(col):
        return pl.BlockSpec((t_len, hd), lambda b, h, t: (b * nt + t, col * n_heads + h))

    return pl.pallas_call(
        _attn_kernel,
        grid=(bsz, n_heads, nt),
        in_specs=[head_block(0), head_block(1), head_block(2),
                  pl.BlockSpec((1, 1, 2 * BAND_BLOCK), lambda b, h, t: (h, 0, 0))],
        out_specs=pl.BlockSpec((t_len, hd), lambda b, h, t: (b * nt + t, h)),
        out_shape=jax.ShapeDtypeStruct((m, n_heads * hd), BF16),
        scratch_shapes=[pltpu.VMEM((3, t_len, hd), F32),
                        pltpu.VMEM((3, t_len, hd), F32),
                        pltpu.VMEM((t_len, hd), BF16),
                        pltpu.VMEM((kv_rows, hd), BF16),
                        pltpu.VMEM((kv_rows, hd), BF16),
                        pltpu.VMEM((3, t_len, hd), F32),
                        pltpu.VMEM((3, t_len, hd), F32),
                        pltpu.VMEM((3, BAND_BLOCK, 2 * BAND_BLOCK), F32),
                        pltpu.VMEM((ATT_UNROLL * BAND_BLOCK, 2 * BAND_BLOCK), F32),
                        pltpu.VMEM((ATT_UNROLL * BAND_BLOCK, 2 * BAND_BLOCK), BF16)],
        compiler_params=_params(("parallel", "parallel", "arbitrary")),
        name="dilated_attention",
    )(proj, proj, proj, slopes)


def _conv_tile(has_history, a_ref, g_ref, ha_ref, hg_ref, w_ref, cb_ref, lg_ref, lb_ref, o_ref, h_ref, hs_ref, c_ref,
               n_parts, side_work):
    ts, ch = a_ref.shape
    hist = ha_ref[...].astype(F32) * jax.nn.sigmoid(hg_ref[...].astype(F32))
    h_ref[0:CONV_HALO, :] = jnp.where(has_history, hist, 0.0)
    h_ref[CONV_HALO:, :] = a_ref[...].astype(F32) * jax.nn.sigmoid(g_ref[...].astype(F32))
    first = CONV_HALO - (CONV_WIDTH - 1)
    h_rows = h_ref.shape[0]
    for s in range(1, SUBLANES):
        hs_ref[s - 1, SUBLANES - s:SUBLANES - s + h_rows, :] = h_ref[...]
    rows, lanes = 32, 512
    groups = rows // SUBLANES
    part_rows = ts // n_parts

    def part(p, carry):
        base = pl.multiple_of(p * part_rows, part_rows)
        for r0 in range(0, part_rows, rows):
            for c0 in range(0, ch, lanes):
                acc = jnp.broadcast_to(cb_ref[:, c0:c0 + lanes], (groups, SUBLANES, lanes))
                for j in range(CONV_WIDTH):
                    a8, s = (first + j) // SUBLANES * SUBLANES, (first + j) % SUBLANES
                    if s == 0:
                        tap = h_ref[pl.ds(base + r0 + a8, rows), c0:c0 + lanes]
                    else:
                        tap = hs_ref[s - 1, pl.ds(base + SUBLANES + r0 + a8, rows), c0:c0 + lanes]
                    acc = acc + w_ref[j, :, c0:c0 + lanes] * tap.reshape(groups, SUBLANES, lanes)
                c_ref[pl.ds(base + r0, rows), c0:c0 + lanes] = acc.reshape(rows, lanes)
        side_work(p)
        return carry

    lax.fori_loop(0, n_parts, part, 0)
    y = _layer_norm(c_ref[...], lg_ref[...], lb_ref[...])
    o_ref[...] = (y * jax.nn.sigmoid(y)).astype(o_ref.dtype)


def _qkv_conv_kernel(n_row_tiles, tiles_per_seq, x_ref, w_ref, s_ref, a_ref, g_ref, ha_ref, hg_ref, cw_ref, cb_ref,
                     lg_ref, lb_ref, o_ref, conv_ref, wb_ref, h_ref, hs_ref, c_ref):
    j, i = pl.program_id(0), pl.program_id(1)

    @pl.when(i == 0)
    def _():
        wb_ref[...] = w_ref[...].astype(BF16)

    conv_tile = j * n_row_tiles + i
    rm = x_ref.shape[0] // QKV_ROW_CHUNKS

    def matmul_chunk(c):
        r0 = pl.multiple_of(c * rm, rm)
        acc = jnp.dot(x_ref[pl.ds(r0, rm), :], wb_ref[...], preferred_element_type=F32)
        o_ref[pl.ds(r0, rm), :] = (acc * s_ref[...]).astype(o_ref.dtype)

    _conv_tile(conv_tile % tiles_per_seq != 0, a_ref, g_ref, ha_ref, hg_ref, cw_ref, cb_ref, lg_ref, lb_ref,
               conv_ref, h_ref, hs_ref, c_ref, QKV_ROW_CHUNKS, matmul_chunk)


def _qkv_proj_conv(x, w_stack, layer, n_cols, col_scale, u, seq, conv_w, conv_b, ln_g, ln_b, tm=1024, tn=768):
    m, k = x.shape
    ch = conv_w.shape[1]
    n_col_blocks, n_row_tiles = n_cols // tn, m // tm
    ts = m // (n_col_blocks * n_row_tiles)
    assert n_cols % tn == 0 and m % tm == 0 and ts % CONV_HALO == 0 and seq % ts == 0
    per = ts // CONV_HALO
    w_rows = jnp.broadcast_to(conv_w[:, None, :], (CONV_WIDTH, SUBLANES, ch))

    def tile(col):
        return pl.BlockSpec((ts, ch), lambda j, i: (j * n_row_tiles + i, col))

    def halo(col):
        return pl.BlockSpec((CONV_HALO, ch), lambda j, i: (jnp.maximum((j * n_row_tiles + i) * per - 1, 0), col))

    def const(shape):
        return pl.BlockSpec(shape, lambda j, i: (0,) * len(shape))

    return pl.pallas_call(
        functools.partial(_qkv_conv_kernel, n_row_tiles, seq // ts),
        grid=(n_col_blocks, n_row_tiles),
        in_specs=[pl.BlockSpec((tm, k), lambda j, i: (i, 0)),
                  pl.BlockSpec((None, k, tn), lambda j, i: (layer, 0, j)),
                  pl.BlockSpec((1, tn), lambda j, i: (0, j)),
                  tile(0), tile(1), halo(0), halo(1),
                  const((CONV_WIDTH, SUBLANES, ch)), const((1, ch)), const((1, ch)), const((1, ch))],
        out_specs=[pl.BlockSpec((tm, tn), lambda j, i: (i, j)),
                   pl.BlockSpec((ts, ch), lambda j, i: (j * n_row_tiles + i, 0))],
        out_shape=[jax.ShapeDtypeStruct((m, n_cols), BF16), jax.ShapeDtypeStruct((m, ch), BF16)],
        scratch_shapes=[pltpu.VMEM((k, tn), BF16),
                        pltpu.VMEM((ts + CONV_HALO, ch), F32),
                        pltpu.VMEM((SUBLANES - 1, ts + CONV_HALO + SUBLANES, ch), F32),
                        pltpu.VMEM((ts, ch), F32)],
        compiler_params=_params(("parallel", "arbitrary")),
        name="qkv_proj_conv",
    )(x, w_stack, col_scale, u, u, u, u, w_rows, conv_b.reshape(1, ch), ln_g.reshape(1, ch), ln_b.reshape(1, ch))


def _gates_kernel(x_ref, w_ref, bias_ref, gc_ref, gr_ref):
    length = MLSTM_CHUNK
    w = w_ref[...].astype(BF16)
    w = jnp.concatenate([w, jnp.zeros((GATE_LANES - w.shape[0], w.shape[1]), BF16)], axis=0)
    pre = lax.dot_general(x_ref[...], w, (((1,), (1,)), ((), ())), preferred_element_type=F32) + bias_ref[...]
    log_f = jnp.minimum(pre, 0.0) - jnp.log1p(jnp.exp(-jnp.abs(pre)))
    ti = lax.broadcasted_iota(jnp.int32, (length, length), 0)
    si = lax.broadcasted_iota(jnp.int32, (length, length), 1)
    tril = (ti >= si).astype(F32)
    lane = lax.broadcasted_iota(jnp.int32, (length, GATE_LANES), 1)
    for r0 in range(0, x_ref.shape[0], length):
        cum_f = jnp.dot(tril, log_f[r0:r0 + length], preferred_element_type=F32, precision=lax.Precision.HIGHEST)
        gates = jnp.where(lane < C_HEADS, pre[r0:r0 + length], cum_f)
        gc_ref[r0:r0 + length, :] = gates
        gr_ref[:, r0:r0 + length] = gates.T


def _mlstm_gates(x, w_stack_t, layer, gate_row0, bias):
    m, d = x.shape
    length = GATE_CHUNKS_PER_STEP * MLSTM_CHUNK
    gate_rows = 2 * C_HEADS
    gate_block = gate_row0 // gate_rows
    return pl.pallas_call(
        _gates_kernel,
        grid=(m // length,),
        in_specs=[pl.BlockSpec((length, d), lambda i: (i, 0)),
                  pl.BlockSpec((None, gate_rows, d), lambda i: (layer, gate_block, 0)),
                  pl.BlockSpec((1, GATE_LANES), lambda i: (0, 0))],
        out_specs=[pl.BlockSpec((length, GATE_LANES), lambda i: (i, 0)),
                   pl.BlockSpec((GATE_LANES, length), lambda i: (0, i))],
        out_shape=[jax.ShapeDtypeStruct((m, GATE_LANES), F32),
                   jax.ShapeDtypeStruct((GATE_LANES, m), F32)],
        compiler_params=_params(("parallel",)),
        name="mlstm_gates",
    )(x, w_stack_t, bias)


def _mlstm_kernel(dk, dv, q_ref, k_ref, v_ref, o_ref, gc_ref, gr_ref, ng_ref, y_ref, c_ref, n_ref, m_ref):
    length = q_ref.shape[0]

    @pl.when(pl.program_id(1) == 0)
    def _():
        c_ref[...] = jnp.zeros_like(c_ref)
        n_ref[...] = jnp.zeros_like(n_ref)
        m_ref[...] = jnp.zeros_like(m_ref)

    ti = lax.broadcasted_iota(jnp.int32, (length, length), 0)
    si = lax.broadcasted_iota(jnp.int32, (length, length), 1)
    causal = ti >= si
    for h in range(C_HEADS):
        q = q_ref[:, h * dk:(h + 1) * dk]
        k = k_ref[:, h * dk:(h + 1) * dk]
        v = v_ref[:, h * dv:(h + 1) * dv]
        i_row = gr_ref[h:h + 1, :]
        b_row = gr_ref[C_HEADS + h:C_HEADS + h + 1, :]
        i_col = gc_ref[:, h:h + 1]
        b_col = gc_ref[:, C_HEADS + h:C_HEADS + h + 1]
        m_prev = m_ref[h:h + 1, 0:1]
        c_prev = c_ref[h]
        n_prev = n_ref[h:h + 1, :]

        dmat = jnp.where(causal, b_col - b_row + i_row, -jnp.inf)
        inter = b_col + m_prev
        m_t = jnp.maximum(inter, jnp.max(dmat, axis=-1, keepdims=True))
        w = jnp.exp(dmat - m_t)
        a = jnp.exp(inter - m_t)
        qk = lax.dot_general(q, k, (((1,), (1,)), ((), ())), preferred_element_type=F32)
        sqk = qk * w
        num = (a * jnp.dot(q, c_prev.astype(BF16), preferred_element_type=F32)
               + jnp.dot(sqk.astype(BF16), v, preferred_element_type=F32))
        den = (a * jnp.sum(q.astype(F32) * n_prev, axis=-1, keepdims=True)
               + jnp.sum(sqk, axis=-1, keepdims=True))
        hh = num / jnp.maximum(jnp.abs(den), jnp.exp(-m_t))

        b_last = b_col[length - 1:length, :]
        g = b_last - b_col + i_col
        m_new = jnp.maximum(b_last + m_prev, jnp.max(g, axis=0, keepdims=True))
        decay = jnp.exp(b_last + m_prev - m_new)
        kw = k.astype(F32) * jnp.exp(g - m_new)
        c_ref[h] = decay * c_prev + lax.dot_general(kw.astype(BF16), v, (((0,), (0,)), ((), ())),
                                                    preferred_element_type=F32)
        n_ref[h:h + 1, :] = decay * n_prev + jnp.sum(kw, axis=0, keepdims=True)
        m_ref[h:h + 1, :] = jnp.broadcast_to(m_new, (1, m_ref.shape[1]))

        mu = jnp.mean(hh, axis=-1, keepdims=True)
        hc = hh - mu
        var = jnp.mean(hc * hc, axis=-1, keepdims=True)
        hn = hc * lax.rsqrt(var + LN_EPS) * ng_ref[:, h * dv:(h + 1) * dv]
        gate = jax.nn.sigmoid(o_ref[:, h * dv:(h + 1) * dv].astype(F32))
        y_ref[:, h * dv:(h + 1) * dv] = (gate * hn).astype(y_ref.dtype)


def _mlstm(qkv, o, gates_col, gates_row, norm_g, bsz, seq, dk, dv):
    length = MLSTM_CHUNK
    nc = seq // length
    qk_w = C_HEADS * dk
    v_w = C_HEADS * dv
    v_blk = (2 * qk_w) // v_w
    return pl.pallas_call(
        functools.partial(_mlstm_kernel, dk, dv),
        grid=(bsz, nc),
        in_specs=[pl.BlockSpec((length, qk_w), lambda b, c: (b * nc + c, 0)),
                  pl.BlockSpec((length, qk_w), lambda b, c: (b * nc + c, 1)),
                  pl.BlockSpec((length, v_w), lambda b, c: (b * nc + c, v_blk)),
                  pl.BlockSpec((length, v_w), lambda b, c: (b * nc + c, 0)),
                  pl.BlockSpec((length, GATE_LANES), lambda b, c: (b * nc + c, 0)),
                  pl.BlockSpec((GATE_LANES, length), lambda b, c: (0, b * nc + c)),
                  pl.BlockSpec((1, v_w), lambda b, c: (0, 0))],
        out_specs=pl.BlockSpec((length, v_w), lambda b, c: (b * nc + c, 0)),
        out_shape=jax.ShapeDtypeStruct((bsz * seq, v_w), BF16),
        scratch_shapes=[pltpu.VMEM((C_HEADS, dk, dv), F32),
                        pltpu.VMEM((8, dk), F32),
                        pltpu.VMEM((8, GATE_LANES), F32)],
        compiler_params=_params(("parallel", "arbitrary")),
        name="mlstm_chunk",
    )(qkv, qkv, qkv, o, gates_col, gates_row, norm_g)


def _even_mixer(xb, x, w_in, j, conv_w, conv_b, conv_ln_g, conv_ln_b, w_out, ln_g, ln_b, alpha, bsz, seq):
    d = x.shape[1]
    a_width = d // 2
    n_heads = a_width // A_HEAD_DIM
    in_width = w_in.shape[2]
    qkv_w = 3 * a_width
    qkv_scale = jnp.ones((1, qkv_w), F32).at[:, :a_width].set(A_HEAD_DIM ** -0.5)
    u = _matmul(xb, w_in, j, qkv_w, in_width - qkv_w, jnp.ones((1, in_width - qkv_w), F32), BF16)
    qkv, conv = _qkv_proj_conv(xb, w_in, j, qkv_w, qkv_scale, u, seq, conv_w, conv_b, conv_ln_g, conv_ln_b)
    att = _dilated_mixture_attention(qkv, bsz, seq, n_heads)
    return _out_proj_ln([att, conv], w_out, j, x, ln_g.reshape(1, d), ln_b.reshape(1, d), alpha)


def _odd_mixer(xb, x, w_in_t, j, igate_b, fgate_b, norm_g, w_out, ln_g, ln_b, alpha, bsz, seq):
    d = x.shape[1]
    v_w = d
    dv = v_w // C_HEADS
    dk = dv // 2
    qk_w = C_HEADS * dk
    qkv_w = 2 * qk_w + v_w
    qkv_scale = jnp.ones((1, qkv_w), F32).at[:, :qk_w].set(dk ** -0.5)
    qkv = _matmul(xb, w_in_t, j, 0, qkv_w, qkv_scale, BF16, w_is_transposed=True)
    o = _matmul(xb, w_in_t, j, qkv_w, v_w, jnp.ones((1, v_w), F32), BF16, w_is_transposed=True)
    bias = jnp.zeros((1, GATE_LANES), F32).at[0, :C_HEADS].set(igate_b).at[0, C_HEADS:2 * C_HEADS].set(fgate_b)
    gates_col, gates_row = _mlstm_gates(xb, w_in_t, j, qkv_w + v_w, bias)
    y = _mlstm(qkv, o, gates_col, gates_row, norm_g.reshape(1, v_w), bsz, seq, dk, dv)
    return _out_proj_ln([y], w_out, j, x, ln_g.reshape(1, d), ln_b.reshape(1, d), alpha)


def kernel(x, even_w_in, even_conv_w, even_conv_b, even_conv_ln_g, even_conv_ln_b, even_w_out,
           odd_w_in, odd_igate_b, odd_fgate_b, odd_norm_g, odd_w_out, mix_ln_g, mix_ln_b,
           ffn_w1, ffn_w2, ffn_ln_g, ffn_ln_b):
    bsz, seq, d = x.shape
    depth = mix_ln_g.shape[0]
    alpha = (2 * depth) ** 0.25
    xf = x.reshape(bsz * seq, d)
    xb = xf.astype(BF16)
    odd_w_in_t = jnp.swapaxes(odd_w_in, 1, 2)
    for layer in range(depth):
        j = layer // 2
        if layer % 2 == 0:
            xf, xb = _even_mixer(xb, xf, even_w_in, j, even_conv_w[j], even_conv_b[j], even_conv_ln_g[j],
                                 even_conv_ln_b[j], even_w_out, mix_ln_g[layer], mix_ln_b[layer],
                                 alpha, bsz, seq)
        else:
            xf, xb = _odd_mixer(xb, xf, odd_w_in_t, j, odd_igate_b[j], odd_fgate_b[j], odd_norm_g[j],
                                odd_w_out, mix_ln_g[layer], mix_ln_b[layer], alpha, bsz, seq)
        xf, xb = _ffn_ln(xf, ffn_w1, ffn_w2, layer,
                         ffn_ln_g[layer].reshape(1, d), ffn_ln_b[layer].reshape(1, d), alpha)
    return xf.reshape(bsz, seq, d)
```

```python
import functools

import jax
import jax.numpy as jnp
from jax import lax
from jax.experimental import pallas as pl
from jax.experimental.pallas import tpu as pltpu

LN_EPS = 1e-5
DILATED_BRANCHES = ((128, 1), (512, 4), (2048, 16))
BAND_BLOCK = 128
A_HEAD_DIM = 128
ATT_TILE = 2048
ATT_UNROLL = 16
CONV_WIDTH = 31
CONV_HALO = 32
C_HEADS = 4
MLSTM_CHUNK = 256
GATE_LANES = 128
GATE_CHUNKS_PER_STEP = 4
SUBLANES = 8
FFN_DMA_CHUNKS = 4
QKV_ROW_CHUNKS = 2
V7X_VMEM_LIMIT = 56 * 1024 * 1024
V7X_VMEM_LIMIT_FFN = 58 * 1024 * 1024

BF16 = jnp.bfloat16
F32 = jnp.float32


def _params(semantics, vmem_limit=V7X_VMEM_LIMIT):
    return pltpu.CompilerParams(dimension_semantics=semantics, vmem_limit_bytes=vmem_limit)


def _layer_norm(z, g, b):
    mu = jnp.mean(z, axis=-1, keepdims=True)
    zc = z - mu
    var = jnp.mean(zc * zc, axis=-1, keepdims=True)
    return zc * lax.rsqrt(var + LN_EPS) * g + b


def _matmul_kernel(w_is_transposed, x_ref, w_ref, s_ref, o_ref, wb_ref):
    @pl.when(pl.program_id(1) == 0)
    def _():
        w = w_ref[...]
        wb_ref[...] = (w.T if w_is_transposed else w).astype(BF16)

    acc = jnp.dot(x_ref[...], wb_ref[...], preferred_element_type=F32)
    o_ref[...] = (acc * s_ref[...]).astype(o_ref.dtype)


def _matmul(x, w_stack, layer, col0, n_cols, col_scale, out_dtype, w_is_transposed=False, tm=1024, tn=1024):
    m, k = x.shape
    cb0 = col0 // tn
    if w_is_transposed:
        w_spec = pl.BlockSpec((None, tn, k), lambda j, i: (layer, cb0 + j, 0))
    else:
        w_spec = pl.BlockSpec((None, k, tn), lambda j, i: (layer, 0, cb0 + j))
    return pl.pallas_call(
        functools.partial(_matmul_kernel, w_is_transposed),
        grid=(n_cols // tn, m // tm),
        in_specs=[pl.BlockSpec((tm, k), lambda j, i: (i, 0)),
                  w_spec,
                  pl.BlockSpec((1, tn), lambda j, i: (0, j))],
        out_specs=pl.BlockSpec((tm, tn), lambda j, i: (i, j)),
        out_shape=jax.ShapeDtypeStruct((m, n_cols), out_dtype),
        scratch_shapes=[pltpu.VMEM((k, tn), BF16)],
        compiler_params=_params(("parallel", "arbitrary")),
        name="proj_matmul",
    )(x, w_stack, col_scale)


def _out_proj_ln_kernel(alpha, n_in, *refs):
    ys = refs[:n_in]
    w_ref, x_ref, g_ref, b_ref, of_ref, ob_ref, wb_ref = refs[n_in:]

    @pl.when(pl.program_id(0) == 0)
    def _():
        wb_ref[...] = w_ref[...].astype(BF16)

    acc = alpha * x_ref[...]
    k0 = 0
    for y_ref in ys:
        kw = y_ref.shape[1]
        acc = acc + jnp.dot(y_ref[...], wb_ref[k0:k0 + kw, :], preferred_element_type=F32)
        k0 += kw
    out = _layer_norm(acc, g_ref[...], b_ref[...])
    of_ref[...] = out
    ob_ref[...] = out.astype(BF16)


def _out_proj_ln(ys, w_stack, layer, x, g, b, alpha, tm=512):
    m, d = x.shape
    n_in = len(ys)
    k = w_stack.shape[1]
    in_specs = ([pl.BlockSpec((tm, y.shape[1]), lambda i: (i, 0)) for y in ys]
                + [pl.BlockSpec((None, k, d), lambda i: (layer, 0, 0), pipeline_mode=pl.Buffered(1)),
                   pl.BlockSpec((tm, d), lambda i: (i, 0)),
                   pl.BlockSpec((1, d), lambda i: (0, 0)),
                   pl.BlockSpec((1, d), lambda i: (0, 0))])
    return pl.pallas_call(
        functools.partial(_out_proj_ln_kernel, alpha, n_in),
        grid=(m // tm,),
        in_specs=in_specs,
        out_specs=[pl.BlockSpec((tm, d), lambda i: (i, 0)),
                   pl.BlockSpec((tm, d), lambda i: (i, 0))],
        out_shape=[jax.ShapeDtypeStruct((m, d), F32), jax.ShapeDtypeStruct((m, d), BF16)],
        scratch_shapes=[pltpu.VMEM((k, d), BF16)],
        compiler_params=_params(("arbitrary",)),
        name="out_proj_ln",
    )(*ys, w_stack, x, g, b)


def _ffn_kernel(alpha, layer, tm, th, n_blocks, x_hbm, w1_hbm, w2_hbm, g_ref, b_ref, of_hbm, ob_hbm,
                x_buf, xb_ref, acc_ref, of_buf, ob_buf, w1_buf, w2_buf, sem_x, sem_w, sem_o):
    i = pl.program_id(0)
    last_tile = pl.num_programs(0) - 1
    chunks = FFN_DMA_CHUNKS

    def w_copies(j, slot):
        col = pl.multiple_of(j * th, th)
        r1 = w1_buf.shape[1] // chunks
        r2 = th // chunks
        copies = []
        for c in range(chunks):
            copies.append(pltpu.make_async_copy(
                w1_hbm.at[layer, pl.ds(c * r1, r1), pl.ds(col, th)],
                w1_buf.at[slot, pl.ds(c * r1, r1), :], sem_w.at[0, slot, c]))
            copies.append(pltpu.make_async_copy(
                w2_hbm.at[layer, pl.ds(col + c * r2, r2), :],
                w2_buf.at[slot, pl.ds(c * r2, r2), :], sem_w.at[1, slot, c]))
        return copies

    def x_copies(tile):
        rows = tm // chunks
        row0 = pl.multiple_of(tile * tm, tm)
        return [pltpu.make_async_copy(x_hbm.at[pl.ds(row0 + c * rows, rows), :],
                                      x_buf.at[pl.ds(c * rows, rows), :], sem_x.at[c]) for c in range(chunks)]

    def out_copies(tile):
        rows = tm // chunks
        row0 = pl.multiple_of(tile * tm, tm)
        copies = []
        for c in range(chunks):
            copies.append(pltpu.make_async_copy(of_buf.at[pl.ds(c * rows, rows), :],
                                                of_hbm.at[pl.ds(row0 + c * rows, rows), :], sem_o.at[0, c]))
            copies.append(pltpu.make_async_copy(ob_buf.at[pl.ds(c * rows, rows), :],
                                                ob_hbm.at[pl.ds(row0 + c * rows, rows), :], sem_o.at[1, c]))
        return copies

    def start(copies):
        for copy in copies:
            copy.start()

    def wait(copies):
        for copy in copies:
            copy.wait()

    @pl.when(i == 0)
    def _():
        start(x_copies(0))
        start(w_copies(0, 0))

    wait(x_copies(i))
    x = x_buf[...]
    xb_ref[...] = x.astype(BF16)
    acc_ref[...] = alpha * x

    @pl.when(i < last_tile)
    def _():
        start(x_copies(i + 1))

    def block(j, slot):
        wait(w_copies(j, slot))
        h = jnp.dot(xb_ref[...], w1_buf[slot].astype(BF16), preferred_element_type=F32)
        h = jnp.maximum(h, 0.0)
        h = (h * h).astype(BF16)
        acc_ref[...] += jnp.dot(h, w2_buf[slot].astype(BF16), preferred_element_type=F32)

    def pair(p, carry):
        j = 2 * p
        start(w_copies(j + 1, 1))
        block(j, 0)

        @pl.when(j + 2 < n_blocks)
        def _():
            start(w_copies(j + 2, 0))

        @pl.when(jnp.logical_and(j + 2 == n_blocks, i < last_tile))
        def _():
            start(w_copies(0, 0))

        block(j + 1, 1)
        return carry

    lax.fori_loop(0, n_blocks // 2, pair, 0)

    @pl.when(i > 0)
    def _():
        wait(out_copies(i - 1))

    out = _layer_norm(acc_ref[...], g_ref[...], b_ref[...])
    of_buf[...] = out
    ob_buf[...] = out.astype(BF16)
    start(out_copies(i))

    @pl.when(i == last_tile)
    def _():
        wait(out_copies(i))


def _ffn_ln(x, w1, w2, layer, g, b, alpha, tm=1024, th=512):
    m, d = x.shape
    f = w1.shape[2]
    n_blocks = f // th
    assert n_blocks % 2 == 0 and m % tm == 0
    hbm = pl.BlockSpec(memory_space=pl.ANY)
    return pl.pallas_call(
        functools.partial(_ffn_kernel, alpha, layer, tm, th, n_blocks),
        grid=(m // tm,),
        in_specs=[hbm, hbm, hbm,
                  pl.BlockSpec((1, d), lambda i: (0, 0)),
                  pl.BlockSpec((1, d), lambda i: (0, 0))],
        out_specs=[hbm, hbm],
        out_shape=[jax.ShapeDtypeStruct((m, d), F32), jax.ShapeDtypeStruct((m, d), BF16)],
        scratch_shapes=[pltpu.VMEM((tm, d), F32),
                        pltpu.VMEM((tm, d), BF16),
                        pltpu.VMEM((tm, d), F32),
                        pltpu.VMEM((tm, d), F32),
                        pltpu.VMEM((tm, d), BF16),
                        pltpu.VMEM((2, d, th), F32),
                        pltpu.VMEM((2, th, d), F32),
                        pltpu.SemaphoreType.DMA((FFN_DMA_CHUNKS,)),
                        pltpu.SemaphoreType.DMA((2, 2, FFN_DMA_CHUNKS)),
                        pltpu.SemaphoreType.DMA((2, FFN_DMA_CHUNKS))],
        compiler_params=_params(("arbitrary",), V7X_VMEM_LIMIT_FFN),
        name="ffn_ln",
    )(x, w1, w2, g, b)


def _kv_rows(dil):
    return dil * BAND_BLOCK + ATT_TILE


def _attn_kernel(q_ref, k_ref, v_ref, slope_ref, out_ref,
                 f_ref, g4_ref, qs_ref, ks_ref, vs_ref, o_ref, l_ref, bias_ref, s_ref, e_ref):
    tile = pl.program_id(2)
    p = BAND_BLOCK
    t_len = ATT_TILE
    qi = lax.broadcasted_iota(jnp.int32, (p, 2 * p), 0)
    kj = lax.broadcasted_iota(jnp.int32, (p, 2 * p), 1)
    dist = p + qi - kj
    valid = jnp.logical_and(dist >= 0, dist <= p)
    in_prev_block = kj < p
    slope = slope_ref[0]
    for bi, (_, dil) in enumerate(DILATED_BRANCHES):
        bias_ref[bi] = jnp.where(valid, -(slope * (dist * dil).astype(F32)), -jnp.inf)

    def rows(start, size, dil):
        return pl.ds(start, size, stride=dil) if dil > 1 else pl.ds(start, size)

    kv_base = [sum(_kv_rows(d) for _, d in DILATED_BRANCHES[:bi]) for bi in range(len(DILATED_BRANCHES))]
    srcs = (q_ref, k_ref, v_ref)
    dil4 = DILATED_BRANCHES[1][1]
    w4 = t_len // dil4
    for x, src in enumerate(srcs):
        f_ref[x] = src[...].astype(F32)
        for r in range(dil4):
            g4_ref[x, r * w4:(r + 1) * w4, :] = f_ref[x, rows(r, w4, dil4), :]

    def residue_rows(x, r, dil):
        if dil == 1:
            return srcs[x][...]
        if dil == dil4:
            return g4_ref[x, r * w4:(r + 1) * w4, :]
        return g4_ref[x, rows((r % dil4) * w4 + r // dil4, t_len // dil, dil // dil4), :]

    nt = (((1,), (1,)), ((), ()))
    for bi, (_, dil) in enumerate(DILATED_BRANCHES):
        wq = t_len // dil
        nq = wq // p
        wk = p + wq
        base = kv_base[bi]

        @pl.when(tile == 0)
        def _(dil=dil, wk=wk, base=base):
            for r in range(dil):
                for dst_ref in (ks_ref, vs_ref):
                    dst_ref[base + r * wk:base + r * wk + p, :] = jnp.zeros((p, A_HEAD_DIM), BF16)

        @pl.when(tile > 0)
        def _(dil=dil, wk=wk, wq=wq, base=base):
            for r in range(dil):
                for dst_ref in (ks_ref, vs_ref):
                    dst_ref[base + r * wk:base + r * wk + p, :] = dst_ref[base + r * wk + wq:base + (r + 1) * wk, :]

        for r in range(dil):
            qs_ref[r * wq:(r + 1) * wq, :] = residue_rows(0, r, dil).astype(BF16)
            ks_ref[base + r * wk + p:base + (r + 1) * wk, :] = residue_rows(1, r, dil).astype(BF16)
            vs_ref[base + r * wk + p:base + (r + 1) * wk, :] = residue_rows(2, r, dil).astype(BF16)

        def group(g, carry, dil=dil, nq=nq, wk=wk, bi=bi, base=base):
            starts = []
            for u in range(ATT_UNROLL):
                idx = g * ATT_UNROLL + u
                r = idx // nq
                n = idx - r * nq
                q0 = pl.multiple_of(idx * p, p)
                k0 = pl.multiple_of(base + r * wk + n * p, p)
                s = lax.dot_general(qs_ref[pl.ds(q0, p), :], ks_ref[pl.ds(k0, 2 * p), :], nt,
                                    preferred_element_type=F32) + bias_ref[bi]
                no_history = jnp.logical_and(tile == 0, n == 0)
                s_ref[u * p:(u + 1) * p, :] = jnp.where(jnp.logical_and(no_history, in_prev_block), -jnp.inf, s)
                starts.append((k0, r + dil * p * n))
            s = s_ref[...]
            m = jnp.max(s, axis=-1, keepdims=True)
            e = jnp.exp(s - m)
            l = jnp.sum(e, axis=-1, keepdims=True)
            e_ref[...] = e.astype(BF16)
            inv_l = 1.0 / l
            lse = m + jnp.log(l)
            for u, (k0, t0) in enumerate(starts):
                o = jnp.dot(e_ref[u * p:(u + 1) * p, :], vs_ref[pl.ds(k0, 2 * p), :], preferred_element_type=F32)
                o_ref[bi, rows(t0, p, dil), :] = o * inv_l[u * p:(u + 1) * p]
                l_ref[bi, rows(t0, p, dil), :] = jnp.broadcast_to(lse[u * p:(u + 1) * p], (p, A_HEAD_DIM))
            return carry

        lax.fori_loop(0, (dil * nq) // ATT_UNROLL, group, 0)

    chunk = 256
    for c0 in range(0, t_len, chunk):
        sl = slice(c0, c0 + chunk)
        l1, l2, l3 = l_ref[0, sl, :], l_ref[1, sl, :], l_ref[2, sl, :]
        mx = jnp.maximum(jnp.maximum(l1, l2), l3)
        e1, e2, e3 = jnp.exp(l1 - mx), jnp.exp(l2 - mx), jnp.exp(l3 - mx)
        mix = (e1 * o_ref[0, sl, :] + e2 * o_ref[1, sl, :] + e3 * o_ref[2, sl, :]) / (e1 + e2 + e3)
        out_ref[sl, :] = mix.astype(out_ref.dtype)


def _dilated_mixture_attention(proj, bsz, seq, n_heads):
    m = proj.shape[0]
    hd = A_HEAD_DIM
    t_len = ATT_TILE
    nt = seq // t_len
    slopes = 2.0 ** (-8.0 * jnp.arange(1, n_heads + 1, dtype=F32) / n_heads)
    slopes = jnp.broadcast_to(slopes[:, None, None], (n_heads, 1, 2 * BAND_BLOCK))

    kv_rows = sum(_kv_rows(dil) for _, dil in DILATED_BRANCHES)

    def head_block(col):
        return pl.BlockSpec((t_len, hd), lambda b, h, t: (b * nt + t, col * n_heads + h))

    return pl.pallas_call(
        _attn_kernel,
        grid=(bsz, n_heads, nt),
        in_specs=[head_block(0), head_block(1), head_block(2),
                  pl.BlockSpec((1, 1, 2 * BAND_BLOCK), lambda b, h, t: (h, 0, 0))],
        out_specs=pl.BlockSpec((t_len, hd), lambda b, h, t: (b * nt + t, h)),
        out_shape=jax.ShapeDtypeStruct((m, n_heads * hd), BF16),
        scratch_shapes=[pltpu.VMEM((3, t_len, hd), F32),
                        pltpu.VMEM((3, t_len, hd), F32),
                        pltpu.VMEM((t_len, hd), BF16),
                        pltpu.VMEM((kv_rows, hd), BF16),
                        pltpu.VMEM((kv_rows, hd), BF16),
                        pltpu.VMEM((3, t_len, hd), F32),
                        pltpu.VMEM((3, t_len, hd), F32),
                        pltpu.VMEM((3, BAND_BLOCK, 2 * BAND_BLOCK), F32),
                        pltpu.VMEM((ATT_UNROLL * BAND_BLOCK, 2 * BAND_BLOCK), F32),
                        pltpu.VMEM((ATT_UNROLL * BAND_BLOCK, 2 * BAND_BLOCK), BF16)],
        compiler_params=_params(("parallel", "parallel", "arbitrary")),
        name="dilated_attention",
    )(proj, proj, proj, slopes)


def _conv_tile(has_history, a_ref, g_ref, ha_ref, hg_ref, w_ref, cb_ref, lg_ref, lb_ref, o_ref, h_ref, hs_ref, c_ref,
               n_parts, side_work):
    ts, ch = a_ref.shape
    hist = ha_ref[...].astype(F32) * jax.nn.sigmoid(hg_ref[...].astype(F32))
    h_ref[0:CONV_HALO, :] = jnp.where(has_history, hist, 0.0)
    h_ref[CONV_HALO:, :] = a_ref[...].astype(F32) * jax.nn.sigmoid(g_ref[...].astype(F32))
    first = CONV_HALO - (CONV_WIDTH - 1)
    h_rows = h_ref.shape[0]
    for s in range(1, SUBLANES):
        hs_ref[s - 1, SUBLANES - s:SUBLANES - s + h_rows, :] = h_ref[...]
    rows, lanes = 32, 512
    groups = rows // SUBLANES
    part_rows = ts // n_parts

    def part(p, carry):
        base = pl.multiple_of(p * part_rows, part_rows)
        for r0 in range(0, part_rows, rows):
            for c0 in range(0, ch, lanes):
                acc = jnp.broadcast_to(cb_ref[:, c0:c0 + lanes], (groups, SUBLANES, lanes))
                for j in range(CONV_WIDTH):
                    a8, s = (first + j) // SUBLANES * SUBLANES, (first + j) % SUBLANES
                    if s == 0:
                        tap = h_ref[pl.ds(base + r0 + a8, rows), c0:c0 + lanes]
                    else:
                        tap = hs_ref[s - 1, pl.ds(base + SUBLANES + r0 + a8, rows), c0:c0 + lanes]
                    acc = acc + w_ref[j, :, c0:c0 + lanes] * tap.reshape(groups, SUBLANES, lanes)
                c_ref[pl.ds(base + r0, rows), c0:c0 + lanes] = acc.reshape(rows, lanes)
        side_work(p)
        return carry

    lax.fori_loop(0, n_parts, part, 0)
    y = _layer_norm(c_ref[...], lg_ref[...], lb_ref[...])
    o_ref[...] = (y * jax.nn.sigmoid(y)).astype(o_ref.dtype)


def _qkv_conv_kernel(n_row_tiles, tiles_per_seq, x_ref, w_ref, s_ref, a_ref, g_ref, ha_ref, hg_ref, cw_ref, cb_ref,
                     lg_ref, lb_ref, o_ref, conv_ref, wb_ref, h_ref, hs_ref, c_ref):
    j, i = pl.program_id(0), pl.program_id(1)

    @pl.when(i == 0)
    def _():
        wb_ref[...] = w_ref[...].astype(BF16)

    conv_tile = j * n_row_tiles + i
    rm = x_ref.shape[0] // QKV_ROW_CHUNKS

    def matmul_chunk(c):
        r0 = pl.multiple_of(c * rm, rm)
        acc = jnp.dot(x_ref[pl.ds(r0, rm), :], wb_ref[...], preferred_element_type=F32)
        o_ref[pl.ds(r0, rm), :] = (acc * s_ref[...]).astype(o_ref.dtype)

    _conv_tile(conv_tile % tiles_per_seq != 0, a_ref, g_ref, ha_ref, hg_ref, cw_ref, cb_ref, lg_ref, lb_ref,
               conv_ref, h_ref, hs_ref, c_ref, QKV_ROW_CHUNKS, matmul_chunk)


def _qkv_proj_conv(x, w_stack, layer, n_cols, col_scale, u, seq, conv_w, conv_b, ln_g, ln_b, tm=1024, tn=768):
    m, k = x.shape
    ch = conv_w.shape[1]
    n_col_blocks, n_row_tiles = n_cols // tn, m // tm
    ts = m // (n_col_blocks * n_row_tiles)
    assert n_cols % tn == 0 and m % tm == 0 and ts % CONV_HALO == 0 and seq % ts == 0
    per = ts // CONV_HALO
    w_rows = jnp.broadcast_to(conv_w[:, None, :], (CONV_WIDTH, SUBLANES, ch))

    def tile(col):
        return pl.BlockSpec((ts, ch), lambda j, i: (j * n_row_tiles + i, col))

    def halo(col):
        return pl.BlockSpec((CONV_HALO, ch), lambda j, i: (jnp.maximum((j * n_row_tiles + i) * per - 1, 0), col))

    def const(shape):
        return pl.BlockSpec(shape, lambda j, i: (0,) * len(shape))

    return pl.pallas_call(
        functools.partial(_qkv_conv_kernel, n_row_tiles, seq // ts),
        grid=(n_col_blocks, n_row_tiles),
        in_specs=[pl.BlockSpec((tm, k), lambda j, i: (i, 0)),
                  pl.BlockSpec((None, k, tn), lambda j, i: (layer, 0, j)),
                  pl.BlockSpec((1, tn), lambda j, i: (0, j)),
                  tile(0), tile(1), halo(0), halo(1),
                  const((CONV_WIDTH, SUBLANES, ch)), const((1, ch)), const((1, ch)), const((1, ch))],
        out_specs=[pl.BlockSpec((tm, tn), lambda j, i: (i, j)),
                   pl.BlockSpec((ts, ch), lambda j, i: (j * n_row_tiles + i, 0))],
        out_shape=[jax.ShapeDtypeStruct((m, n_cols), BF16), jax.ShapeDtypeStruct((m, ch), BF16)],
        scratch_shapes=[pltpu.VMEM((k, tn), BF16),
                        pltpu.VMEM((ts + CONV_HALO, ch), F32),
                        pltpu.VMEM((SUBLANES - 1, ts + CONV_HALO + SUBLANES, ch), F32),
                        pltpu.VMEM((ts, ch), F32)],
        compiler_params=_params(("parallel", "arbitrary")),
        name="qkv_proj_conv",
    )(x, w_stack, col_scale, u, u, u, u, w_rows, conv_b.reshape(1, ch), ln_g.reshape(1, ch), ln_b.reshape(1, ch))


def _gates_kernel(x_ref, w_ref, bias_ref, gc_ref, gr_ref):
    length = MLSTM_CHUNK
    w = w_ref[...].astype(BF16)
    w = jnp.concatenate([w, jnp.zeros((GATE_LANES - w.shape[0], w.shape[1]), BF16)], axis=0)
    pre = lax.dot_general(x_ref[...], w, (((1,), (1,)), ((), ())), preferred_element_type=F32) + bias_ref[...]
    log_f = jnp.minimum(pre, 0.0) - jnp.log1p(jnp.exp(-jnp.abs(pre)))
    ti = lax.broadcasted_iota(jnp.int32, (length, length), 0)
    si = lax.broadcasted_iota(jnp.int32, (length, length), 1)
    tril = (ti >= si).astype(F32)
    lane = lax.broadcasted_iota(jnp.int32, (length, GATE_LANES), 1)
    for r0 in range(0, x_ref.shape[0], length):
        cum_f = jnp.dot(tril, log_f[r0:r0 + length], preferred_element_type=F32, precision=lax.Precision.HIGHEST)
        gates = jnp.where(lane < C_HEADS, pre[r0:r0 + length], cum_f)
        gc_ref[r0:r0 + length, :] = gates
        gr_ref[:, r0:r0 + length] = gates.T


def _mlstm_gates(x, w_stack_t, layer, gate_row0, bias):
    m, d = x.shape
    length = GATE_CHUNKS_PER_STEP * MLSTM_CHUNK
    gate_rows = 2 * C_HEADS
    gate_block = gate_row0 // gate_rows
    return pl.pallas_call(
        _gates_kernel,
        grid=(m // length,),
        in_specs=[pl.BlockSpec((length, d), lambda i: (i, 0)),
                  pl.BlockSpec((None, gate_rows, d), lambda i: (layer, gate_block, 0)),
                  pl.BlockSpec((1, GATE_LANES), lambda i: (0, 0))],
        out_specs=[pl.BlockSpec((length, GATE_LANES), lambda i: (i, 0)),
                   pl.BlockSpec((GATE_LANES, length), lambda i: (0, i))],
        out_shape=[jax.ShapeDtypeStruct((m, GATE_LANES), F32),
                   jax.ShapeDtypeStruct((GATE_LANES, m), F32)],
        compiler_params=_params(("parallel",)),
        name="mlstm_gates",
    )(x, w_stack_t, bias)


def _mlstm_kernel(dk, dv, q_ref, k_ref, v_ref, o_ref, gc_ref, gr_ref, ng_ref, y_ref, c_ref, n_ref, m_ref):
    length = q_ref.shape[0]

    @pl.when(pl.program_id(1) == 0)
    def _():
        c_ref[...] = jnp.zeros_like(c_ref)
        n_ref[...] = jnp.zeros_like(n_ref)
        m_ref[...] = jnp.zeros_like(m_ref)

    ti = lax.broadcasted_iota(jnp.int32, (length, length), 0)
    si = lax.broadcasted_iota(jnp.int32, (length, length), 1)
    causal = ti >= si
    for h in range(C_HEADS):
        q = q_ref[:, h * dk:(h + 1) * dk]
        k = k_ref[:, h * dk:(h + 1) * dk]
        v = v_ref[:, h * dv:(h + 1) * dv]
        i_row = gr_ref[h:h + 1, :]
        b_row = gr_ref[C_HEADS + h:C_HEADS + h + 1, :]
        i_col = gc_ref[:, h:h + 1]
        b_col = gc_ref[:, C_HEADS + h:C_HEADS + h + 1]
        m_prev = m_ref[h:h + 1, 0:1]
        c_prev = c_ref[h]
        n_prev = n_ref[h:h + 1, :]

        dmat = jnp.where(causal, b_col - b_row + i_row, -jnp.inf)
        inter = b_col + m_prev
        m_t = jnp.maximum(inter, jnp.max(dmat, axis=-1, keepdims=True))
        w = jnp.exp(dmat - m_t)
        a = jnp.exp(inter - m_t)
        qk = lax.dot_general(q, k, (((1,), (1,)), ((), ())), preferred_element_type=F32)
        sqk = qk * w
        num = (a * jnp.dot(q, c_prev.astype(BF16), preferred_element_type=F32)
               + jnp.dot(sqk.astype(BF16), v, preferred_element_type=F32))
        den = (a * jnp.sum(q.astype(F32) * n_prev, axis=-1, keepdims=True)
               + jnp.sum(sqk, axis=-1, keepdims=True))
        hh = num / jnp.maximum(jnp.abs(den), jnp.exp(-m_t))

        b_last = b_col[length - 1:length, :]
        g = b_last - b_col + i_col
        m_new = jnp.maximum(b_last + m_prev, jnp.max(g, axis=0, keepdims=True))
        decay = jnp.exp(b_last + m_prev - m_new)
        kw = k.astype(F32) * jnp.exp(g - m_new)
        c_ref[h] = decay * c_prev + lax.dot_general(kw.astype(BF16), v, (((0,), (0,)), ((), ())),
                                                    preferred_element_type=F32)
        n_ref[h:h + 1, :] = decay * n_prev + jnp.sum(kw, axis=0, keepdims=True)
        m_ref[h:h + 1, :] = jnp.broadcast_to(m_new, (1, m_ref.shape[1]))

        mu = jnp.mean(hh, axis=-1, keepdims=True)
        hc = hh - mu
        var = jnp.mean(hc * hc, axis=-1, keepdims=True)
        hn = hc * lax.rsqrt(var + LN_EPS) * ng_ref[:, h * dv:(h + 1) * dv]
        y_ref[:, h * dv:(h + 1) * dv] = (jax.nn.sigmoid(o_ref[:, h * dv:(h + 1) * dv]) * hn).astype(y_ref.dtype)


def _mlstm(qkv, o, gates_col, gates_row, norm_g, bsz, seq, dk, dv):
    length = MLSTM_CHUNK
    nc = seq // length
    qk_w = C_HEADS * dk
    v_w = C_HEADS * dv
    v_blk = (2 * qk_w) // v_w
    return pl.pallas_call(
        functools.partial(_mlstm_kernel, dk, dv),
        grid=(bsz, nc),
        in_specs=[pl.BlockSpec((length, qk_w), lambda b, c: (b * nc + c, 0)),
                  pl.BlockSpec((length, qk_w), lambda b, c: (b * nc + c, 1)),
                  pl.BlockSpec((length, v_w), lambda b, c: (b * nc + c, v_blk)),
                  pl.BlockSpec((length, v_w), lambda b, c: (b * nc + c, 0)),
                  pl.BlockSpec((length, GATE_LANES), lambda b, c: (b * nc + c, 0)),
                  pl.BlockSpec((GATE_LANES, length), lambda b, c: (0, b * nc + c)),
                  pl.BlockSpec((1, v_w), lambda b, c: (0, 0))],
        out_specs=pl.BlockSpec((length, v_w), lambda b, c: (b * nc + c, 0)),
        out_shape=jax.ShapeDtypeStruct((bsz * seq, v_w), BF16),
        scratch_shapes=[pltpu.VMEM((C_HEADS, dk, dv), F32),
                        pltpu.VMEM((8, dk), F32),
                        pltpu.VMEM((8, GATE_LANES), F32)],
        compiler_params=_params(("parallel", "arbitrary")),
        name="mlstm_chunk",
    )(qkv, qkv, qkv, o, gates_col, gates_row, norm_g)


def _even_mixer(xb, x, w_in, j, conv_w, conv_b, conv_ln_g, conv_ln_b, w_out, ln_g, ln_b, alpha, bsz, seq):
    d = x.shape[1]
    a_width = d // 2
    n_heads = a_width // A_HEAD_DIM
    in_width = w_in.shape[2]
    qkv_w = 3 * a_width
    qkv_scale = jnp.ones((1, qkv_w), F32).at[:, :a_width].set(A_HEAD_DIM ** -0.5)
    u = _matmul(xb, w_in, j, qkv_w, in_width - qkv_w, jnp.ones((1, in_width - qkv_w), F32), BF16)
    qkv, conv = _qkv_proj_conv(xb, w_in, j, qkv_w, qkv_scale, u, seq, conv_w, conv_b, conv_ln_g, conv_ln_b)
    att = _dilated_mixture_attention(qkv, bsz, seq, n_heads)
    return _out_proj_ln([att, conv], w_out, j, x, ln_g.reshape(1, d), ln_b.reshape(1, d), alpha)


def _odd_mixer(xb, x, w_in_t, j, igate_b, fgate_b, norm_g, w_out, ln_g, ln_b, alpha, bsz, seq):
    d = x.shape[1]
    v_w = d
    dv = v_w // C_HEADS
    dk = dv // 2
    qk_w = C_HEADS * dk
    qkv_w = 2 * qk_w + v_w
    qkv_scale = jnp.ones((1, qkv_w), F32).at[:, :qk_w].set(dk ** -0.5)
    qkv = _matmul(xb, w_in_t, j, 0, qkv_w, qkv_scale, BF16, w_is_transposed=True)
    o = _matmul(xb, w_in_t, j, qkv_w, v_w, jnp.ones((1, v_w), F32), F32, w_is_transposed=True)
    bias = jnp.zeros((1, GATE_LANES), F32).at[0, :C_HEADS].set(igate_b).at[0, C_HEADS:2 * C_HEADS].set(fgate_b)
    gates_col, gates_row = _mlstm_gates(xb, w_in_t, j, qkv_w + v_w, bias)
    y = _mlstm(qkv, o, gates_col, gates_row, norm_g.reshape(1, v_w), bsz, seq, dk, dv)
    return _out_proj_ln([y], w_out, j, x, ln_g.reshape(1, d), ln_b.reshape(1, d), alpha)


def kernel(x, even_w_in, even_conv_w, even_conv_b, even_conv_ln_g, even_conv_ln_b, even_w_out,
           odd_w_in, odd_igate_b, odd_fgate_b, odd_norm_g, odd_w_out, mix_ln_g, mix_ln_b,
           ffn_w1, ffn_w2, ffn_ln_g, ffn_ln_b):
    bsz, seq, d = x.shape
    depth = mix_ln_g.shape[0]
    alpha = (2 * depth) ** 0.25
    xf = x.reshape(bsz * seq, d)
    xb = xf.astype(BF16)
    odd_w_in_t = jnp.swapaxes(odd_w_in, 1, 2)
    for layer in range(depth):
        j = layer // 2
        if layer % 2 == 0:
            xf, xb = _even_mixer(xb, xf, even_w_in, j, even_conv_w[j], even_conv_b[j], even_conv_ln_g[j],
                                 even_conv_ln_b[j], even_w_out, mix_ln_g[layer], mix_ln_b[layer],
                                 alpha, bsz, seq)
        else:
            xf, xb = _odd_mixer(xb, xf, odd_w_in_t, j, odd_igate_b[j], odd_fgate_b[j], odd_norm_g[j],
                                odd_w_out, mix_ln_g[layer], mix_ln_b[layer], alpha, bsz, seq)
        xf, xb = _ffn_ln(xf, ffn_w1, ffn_w2, layer,
                         ffn_ln_g[layer].reshape(1, d), ffn_ln_b[layer].reshape(1, d), alpha)
    return xf.reshape(bsz, seq, d)
```
